```python
import jax, jax.numpy as jnp
from jax import lax
import numpy as np

D_MODEL = 1024
BATCH = 8
SEQ = 4096
DEPTH = 2

GRID_W = 64
CTX_LEN = 256
CONV_W = 1024
CONV_K = 3
N_HEADS = 8
N_KV_HEADS = 2
HEAD_DIM = 128
GROUP = N_HEADS // N_KV_HEADS
Q_BLOCK = 128
ROPE_THETA = 10000.0
ROPE_AXIS_DIM = HEAD_DIM // 2
ROPE_PAIRS = ROPE_AXIS_DIM // 2
ATTN_SCALE = HEAD_DIM ** -0.5
GLA_HEADS = 4
GLA_DK = D_MODEL // 2
GLA_DV = D_MODEL
GLA_DKH = GLA_DK // GLA_HEADS
GLA_DVH = GLA_DV // GLA_HEADS
GLA_RANK = 16
GLA_TAU = 16.0
GLA_CHUNK = 64
N_BRANCH = 3
EPS = 1e-6

IN_SPLITS = (CONV_W, CONV_W, CONV_W, CONV_W,
             N_HEADS * HEAD_DIM, N_KV_HEADS * HEAD_DIM, N_KV_HEADS * HEAD_DIM,
             N_HEADS * HEAD_DIM,
             GLA_DK, GLA_DK, GLA_DV, GLA_RANK, GLA_RANK, GLA_DV,
             N_BRANCH * D_MODEL)
IN_WIDTH = sum(IN_SPLITS)

kernel_name = "hybrid_conv_gqa_gla_prefix_dit"


def _rmsnorm(x, g):
    xf = x.astype(jnp.float32)
    y = xf * lax.rsqrt(jnp.mean(xf * xf, axis=-1, keepdims=True) + EPS)
    return (y * g.astype(jnp.float32)).astype(x.dtype)


def _split_proj(p):
    offsets = np.cumsum(IN_SPLITS)[:-1].tolist()
    return jnp.split(p, offsets, axis=-1)


def _short_conv(u, w):
    up = jnp.pad(u, ((0, 0), (1, 1), (0, 0)))
    return w[0] * up[:, :-2] + w[1] * up[:, 1:-1] + w[2] * up[:, 2:]


def _axial_rope_tables(n_tokens):
    n_rows = n_tokens // GRID_W
    row = jnp.repeat(jnp.arange(n_rows, dtype=jnp.float32), GRID_W)
    col = jnp.tile(jnp.arange(GRID_W, dtype=jnp.float32), n_rows)
    freqs = ROPE_THETA ** (-jnp.arange(ROPE_PAIRS, dtype=jnp.float32) * 2.0 / ROPE_AXIS_DIM)
    ang = jnp.stack([row[:, None] * freqs, col[:, None] * freqs], axis=1)
    return jnp.cos(ang), jnp.sin(ang)


def _apply_rope(x, cos, sin):
    b_, t_, h_, _ = x.shape
    xr = x.reshape(b_, t_, h_, 2, 2, ROPE_PAIRS)
    x1, x2 = xr[..., 0, :], xr[..., 1, :]
    c_, s_ = cos[None, :, None], sin[None, :, None]
    out = jnp.stack([x1 * c_ - x2 * s_, x2 * c_ + x1 * s_], axis=-2)
    return out.reshape(b_, t_, h_, HEAD_DIM).astype(x.dtype)


def _attn_heads(q, k, v, q_g, k_g, rope):
    b_, t_ = q.shape[:2]
    q = _rmsnorm(q.reshape(b_, t_, N_HEADS, HEAD_DIM), q_g)
    k = _rmsnorm(k.reshape(b_, t_, N_KV_HEADS, HEAD_DIM), k_g)
    v = v.reshape(b_, t_, N_KV_HEADS, HEAD_DIM)
    if rope is not None:
        q = _apply_rope(q, *rope)
        k = _apply_rope(k, *rope)
    return q, k, v


def _sdpa_blocks(q, keys, vals):
    b_, t_ = q.shape[:2]
    nb = t_ // Q_BLOCK
    qb = q.reshape(b_, nb, Q_BLOCK, N_KV_HEADS, GROUP, HEAD_DIM).transpose(1, 0, 2, 3, 4, 5)

    def one_block(qblk):
        s = jnp.einsum('bqkgd,bskd->bkgqs', qblk, keys).astype(jnp.float32) * ATTN_SCALE
        p = jax.nn.softmax(s, axis=-1).astype(vals.dtype)
        return jnp.einsum('bkgqs,bskd->bqkgd', p, vals)

    o = lax.map(one_block, qb)
    return o.transpose(1, 0, 2, 3, 4, 5).reshape(b_, t_, N_HEADS * HEAD_DIM)


def _gla_inputs(q, k, v, r_f, r_b, w_df, b_df, w_db, b_db):
    b_, t_ = q.shape[:2]
    q = q.reshape(b_, t_, GLA_HEADS, GLA_DKH) * (GLA_DKH ** -0.5)
    k = k.reshape(b_, t_, GLA_HEADS, GLA_DKH)
    v = v.reshape(b_, t_, GLA_HEADS, GLA_DVH)
    la_f = (jax.nn.log_sigmoid((r_f @ w_df + b_df).astype(jnp.float32)) / GLA_TAU).reshape(b_, t_, GLA_HEADS, GLA_DKH)
    la_b = (jax.nn.log_sigmoid((r_b @ w_db + b_db).astype(jnp.float32)) / GLA_TAU).reshape(b_, t_, GLA_HEADS, GLA_DKH)
    return q, k, v, la_f, la_b


def _gla_scan(q, k, v, log_a, s0):
    b_, t_, h_, _ = q.shape
    dv = v.shape[-1]
    nc = t_ // GLA_CHUNK

    def chunks(a):
        return a.astype(jnp.float32).reshape(b_, nc, GLA_CHUNK, h_, a.shape[-1]).transpose(1, 0, 3, 2, 4)

    mask = jnp.tril(jnp.ones((GLA_CHUNK, GLA_CHUNK), dtype=bool))

    def step(s, inp):
        qc, kc, vc, ac = inp
        bcum = jnp.cumsum(ac, axis=2)
        o_inter = jnp.einsum('bhtd,bhde->bhte', qc * jnp.exp(bcum), s)
        diff = bcum[:, :, :, None, :] - bcum[:, :, None, :, :]
        decay = jnp.exp(jnp.where(mask[:, :, None], diff, -jnp.inf))
        att = jnp.einsum('bhtd,bhsd,bhtsd->bhts', qc, kc, decay)
        o_intra = jnp.einsum('bhts,bhse->bhte', att, vc)
        b_last = bcum[:, :, -1:, :]
        s_new = jnp.exp(b_last[:, :, 0, :])[..., None] * s + jnp.einsum('bhsd,bhse->bhde', kc * jnp.exp(b_last - bcum), vc)
        return s_new, o_inter + o_intra

    s_f, o = lax.scan(step, s0, (chunks(q), chunks(k), chunks(v), chunks(log_a)))
    o = o.transpose(1, 0, 3, 2, 4).reshape(b_, t_, h_, dv)
    return o.astype(v.dtype), s_f


def _flip(a):
    return jnp.flip(a, axis=1)


def _gla_bidirectional(ctx_in, lat_in):
    qc, kc, vc, lfc, lbc = ctx_in
    ql, kl, vl, lfl, lbl = lat_in
    s0 = jnp.zeros((qc.shape[0], GLA_HEADS, GLA_DKH, GLA_DVH), jnp.float32)
    o_cf, s_f = _gla_scan(qc, kc, vc, lfc, s0)
    o_cb, s_b = _gla_scan(_flip(qc), _flip(kc), _flip(vc), _flip(lbc), s0)
    o_lf, _ = _gla_scan(ql, kl, vl, lfl, s_f)
    o_lb, _ = _gla_scan(_flip(ql), _flip(kl), _flip(vl), _flip(lbl), s_b)
    return o_lf + _flip(o_lb), o_cf + _flip(o_cb)


def _mixer_output(parts, att_o, gla_o, conv_w_l, gla_g_l, w_br_conv_l, w_br_attn_l, w_br_gla_l, b_gate_l, w_out_l):
    a_b, a_c, a_x, a_z = parts[0], parts[1], parts[2], parts[3]
    z_attn, z_gla, mg = parts[7], parts[13], parts[14]
    b_, t_ = a_b.shape[:2]
    br_a = ((a_b * _short_conv(a_c * a_x, conv_w_l)) * jax.nn.silu(a_z)) @ w_br_conv_l
    br_b = (att_o * jax.nn.silu(z_attn)) @ w_br_attn_l
    br_c = (_rmsnorm(gla_o, gla_g_l).reshape(b_, t_, GLA_DV) * jax.nn.silu(z_gla)) @ w_br_gla_l
    g_a, g_b, g_c = jnp.split(jax.nn.sigmoid(mg + b_gate_l), N_BRANCH, axis=-1)
    return (g_a * br_a + g_b * br_b + g_c * br_c) @ w_out_l


def setup_inputs(seed: int = 0) -> dict:
    key = jax.random.key(seed)
    ks = jax.random.split(key, 24)
    n = jax.random.normal
    f32 = jnp.float32
    return {
        "x": n(ks[0], (BATCH, SEQ, D_MODEL), f32),
        "c": n(ks[1], (BATCH, D_MODEL), f32),
        "ctx": n(ks[2], (BATCH, CTX_LEN, D_MODEL), f32),
        "c_ctx": n(ks[3], (D_MODEL,), f32),
        "w_ada": n(ks[4], (DEPTH, D_MODEL, 3 * D_MODEL), f32) * D_MODEL ** -0.5,
        "b_ada": 0.02 * n(ks[5], (DEPTH, 3 * D_MODEL), f32),
        "g_pre": 1.0 + 0.02 * n(ks[6], (DEPTH, D_MODEL), f32),
        "g_post": 1.0 + 0.02 * n(ks[7], (DEPTH, D_MODEL), f32),
        "w_in": n(ks[8], (DEPTH, D_MODEL, IN_WIDTH), f32) * D_MODEL ** -0.5,
        "conv_w": n(ks[9], (DEPTH, CONV_K, CONV_W), f32) * CONV_K ** -0.5,
        "q_norm_g": 1.0 + 0.02 * n(ks[10], (DEPTH, HEAD_DIM), f32),
        "k_norm_g": 1.0 + 0.02 * n(ks[11], (DEPTH, HEAD_DIM), f32),
        "w_decay_fwd": n(ks[12], (DEPTH, GLA_RANK, GLA_DK), f32) * GLA_RANK ** -0.5,
        "b_decay_fwd": 0.01 * n(ks[13], (DEPTH, GLA_DK), f32),
        "w_decay_bwd": n(ks[14], (DEPTH, GLA_RANK, GLA_DK), f32) * GLA_RANK ** -0.5,
        "b_decay_bwd": 0.01 * n(ks[15], (DEPTH, GLA_DK), f32),
        "gla_norm_g": 1.0 + 0.02 * n(ks[16], (DEPTH, GLA_DVH), f32),
        "w_br_conv": n(ks[17], (DEPTH, CONV_W, D_MODEL), f32) * CONV_W ** -0.5,
        "w_br_attn": n(ks[18], (DEPTH, N_HEADS * HEAD_DIM, D_MODEL), f32) * (N_HEADS * HEAD_DIM) ** -0.5,
        "w_br_gla": n(ks[19], (DEPTH, GLA_DV, D_MODEL), f32) * GLA_DV ** -0.5,
        "b_gate": 0.02 * n(ks[20], (DEPTH, N_BRANCH * D_MODEL), f32),
        "w_out": n(ks[21], (DEPTH, D_MODEL, D_MODEL), f32) * D_MODEL ** -0.5,
    }


def reference(x, c, ctx, c_ctx, w_ada, b_ada, g_pre, g_post, w_in, conv_w, q_norm_g, k_norm_g,
              w_decay_fwd, b_decay_fwd, w_decay_bwd, b_decay_bwd, gla_norm_g,
              w_br_conv, w_br_attn, w_br_gla, b_gate, w_out):
    n_tok = x.shape[1]
    rope = _axial_rope_tables(n_tok)
    xc = ctx
    for l in range(DEPTH):
        last = l == DEPTH - 1
        mod_l = jax.nn.silu(c) @ w_ada[l] + b_ada[l]
        sh_l, sc_l, gt_l = jnp.split(mod_l[:, None, :], 3, axis=-1)
        mod_c = jax.nn.silu(c_ctx) @ w_ada[l] + b_ada[l]
        sh_c, sc_c, gt_c = jnp.split(mod_c, 3, axis=-1)
        h_l = _rmsnorm(x, g_pre[l]) * (1.0 + sc_l) + sh_l
        h_c = _rmsnorm(xc, g_pre[l]) * (1.0 + sc_c) + sh_c
        pl = _split_proj(h_l @ w_in[l])
        pc = _split_proj(h_c @ w_in[l])
        q_l, k_l, v_l = _attn_heads(pl[4], pl[5], pl[6], q_norm_g[l], k_norm_g[l], rope)
        q_c, k_c, v_c = _attn_heads(pc[4], pc[5], pc[6], q_norm_g[l], k_norm_g[l], None)
        keys = jnp.concatenate([k_l, k_c], axis=1)
        vals = jnp.concatenate([v_l, v_c], axis=1)
        att_l = _sdpa_blocks(q_l, keys, vals)
        dec = (w_decay_fwd[l], b_decay_fwd[l], w_decay_bwd[l], b_decay_bwd[l])
        gla_c_in = _gla_inputs(pc[8], pc[9], pc[10], pc[11], pc[12], *dec)
        gla_l_in = _gla_inputs(pl[8], pl[9], pl[10], pl[11], pl[12], *dec)
        gla_l, gla_c = _gla_bidirectional(gla_c_in, gla_l_in)
        shared = (conv_w[l], gla_norm_g[l], w_br_conv[l], w_br_attn[l], w_br_gla[l], b_gate[l], w_out[l])
        out_l = _mixer_output(pl, att_l, gla_l, *shared)
        if not last:
            att_c = _sdpa_blocks(q_c, k_c, v_c)
            out_c = _mixer_output(pc, att_c, gla_c, *shared)
            xc = xc + gt_c * _rmsnorm(out_c, g_post[l])
        x = x + gt_l * _rmsnorm(out_l, g_post[l])
    return x
```

```python
import functools

import jax
import jax.numpy as jnp
import numpy as np
from jax import lax
from jax.experimental import pallas as pl
from jax.experimental.pallas import tpu as pltpu

F32 = jnp.float32
BF16 = jnp.bfloat16

D_MODEL = 1024
DEPTH = 2
GRID_W = 64
CONV_W = 1024
N_HEADS = 8
N_KV_HEADS = 2
HEAD_DIM = 128
GROUP = N_HEADS // N_KV_HEADS
ROPE_THETA = 10000.0
ROPE_AXIS_DIM = HEAD_DIM // 2
ROPE_PAIRS = ROPE_AXIS_DIM // 2
ATTN_SCALE = HEAD_DIM ** -0.5
GLA_HEADS = 4
GLA_DK = D_MODEL // 2
GLA_DV = D_MODEL
GLA_DKH = GLA_DK // GLA_HEADS
GLA_DVH = GLA_DV // GLA_HEADS
GLA_RANK = 16
GLA_TAU = 16.0
N_BRANCH = 3
EPS = 1e-6

Q_W = N_HEADS * HEAD_DIM
KV_W = N_KV_HEADS * HEAD_DIM

LANES = 128
SUBLANES = 8
BF16_ROWS = 16
VMEM_LIMIT = 56 * 1024 * 1024

OFF_A_B = 0
OFF_A_C = OFF_A_B + CONV_W
OFF_A_X = OFF_A_C + CONV_W
OFF_A_Z = OFF_A_X + CONV_W
OFF_Q = OFF_A_Z + CONV_W
OFF_K = OFF_Q + Q_W
OFF_V = OFF_K + KV_W
OFF_Z_ATT = OFF_V + KV_W
OFF_GQ = OFF_Z_ATT + Q_W
OFF_GK = OFF_GQ + GLA_DK
OFF_GV = OFF_GK + GLA_DK
OFF_Z_GLA = OFF_GV + GLA_DV
OFF_MG = OFF_Z_GLA + GLA_DV
OFF_R = OFF_MG + N_BRANCH * D_MODEL
W_PACKED = OFF_R + LANES
ORIG_R = OFF_Z_GLA

PROJ_TM = 256
COL_BLK = 256
ATT_TQ = 256
ATT_KV = 512
GLA_C = 64
GLA_FINE = SUBLANES
GLA_ROWS = 512


def _sigmoid(x):
    return jax.nn.sigmoid(x)


def _silu(x):
    return x * _sigmoid(x)


def _dot(a, b):
    return jnp.dot(a, b, preferred_element_type=F32)


def _dot_nt(a, b):
    return lax.dot_general(a, b, (((1,), (1,)), ((), ())), preferred_element_type=F32)


def _dot_tn(a, b):
    return lax.dot_general(a, b, (((0,), (0,)), ((), ())), preferred_element_type=F32)


def _params(*sem):
    return pltpu.CompilerParams(dimension_semantics=sem, vmem_limit_bytes=VMEM_LIMIT)


def _mod_kernel(c_ref, w_ref, b_ref, o_ref):
    s = _silu(c_ref[...])
    o_ref[...] = _dot(s.astype(BF16), w_ref[...].astype(BF16)) + b_ref[...]


def _modulation(cvec, w_ada, b_ada):
    rows = cvec.shape[0]
    n_col = 3 * D_MODEL // D_MODEL
    return pl.pallas_call(
        _mod_kernel,
        out_shape=jax.ShapeDtypeStruct((DEPTH, rows, 3 * D_MODEL), F32),
        grid=(DEPTH, n_col),
        in_specs=[
            pl.BlockSpec((rows, D_MODEL), lambda l, j: (0, 0)),
            pl.BlockSpec((None, D_MODEL, D_MODEL), lambda l, j: (l, 0, j)),
            pl.BlockSpec((None, 1, D_MODEL), lambda l, j: (l, 0, j)),
        ],
        out_specs=pl.BlockSpec((None, rows, D_MODEL), lambda l, j: (l, 0, j)),
        compiler_params=_params("parallel", "parallel"),
        name="adaln_mod",
    )(cvec, w_ada, b_ada.reshape(DEPTH, 1, 3 * D_MODEL))


def _head_norm(xh, g):
    ms = jnp.mean(xh * xh, axis=-1, keepdims=True)
    return xh * lax.rsqrt(ms + EPS) * g


def _rope(xh, cos, sin):
    lane = lax.broadcasted_iota(jnp.int32, xh.shape, 1)
    first_half = (lane % ROPE_AXIS_DIM) < ROPE_PAIRS
    partner = jnp.where(first_half,
                        pltpu.roll(xh, HEAD_DIM - ROPE_PAIRS, 1),
                        pltpu.roll(xh, ROPE_PAIRS, 1))
    return xh * cos + partner * sin


def _log_sigmoid(x):
    return jnp.minimum(x, 0.0) - jnp.log1p(jnp.exp(-jnp.abs(x)))


def _proj_kernel(x_ref, mod_ref, gpre_ref, w_ref, wdec_ref, bdec_ref, qg_ref, kg_ref,
                 cos_ref, sin_ref, bgate_ref,
                 ua_ref, wa_ref, q_ref, k_ref, v_ref, sza_ref, gq_ref, gk_ref, gv_ref,
                 laf_ref, lab_ref, szg_ref, gate_ref, h_ref, *, use_rope):
    x = x_ref[...]
    ms = jnp.mean(x * x, axis=-1, keepdims=True)
    y = x * lax.rsqrt(ms + EPS) * gpre_ref[...]
    mod = mod_ref[...]
    shift = mod[:, :D_MODEL]
    scale = mod[:, D_MODEL:2 * D_MODEL]
    h_ref[...] = (y * (1.0 + scale) + shift).astype(BF16)

    def proj(off, width=COL_BLK):
        return _dot(h_ref[...], w_ref[:, off:off + width])

    for o in range(0, CONV_W, COL_BLK):
        cols = slice(o, o + COL_BLK)
        ua_ref[:, cols] = (proj(OFF_A_C + o) * proj(OFF_A_X + o)).astype(BF16)
        wa_ref[:, cols] = (proj(OFF_A_B + o) * _silu(proj(OFF_A_Z + o))).astype(BF16)

    def heads(off, width, gain, out_ref, out_scale):
        for o in range(0, width, COL_BLK):
            blk = proj(off + o)
            for hh in range(COL_BLK // HEAD_DIM):
                xh = _head_norm(blk[:, hh * HEAD_DIM:(hh + 1) * HEAD_DIM], gain)
                if use_rope:
                    xh = _rope(xh, cos_ref[...], sin_ref[...])
                c0 = o + hh * HEAD_DIM
                out_ref[:, c0:c0 + HEAD_DIM] = (xh * out_scale).astype(BF16)

    heads(OFF_Q, Q_W, qg_ref[...], q_ref, ATTN_SCALE)
    heads(OFF_K, KV_W, kg_ref[...], k_ref, 1.0)
    v_ref[...] = proj(OFF_V, KV_W).astype(BF16)
    for o in range(0, Q_W, COL_BLK):
        sza_ref[:, o:o + COL_BLK] = _silu(proj(OFF_Z_ATT + o)).astype(BF16)

    for o in range(0, GLA_DK, COL_BLK):
        gq_ref[:, o:o + COL_BLK] = (proj(OFF_GQ + o) * (GLA_DKH ** -0.5)).astype(BF16)
        gk_ref[:, o:o + COL_BLK] = proj(OFF_GK + o).astype(BF16)
    for o in range(0, GLA_DV, COL_BLK):
        gv_ref[:, o:o + COL_BLK] = proj(OFF_GV + o).astype(BF16)
        szg_ref[:, o:o + COL_BLK] = _silu(proj(OFF_Z_GLA + o)).astype(BF16)
    r = proj(OFF_R, LANES).astype(BF16)
    for o in range(0, GLA_DK, COL_BLK):
        laf_ref[:, o:o + COL_BLK] = _log_sigmoid(
            _dot(r, wdec_ref[:, o:o + COL_BLK]) + bdec_ref[:, o:o + COL_BLK]) * (1.0 / GLA_TAU)
        ob = GLA_DK + o
        lab_ref[:, o:o + COL_BLK] = _log_sigmoid(
            _dot(r, wdec_ref[:, ob:ob + COL_BLK]) + bdec_ref[:, ob:ob + COL_BLK]) * (1.0 / GLA_TAU)

    for o in range(0, N_BRANCH * D_MODEL, COL_BLK):
        gate_ref[:, o:o + COL_BLK] = _sigmoid(
            proj(OFF_MG + o) + bgate_ref[:, o:o + COL_BLK]).astype(BF16)


def _projection(x2, mod3, mod_row0, seq_len, gpre, w_pk, wdec, bdec, qg, kg, cos_t, sin_t, bgate,
                use_rope):
    n = x2.shape[0]
    tm = PROJ_TM
    tps = seq_len // tm
    const = lambda i: (0, 0)
    row = lambda i: (i, 0)
    pos = lambda i: (i % tps, 0)
    if mod_row0 is None:
        mod_map = lambda i: (i // tps, 0, 0)
    else:
        mod_map = lambda i: (mod_row0, 0, 0)
    widths = [(CONV_W, BF16), (CONV_W, BF16), (Q_W, BF16), (KV_W, BF16), (KV_W, BF16), (Q_W, BF16),
              (GLA_DK, BF16), (GLA_DK, BF16), (GLA_DV, BF16), (GLA_DK, F32), (GLA_DK, F32),
              (GLA_DV, BF16), (N_BRANCH * D_MODEL, BF16)]
    return pl.pallas_call(
        functools.partial(_proj_kernel, use_rope=use_rope),
        out_shape=[jax.ShapeDtypeStruct((n, w), dt) for w, dt in widths],
        grid=(n // tm,),
        in_specs=[
            pl.BlockSpec((tm, D_MODEL), row),
            pl.BlockSpec((None, 1, 3 * D_MODEL), mod_map),
            pl.BlockSpec((1, D_MODEL), const),
            pl.BlockSpec((D_MODEL, W_PACKED), const, pipeline_mode=pl.Buffered(1)),
            pl.BlockSpec((LANES, 2 * GLA_DK), const),
            pl.BlockSpec((1, 2 * GLA_DK), const),
            pl.BlockSpec((1, HEAD_DIM), const),
            pl.BlockSpec((1, HEAD_DIM), const),
            pl.BlockSpec((tm, HEAD_DIM), pos),
            pl.BlockSpec((tm, HEAD_DIM), pos),
            pl.BlockSpec((1, N_BRANCH * D_MODEL), const),
        ],
        out_specs=[pl.BlockSpec((tm, w), row) for w, _ in widths],
        scratch_shapes=[pltpu.VMEM((tm, D_MODEL), BF16)],
        compiler_params=_params("parallel"),
        name="in_proj_rope" if use_rope else "in_proj",
    )(x2, mod3, gpre, w_pk, wdec, bdec, qg, kg, cos_t, sin_t, bgate)


def _attn_kernel(*refs, n_lat_blocks):
    if n_lat_blocks:
        q_ref, kl_ref, vl_ref, kc_ref, vc_ref, o_ref = refs
    else:
        q_ref, kc_ref, vc_ref, o_ref = refs
    tq = q_ref.shape[0]

    def block(q, k, v, state):
        s = _dot_nt(q, k)
        m_blk = jnp.max(s, axis=-1, keepdims=True)
        if state is None:
            m_new = m_blk
            p = jnp.exp(s - m_new)
            return m_new, jnp.sum(p, axis=-1, keepdims=True), _dot(p.astype(BF16), v)
        m, l, acc = state
        m_new = jnp.maximum(m, m_blk)
        alpha = jnp.exp(m - m_new)
        p = jnp.exp(s - m_new)
        return (m_new, alpha * l + jnp.sum(p, axis=-1, keepdims=True),
                alpha * acc + _dot(p.astype(BF16), v))

    for g in range(GROUP):
        cols = slice(g * HEAD_DIM, (g + 1) * HEAD_DIM)
        q = q_ref[:, cols]
        state = None
        for j in range(n_lat_blocks):
            rows = slice(j * ATT_KV, (j + 1) * ATT_KV)
            state = block(q, kl_ref[rows, :], vl_ref[rows, :], state)
        _, l, acc = block(q, kc_ref[...], vc_ref[...], state)
        o_ref[:, cols] = (acc / l).astype(BF16)
    del tq


def _attention(q2, k_lat, v_lat, k_ctx, v_ctx, batch, q_len, lat_len, ctx_len):
    tq = ATT_TQ
    tiles = q_len // tq
    qmap = lambda b, kv, i: (b * tiles + i, kv)
    kvmap = lambda b, kv, i: (b, kv)
    in_specs = [pl.BlockSpec((tq, GROUP * HEAD_DIM), qmap)]
    args = [q2]
    n_lat_blocks = 0
    if k_lat is not None:
        n_lat_blocks = lat_len // ATT_KV
        in_specs += [pl.BlockSpec((lat_len, HEAD_DIM), kvmap), pl.BlockSpec((lat_len, HEAD_DIM), kvmap)]
        args += [k_lat, v_lat]
    in_specs += [pl.BlockSpec((ctx_len, HEAD_DIM), kvmap), pl.BlockSpec((ctx_len, HEAD_DIM), kvmap)]
    args += [k_ctx, v_ctx]
    return pl.pallas_call(
        functools.partial(_attn_kernel, n_lat_blocks=n_lat_blocks),
        out_shape=jax.ShapeDtypeStruct(q2.shape, BF16),
        grid=(batch, N_KV_HEADS, tiles),
        in_specs=in_specs,
        out_specs=pl.BlockSpec((tq, GROUP * HEAD_DIM), qmap),
        compiler_params=_params("parallel", "parallel", "parallel"),
        name="gqa_lat" if n_lat_blocks else "gqa_ctx",
    )(*args)


def _split3(x):
    hi = x.astype(BF16)
    r1 = x - hi.astype(F32)
    mid = r1.astype(BF16)
    lo = (r1 - mid.astype(F32)).astype(BF16)
    return hi, mid, lo


def _gla_chunk(q, k, v, la, st, rev):
    c = GLA_C
    row = lax.broadcasted_iota(jnp.int32, (c, c), 0)
    col = lax.broadcasted_iota(jnp.int32, (c, c), 1)
    tri = jnp.where((col >= row) if rev else (col <= row), 1.0, 0.0).astype(BF16)
    cum3 = _dot(tri, jnp.concatenate(_split3(la), axis=1))
    b = cum3[:, :GLA_DKH] + cum3[:, GLA_DKH:2 * GLA_DKH] + cum3[:, 2 * GLA_DKH:]
    tot = b[0:1, :] if rev else b[c - 1:c, :]

    qf = q.astype(F32)
    kf = k.astype(F32)
    o = _dot_nt((qf * jnp.exp(b)).astype(BF16), st.astype(BF16))
    kd = (kf * jnp.exp(tot - b)).astype(BF16)
    st_new = st * jnp.exp(tot) + _dot_tn(v, kd)

    rid = lax.broadcasted_iota(jnp.int32, (c, GLA_DKH), 0)
    att = jnp.zeros((c, c), F32)
    m = c // 2
    while m >= GLA_FINE:
        pivot = (m - 1) if rev else m
        b3 = b.reshape(c // (2 * m), 2 * m, GLA_DKH)
        f = jnp.exp(-jnp.abs(b3 - b3[:, pivot:pivot + 1, :])).reshape(c, GLA_DKH)
        upper = (rid % (2 * m)) >= m
        q_half, k_half = (~upper, upper) if rev else (upper, ~upper)
        qm = jnp.where(q_half, qf * f, 0.0).astype(BF16)
        km = jnp.where(k_half, kf * f, 0.0).astype(BF16)
        a_m = _dot_nt(qm, km)
        if 2 * m < c:
            a_m = jnp.where((row // (2 * m)) == (col // (2 * m)), a_m, 0.0)
        att = att + a_m
        m //= 2

    sub = lax.broadcasted_iota(jnp.int32, (c, 1), 0) % GLA_FINE
    att = att + jnp.where(row == col, jnp.sum(qf * kf, axis=-1, keepdims=True), 0.0)
    for d in range(1, GLA_FINE):
        shift = (c - d) if rev else d
        kr = pltpu.roll(kf, shift, 0)
        br = pltpu.roll(b, shift, 0)
        w = jnp.sum(qf * kr * jnp.exp(jnp.minimum(b - br, 0.0)), axis=-1, keepdims=True)
        ok = (sub + d < GLA_FINE) if rev else (sub >= d)
        w = jnp.where(ok, w, 0.0)
        hit = (col == row + d) if rev else (col == row - d)
        att = att + jnp.where(hit, w, 0.0)

    o = o + _dot(att.astype(BF16), v)
    return o, st_new


def _gla_kernel(q_ref, k_ref, v_ref, la_ref, s0_ref, o_ref, sfin_ref, st_ref, *, rev):
    i = pl.program_id(2)

    @pl.when(i == 0)
    def _():
        st_ref[...] = s0_ref[...]

    n_chunks = q_ref.shape[0] // GLA_C

    def body(ci, carry):
        cc = (n_chunks - 1 - ci) if rev else ci
        rows = pl.ds(pl.multiple_of(cc * GLA_C, GLA_C), GLA_C)
        o, st_new = _gla_chunk(q_ref[rows, :], k_ref[rows, :], v_ref[rows, :], la_ref[rows, :],
                               st_ref[...], rev)
        o_ref[rows, :] = o.astype(BF16)
        st_ref[...] = st_new
        return carry

    lax.fori_loop(0, n_chunks, body, 0)
    sfin_ref[...] = st_ref[...]


def _gla_scan(gq, gk, gv, la, s0, batch, seq_len, rev):
    rows = min(GLA_ROWS, seq_len)
    nblk = seq_len // rows
    if rev:
        tok = lambda b, h, i: (b * nblk + (nblk - 1 - i), h)
    else:
        tok = lambda b, h, i: (b * nblk + i, h)
    smap = lambda b, h, i: (b, h, 0, 0)
    return pl.pallas_call(
        functools.partial(_gla_kernel, rev=rev),
        out_shape=[jax.ShapeDtypeStruct(gv.shape, BF16),
                   jax.ShapeDtypeStruct((batch, GLA_HEADS, GLA_DVH, GLA_DKH), F32)],
        grid=(batch, GLA_HEADS, nblk),
        in_specs=[
            pl.BlockSpec((rows, GLA_DKH), tok),
            pl.BlockSpec((rows, GLA_DKH), tok),
            pl.BlockSpec((rows, GLA_DVH), tok),
            pl.BlockSpec((rows, GLA_DKH), tok),
            pl.BlockSpec((None, None, GLA_DVH, GLA_DKH), smap),
        ],
        out_specs=[pl.BlockSpec((rows, GLA_DVH), tok),
                   pl.BlockSpec((None, None, GLA_DVH, GLA_DKH), smap)],
        scratch_shapes=[pltpu.VMEM((GLA_DVH, GLA_DKH), F32)],
        compiler_params=_params("parallel", "parallel", "arbitrary"),
        name="gla_bwd" if rev else "gla_fwd",
    )(gq, gk, gv, la, s0)


def _merge_kernel(ua_ref, uprev_ref, unext_ref, wa_ref, att_ref, sza_ref, of_ref, ob_ref, szg_ref,
                  gate_ref, x_ref, mod_ref, convw_ref, glag_ref, gpost_ref,
                  wconv_ref, watt_ref, wgla_ref, wout_ref, o_ref, *, tiles_per_seq):
    tm = x_ref.shape[0]
    ti = pl.program_id(0) % tiles_per_seq
    u = ua_ref[...].astype(F32)
    prev_row = jnp.where(ti == 0, 0.0, uprev_ref[BF16_ROWS - 1:BF16_ROWS, :].astype(F32))
    next_row = jnp.where(ti == tiles_per_seq - 1, 0.0, unext_ref[0:1, :].astype(F32))
    rid = lax.broadcasted_iota(jnp.int32, (tm, 1), 0)
    u_prev = jnp.where(rid == 0, prev_row, pltpu.roll(u, 1, 0))
    u_next = jnp.where(rid == tm - 1, next_row, pltpu.roll(u, tm - 1, 0))
    cw = convw_ref[...]
    conv = cw[0:1, :] * u_prev + cw[1:2, :] * u + cw[2:3, :] * u_next
    br_a = _dot((wa_ref[...].astype(F32) * conv).astype(BF16), wconv_ref[...])
    br_b = _dot((att_ref[...].astype(F32) * sza_ref[...].astype(F32)).astype(BF16), watt_ref[...])
    parts = []
    for hh in range(GLA_HEADS):
        cols = slice(hh * GLA_DVH, (hh + 1) * GLA_DVH)
        oh = of_ref[:, cols].astype(F32) + ob_ref[:, cols].astype(F32)
        parts.append((_head_norm(oh, glag_ref[...]) * szg_ref[:, cols].astype(F32)).astype(BF16))
    br_c = _dot(jnp.concatenate(parts, axis=1), wgla_ref[...])
    merged = (gate_ref[:, :D_MODEL].astype(F32) * br_a
              + gate_ref[:, D_MODEL:2 * D_MODEL].astype(F32) * br_b
              + gate_ref[:, 2 * D_MODEL:].astype(F32) * br_c)
    out = _dot(merged.astype(BF16), wout_ref[...])
    gate = mod_ref[:, 2 * D_MODEL:]
    o_ref[...] = x_ref[...] + gate * _head_norm(out, gpost_ref[...])


def _merge(ua, wa, att, sza, o_f, o_b, szg, gates, x2, mod3, mod_row0, seq_len,
           convw, glag, gpost, wconv, watt, wgla, wout):
    n = x2.shape[0]
    tm = PROJ_TM
    tps = seq_len // tm
    halo = tm // BF16_ROWS
    n_halo = n // BF16_ROWS
    const = lambda i: (0, 0)
    row = lambda i: (i, 0)
    if mod_row0 is None:
        mod_map = lambda i: (i // tps, 0, 0)
    else:
        mod_map = lambda i: (mod_row0, 0, 0)
    tok = lambda w: pl.BlockSpec((tm, w), row)
    wspec = pl.BlockSpec((D_MODEL, D_MODEL), const)
    return pl.pallas_call(
        functools.partial(_merge_kernel, tiles_per_seq=tps),
        out_shape=jax.ShapeDtypeStruct((n, D_MODEL), F32),
        grid=(n // tm,),
        in_specs=[
            tok(CONV_W),
            pl.BlockSpec((BF16_ROWS, CONV_W), lambda i: (jnp.maximum(i * halo - 1, 0), 0)),
            pl.BlockSpec((BF16_ROWS, CONV_W), lambda i: (jnp.minimum((i + 1) * halo, n_halo - 1), 0)),
            tok(CONV_W), tok(Q_W), tok(Q_W), tok(GLA_DV), tok(GLA_DV), tok(GLA_DV),
            tok(N_BRANCH * D_MODEL), tok(D_MODEL),
            pl.BlockSpec((None, 1, 3 * D_MODEL), mod_map),
            pl.BlockSpec((3, CONV_W), const),
            pl.BlockSpec((1, GLA_DVH), const),
            pl.BlockSpec((1, D_MODEL), const),
            wspec, wspec, wspec, wspec,
        ],
        out_specs=tok(D_MODEL),
        compiler_params=_params("parallel"),
        name="merge_out",
    )(ua, ua, ua, wa, att, sza, o_f, o_b, szg, gates, x2, mod3, convw, glag, gpost,
      wconv, watt, wgla, wout)


def _rope_tables(n_tokens):
    n_rows = n_tokens // GRID_W
    row = jnp.repeat(jnp.arange(n_rows, dtype=F32), GRID_W)
    col = jnp.tile(jnp.arange(GRID_W, dtype=F32), n_rows)
    freqs = ROPE_THETA ** (-jnp.arange(ROPE_PAIRS, dtype=F32) * 2.0 / ROPE_AXIS_DIM)
    ar, ac = row[:, None] * freqs, col[:, None] * freqs
    cos_t = jnp.concatenate([jnp.cos(ar), jnp.cos(ar), jnp.cos(ac), jnp.cos(ac)], axis=1)
    sin_t = jnp.concatenate([-jnp.sin(ar), jnp.sin(ar), -jnp.sin(ac), jnp.sin(ac)], axis=1)
    return cos_t, sin_t


def _pack_w_in(w):
    tail = ORIG_R + 2 * GLA_RANK
    pad = jnp.zeros((D_MODEL, LANES - 2 * GLA_RANK), w.dtype)
    return jnp.concatenate([w[:, :ORIG_R], w[:, tail:], w[:, ORIG_R:tail], pad], axis=1).astype(BF16)


def _pack_decay(w_f, b_f, w_b, b_b):
    wd = jnp.zeros((LANES, 2 * GLA_DK), F32)
    wd = wd.at[:GLA_RANK, :GLA_DK].set(w_f).at[GLA_RANK:2 * GLA_RANK, GLA_DK:].set(w_b)
    return wd.astype(BF16), jnp.concatenate([b_f, b_b])[None, :]


def kernel(x, c, ctx, c_ctx, w_ada, b_ada, g_pre, g_post, w_in, conv_w, q_norm_g, k_norm_g,
           w_decay_fwd, b_decay_fwd, w_decay_bwd, b_decay_bwd, gla_norm_g,
           w_br_conv, w_br_attn, w_br_gla, b_gate, w_out):
    batch, seq, _ = x.shape
    ctx_len = ctx.shape[1]
    assert seq % max(PROJ_TM, ATT_TQ, ATT_KV, GLA_ROWS) == 0 and seq % GRID_W == 0
    assert ctx_len % max(PROJ_TM, ATT_TQ) == 0 and ctx_len % GLA_C == 0

    mod_rows = -(-(batch + 1) // SUBLANES) * SUBLANES
    cvec = jnp.zeros((mod_rows, D_MODEL), F32).at[:batch].set(c).at[batch].set(c_ctx)
    mod = _modulation(cvec, w_ada, b_ada)
    cos_t, sin_t = _rope_tables(seq)
    zero_state = jnp.zeros((batch, GLA_HEADS, GLA_DVH, GLA_DKH), F32)

    xl = x.reshape(batch * seq, D_MODEL)
    xc = ctx.reshape(batch * ctx_len, D_MODEL)
    for l in range(DEPTH):
        last = l == DEPTH - 1
        mod3 = mod[l][:, None, :]
        w_pk = _pack_w_in(w_in[l])
        wdec, bdec = _pack_decay(w_decay_fwd[l], b_decay_fwd[l], w_decay_bwd[l], b_decay_bwd[l])
        shared_in = (g_pre[l][None], w_pk, wdec, bdec, q_norm_g[l][None], k_norm_g[l][None])
        bgate = b_gate[l][None]
        pc = _projection(xc, mod3, batch, ctx_len, *shared_in, cos_t, sin_t, bgate, use_rope=False)
        pl_ = _projection(xl, mod3, None, seq, *shared_in, cos_t, sin_t, bgate, use_rope=True)
        (ua_c, wa_c, q_c, k_c, v_c, sza_c, gq_c, gk_c, gv_c, laf_c, lab_c, szg_c, gate_c) = pc
        (ua_l, wa_l, q_l, k_l, v_l, sza_l, gq_l, gk_l, gv_l, laf_l, lab_l, szg_l, gate_l) = pl_

        att_l = _attention(q_l, k_l, v_l, k_c, v_c, batch, seq, seq, ctx_len)
        of_c, s_f = _gla_scan(gq_c, gk_c, gv_c, laf_c, zero_state, batch, ctx_len, rev=False)
        ob_c, s_b = _gla_scan(gq_c, gk_c, gv_c, lab_c, zero_state, batch, ctx_len, rev=True)
        of_l, _ = _gla_scan(gq_l, gk_l, gv_l, laf_l, s_f, batch, seq, rev=False)
        ob_l, _ = _gla_scan(gq_l, gk_l, gv_l, lab_l, s_b, batch, seq, rev=True)

        shared_out = (conv_w[l], gla_norm_g[l][None], g_post[l][None],
                      w_br_conv[l].astype(BF16), w_br_attn[l].astype(BF16),
                      w_br_gla[l].astype(BF16), w_out[l].astype(BF16))
        if not last:
            att_c = _attention(q_c, None, None, k_c, v_c, batch, ctx_len, 0, ctx_len)
            xc = _merge(ua_c, wa_c, att_c, sza_c, of_c, ob_c, szg_c, gate_c, xc, mod3, batch, ctx_len,
                        *shared_out)
        xl = _merge(ua_l, wa_l, att_l, sza_l, of_l, ob_l, szg_l, gate_l, xl, mod3, None, seq,
                    *shared_out)
    return xl.reshape(batch, seq, D_MODEL)
```

```python
import functools

import jax
import jax.numpy as jnp
import numpy as np
from jax import lax
from jax.experimental import pallas as pl
from jax.experimental.pallas import tpu as pltpu

F32 = jnp.float32
BF16 = jnp.bfloat16

D_MODEL = 1024
DEPTH = 2
GRID_W = 64
CONV_W = 1024
N_HEADS = 8
N_KV_HEADS = 2
HEAD_DIM = 128
GROUP = N_HEADS // N_KV_HEADS
ROPE_THETA = 10000.0
ROPE_AXIS_DIM = HEAD_DIM // 2
ROPE_PAIRS = ROPE_AXIS_DIM // 2
ATTN_SCALE = HEAD_DIM ** -0.5
LOG2_E = 1.4426950408889634
GLA_HEADS = 4
GLA_DK = D_MODEL // 2
GLA_DV = D_MODEL
GLA_DKH = GLA_DK // GLA_HEADS
GLA_DVH = GLA_DV // GLA_HEADS
GLA_RANK = 16
GLA_TAU = 16.0
N_BRANCH = 3
EPS = 1e-6

Q_W = N_HEADS * HEAD_DIM
KV_W = N_KV_HEADS * HEAD_DIM

LANES = 128
SUBLANES = 8
BF16_ROWS = 16
VMEM_LIMIT = 56 * 1024 * 1024

OFF_A_B = 0
OFF_A_C = OFF_A_B + CONV_W
OFF_A_X = OFF_A_C + CONV_W
OFF_A_Z = OFF_A_X + CONV_W
OFF_Q = OFF_A_Z + CONV_W
OFF_K = OFF_Q + Q_W
OFF_V = OFF_K + KV_W
OFF_Z_ATT = OFF_V + KV_W
OFF_GQ = OFF_Z_ATT + Q_W
OFF_GK = OFF_GQ + GLA_DK
OFF_GV = OFF_GK + GLA_DK
OFF_Z_GLA = OFF_GV + GLA_DV
OFF_MG = OFF_Z_GLA + GLA_DV
OFF_R = OFF_MG + N_BRANCH * D_MODEL
W_PACKED = OFF_R + LANES
ORIG_R = OFF_Z_GLA

PROJ_TM = 256
COL_BLK = 256
ATT_TQ = 256
ATT_KV = 512
GLA_C = 64
GLA_FINE = SUBLANES
GLA_ROWS = 512


def _sigmoid(x):
    return jax.nn.sigmoid(x)


def _silu(x):
    return x * _sigmoid(x)


def _dot(a, b):
    return jnp.dot(a, b, preferred_element_type=F32)


def _dot_nt(a, b):
    return lax.dot_general(a, b, (((1,), (1,)), ((), ())), preferred_element_type=F32)


def _dot_tn(a, b):
    return lax.dot_general(a, b, (((0,), (0,)), ((), ())), preferred_element_type=F32)


def _params(*sem):
    return pltpu.CompilerParams(dimension_semantics=sem, vmem_limit_bytes=VMEM_LIMIT)


def _mod_kernel(c_ref, w_ref, b_ref, o_ref):
    s = _silu(c_ref[...])
    o_ref[...] = _dot(s.astype(BF16), w_ref[...].astype(BF16)) + b_ref[...]


def _modulation(cvec, w_ada, b_ada):
    rows = cvec.shape[0]
    n_col = 3 * D_MODEL // D_MODEL
    return pl.pallas_call(
        _mod_kernel,
        out_shape=jax.ShapeDtypeStruct((DEPTH, rows, 3 * D_MODEL), F32),
        grid=(DEPTH, n_col),
        in_specs=[
            pl.BlockSpec((rows, D_MODEL), lambda l, j: (0, 0)),
            pl.BlockSpec((None, D_MODEL, D_MODEL), lambda l, j: (l, 0, j)),
            pl.BlockSpec((None, 1, D_MODEL), lambda l, j: (l, 0, j)),
        ],
        out_specs=pl.BlockSpec((None, rows, D_MODEL), lambda l, j: (l, 0, j)),
        compiler_params=_params("parallel", "parallel"),
        name="adaln_mod",
    )(cvec, w_ada, b_ada.reshape(DEPTH, 1, 3 * D_MODEL))


def _head_norm(xh, g):
    ms = jnp.mean(xh * xh, axis=-1, keepdims=True)
    return xh * lax.rsqrt(ms + EPS) * g


def _rope(xh, cos, sin):
    lane = lax.broadcasted_iota(jnp.int32, xh.shape, 1)
    first_half = (lane % ROPE_AXIS_DIM) < ROPE_PAIRS
    partner = jnp.where(first_half,
                        pltpu.roll(xh, HEAD_DIM - ROPE_PAIRS, 1),
                        pltpu.roll(xh, ROPE_PAIRS, 1))
    return xh * cos + partner * sin


def _log_sigmoid(x):
    return jnp.minimum(x, 0.0) - jnp.log1p(jnp.exp(-jnp.abs(x)))


def _proj_kernel(x_ref, mod_ref, gpre_ref, w_ref, wdec_ref, bdec_ref, qg_ref, kg_ref,
                 cos_ref, sin_ref, bgate_ref,
                 ua_ref, wa_ref, q_ref, k_ref, v_ref, sza_ref, gq_ref, gk_ref, gv_ref,
                 laf_ref, lab_ref, szg_ref, gate_ref, h_ref, *, use_rope):
    x = x_ref[...]
    ms = jnp.mean(x * x, axis=-1, keepdims=True)
    y = x * lax.rsqrt(ms + EPS) * gpre_ref[...]
    mod = mod_ref[...]
    shift = mod[:, :D_MODEL]
    scale = mod[:, D_MODEL:2 * D_MODEL]
    h_ref[...] = (y * (1.0 + scale) + shift).astype(BF16)

    def proj(off, width=COL_BLK):
        return _dot(h_ref[...], w_ref[:, off:off + width])

    for o in range(0, CONV_W, COL_BLK):
        cols = slice(o, o + COL_BLK)
        ua_ref[:, cols] = (proj(OFF_A_C + o) * proj(OFF_A_X + o)).astype(BF16)
        wa_ref[:, cols] = (proj(OFF_A_B + o) * _silu(proj(OFF_A_Z + o))).astype(BF16)

    def heads(off, width, gain, out_ref, out_scale):
        for o in range(0, width, COL_BLK):
            blk = proj(off + o)
            for hh in range(COL_BLK // HEAD_DIM):
                xh = _head_norm(blk[:, hh * HEAD_DIM:(hh + 1) * HEAD_DIM], gain)
                if use_rope:
                    xh = _rope(xh, cos_ref[...], sin_ref[...])
                c0 = o + hh * HEAD_DIM
                out_ref[:, c0:c0 + HEAD_DIM] = (xh * out_scale).astype(BF16)

    heads(OFF_Q, Q_W, qg_ref[...], q_ref, ATTN_SCALE * LOG2_E)
    heads(OFF_K, KV_W, kg_ref[...], k_ref, 1.0)
    v_ref[...] = proj(OFF_V, KV_W).astype(BF16)
    for o in range(0, Q_W, COL_BLK):
        sza_ref[:, o:o + COL_BLK] = _silu(proj(OFF_Z_ATT + o)).astype(BF16)

    for o in range(0, GLA_DK, COL_BLK):
        gq_ref[:, o:o + COL_BLK] = (proj(OFF_GQ + o) * (GLA_DKH ** -0.5)).astype(BF16)
        gk_ref[:, o:o + COL_BLK] = proj(OFF_GK + o).astype(BF16)
    for o in range(0, GLA_DV, COL_BLK):
        gv_ref[:, o:o + COL_BLK] = proj(OFF_GV + o).astype(BF16)
        szg_ref[:, o:o + COL_BLK] = _silu(proj(OFF_Z_GLA + o)).astype(BF16)
    r = proj(OFF_R, LANES).astype(BF16)
    for o in range(0, GLA_DK, COL_BLK):
        laf_ref[:, o:o + COL_BLK] = _log_sigmoid(
            _dot(r, wdec_ref[:, o:o + COL_BLK]) + bdec_ref[:, o:o + COL_BLK]) * (1.0 / GLA_TAU)
        ob = GLA_DK + o
        lab_ref[:, o:o + COL_BLK] = _log_sigmoid(
            _dot(r, wdec_ref[:, ob:ob + COL_BLK]) + bdec_ref[:, ob:ob + COL_BLK]) * (1.0 / GLA_TAU)

    for o in range(0, N_BRANCH * D_MODEL, COL_BLK):
        gate_ref[:, o:o + COL_BLK] = _sigmoid(
            proj(OFF_MG + o) + bgate_ref[:, o:o + COL_BLK]).astype(BF16)


def _projection(x2, mod3, mod_row0, seq_len, gpre, w_pk, wdec, bdec, qg, kg, cos_t, sin_t, bgate,
                use_rope):
    n = x2.shape[0]
    tm = PROJ_TM
    tps = seq_len // tm
    const = lambda i: (0, 0)
    row = lambda i: (i, 0)
    pos = lambda i: (i % tps, 0)
    if mod_row0 is None:
        mod_map = lambda i: (i // tps, 0, 0)
    else:
        mod_map = lambda i: (mod_row0, 0, 0)
    widths = [(CONV_W, BF16), (CONV_W, BF16), (Q_W, BF16), (KV_W, BF16), (KV_W, BF16), (Q_W, BF16),
              (GLA_DK, BF16), (GLA_DK, BF16), (GLA_DV, BF16), (GLA_DK, F32), (GLA_DK, F32),
              (GLA_DV, BF16), (N_BRANCH * D_MODEL, BF16)]
    return pl.pallas_call(
        functools.partial(_proj_kernel, use_rope=use_rope),
        out_shape=[jax.ShapeDtypeStruct((n, w), dt) for w, dt in widths],
        grid=(n // tm,),
        in_specs=[
            pl.BlockSpec((tm, D_MODEL), row),
            pl.BlockSpec((None, 1, 3 * D_MODEL), mod_map),
            pl.BlockSpec((1, D_MODEL), const),
            pl.BlockSpec((D_MODEL, W_PACKED), const, pipeline_mode=pl.Buffered(1)),
            pl.BlockSpec((LANES, 2 * GLA_DK), const),
            pl.BlockSpec((1, 2 * GLA_DK), const),
            pl.BlockSpec((1, HEAD_DIM), const),
            pl.BlockSpec((1, HEAD_DIM), const),
            pl.BlockSpec((tm, HEAD_DIM), pos),
            pl.BlockSpec((tm, HEAD_DIM), pos),
            pl.BlockSpec((1, N_BRANCH * D_MODEL), const),
        ],
        out_specs=[pl.BlockSpec((tm, w), row) for w, _ in widths],
        scratch_shapes=[pltpu.VMEM((tm, D_MODEL), BF16)],
        compiler_params=_params("parallel"),
        name="in_proj_rope" if use_rope else "in_proj",
    )(x2, mod3, gpre, w_pk, wdec, bdec, qg, kg, cos_t, sin_t, bgate)


def _attn_kernel(*refs, n_lat_blocks):
    if n_lat_blocks:
        q_ref, kl_ref, vl_ref, kc_ref, vc_ref, o_ref, qs_ref, m_ref, l_ref, acc_ref = refs
    else:
        q_ref, kc_ref, vc_ref, o_ref, qs_ref, m_ref, l_ref, acc_ref = refs
    tq = q_ref.shape[0]
    for g in range(GROUP):
        qs_ref[g * tq:(g + 1) * tq, :] = q_ref[:, g * HEAD_DIM:(g + 1) * HEAD_DIM]
    m_ref[...] = jnp.full(m_ref.shape, -jnp.inf, F32)
    l_ref[...] = jnp.zeros(l_ref.shape, F32)
    acc_ref[...] = jnp.zeros(acc_ref.shape, F32)

    def block(k, v):
        s = _dot_nt(qs_ref[...], k)
        m_old = m_ref[...]
        m_new = jnp.maximum(m_old, jnp.max(s, axis=-1, keepdims=True))
        alpha = jnp.exp2(m_old - m_new)
        p = jnp.exp2(s - jnp.tile(m_new, (1, s.shape[1] // LANES)))
        l_ref[...] = alpha * l_ref[...] + jnp.sum(p, axis=-1, keepdims=True)
        acc_ref[...] = alpha * acc_ref[...] + _dot(p.astype(BF16), v)
        m_ref[...] = m_new

    def lat_block(j, carry):
        rows = pl.ds(pl.multiple_of(j * ATT_KV, ATT_KV), ATT_KV)
        block(kl_ref[rows, :], vl_ref[rows, :])
        return carry

    if n_lat_blocks:
        lax.fori_loop(0, n_lat_blocks, lat_block, 0)
    block(kc_ref[...], vc_ref[...])
    out = acc_ref[...] / l_ref[...]
    for g in range(GROUP):
        o_ref[:, g * HEAD_DIM:(g + 1) * HEAD_DIM] = out[g * tq:(g + 1) * tq, :].astype(BF16)


def _attention(q2, k_lat, v_lat, k_ctx, v_ctx, batch, q_len, lat_len, ctx_len):
    tq = ATT_TQ
    tiles = q_len // tq
    qmap = lambda b, kv, i: (b * tiles + i, kv)
    kvmap = lambda b, kv, i: (b, kv)
    in_specs = [pl.BlockSpec((tq, GROUP * HEAD_DIM), qmap)]
    args = [q2]
    n_lat_blocks = 0
    if k_lat is not None:
        n_lat_blocks = lat_len // ATT_KV
        in_specs += [pl.BlockSpec((lat_len, HEAD_DIM), kvmap), pl.BlockSpec((lat_len, HEAD_DIM), kvmap)]
        args += [k_lat, v_lat]
    in_specs += [pl.BlockSpec((ctx_len, HEAD_DIM), kvmap), pl.BlockSpec((ctx_len, HEAD_DIM), kvmap)]
    args += [k_ctx, v_ctx]
    return pl.pallas_call(
        functools.partial(_attn_kernel, n_lat_blocks=n_lat_blocks),
        out_shape=jax.ShapeDtypeStruct(q2.shape, BF16),
        grid=(batch, N_KV_HEADS, tiles),
        in_specs=in_specs,
        out_specs=pl.BlockSpec((tq, GROUP * HEAD_DIM), qmap),
        scratch_shapes=[pltpu.VMEM((GROUP * tq, HEAD_DIM), BF16),
                        pltpu.VMEM((GROUP * tq, LANES), F32),
                        pltpu.VMEM((GROUP * tq, LANES), F32),
                        pltpu.VMEM((GROUP * tq, HEAD_DIM), F32)],
        compiler_params=_params("parallel", "parallel", "parallel"),
        name="gqa_lat" if n_lat_blocks else "gqa_ctx",
    )(*args)


def _split3(x):
    hi = x.astype(BF16)
    r1 = x - hi.astype(F32)
    mid = r1.astype(BF16)
    lo = (r1 - mid.astype(F32)).astype(BF16)
    return hi, mid, lo


def _gla_consts(rev):
    c = GLA_C
    row = lax.broadcasted_iota(jnp.int32, (c, c), 0)
    col = lax.broadcasted_iota(jnp.int32, (c, c), 1)
    tri = jnp.where((col >= row) if rev else (col <= row), 1.0, 0.0).astype(BF16)
    level_masks = []
    m = c // 2
    while m >= GLA_FINE:
        same = (row // (2 * m)) == (col // (2 * m))
        q_upper = (row % (2 * m)) >= m
        k_upper = (col % (2 * m)) >= m
        pair = (~q_upper & k_upper) if rev else (q_upper & ~k_upper)
        level_masks.append((m, same & pair))
        m //= 2
    sub = lax.broadcasted_iota(jnp.int32, (c, LANES), 0) % GLA_FINE
    lane = lax.broadcasted_iota(jnp.int32, (c, LANES), 1)
    d_of_lane = lane if rev else (GLA_FINE - 1 - lane)
    ok = (sub + d_of_lane < GLA_FINE) if rev else (sub >= d_of_lane)
    lane_code = jnp.where((lane < GLA_FINE) & ok, d_of_lane, -1)
    return tri, level_masks, lane_code


def _gla_chunk(q, k, v, la, st, rev, consts):
    c = GLA_C
    tri, level_masks, lane_code = consts
    cum3 = _dot(tri, jnp.concatenate(_split3(la), axis=1))
    b = cum3[:, :GLA_DKH] + cum3[:, GLA_DKH:2 * GLA_DKH] + cum3[:, 2 * GLA_DKH:]
    tot = b[0:1, :] if rev else b[c - 1:c, :]

    qf = q.astype(F32)
    kf = k.astype(F32)
    o = _dot_nt((qf * jnp.exp(b)).astype(BF16), st.astype(BF16))
    kd = (kf * jnp.exp(tot - b)).astype(BF16)
    st_new = st * jnp.exp(tot) + _dot_tn(v, kd)

    w_slots = jnp.where(lane_code == 0, jnp.sum(qf * kf, axis=-1, keepdims=True), 0.0)
    for d in range(1, GLA_FINE):
        shift = (c - d) if rev else d
        kr = pltpu.roll(kf, shift, 0)
        br = pltpu.roll(b, shift, 0)
        w = jnp.sum(qf * kr * jnp.exp(jnp.minimum(b - br, 0.0)), axis=-1, keepdims=True)
        w_slots = jnp.where(lane_code == d, w, w_slots)
    att = pltpu.roll(w_slots, 0 if rev else LANES - (GLA_FINE - 1), 1, stride=1, stride_axis=0)[:, :c]

    for m, mask in level_masks:
        pivot = (m - 1) if rev else m
        b3 = b.reshape(c // (2 * m), 2 * m, GLA_DKH)
        f = jnp.exp(-jnp.abs(b3 - b3[:, pivot:pivot + 1, :])).reshape(c, GLA_DKH)
        att = jnp.where(mask, _dot_nt((qf * f).astype(BF16), (kf * f).astype(BF16)), att)

    o = o + _dot(att.astype(BF16), v)
    return o, st_new


def _gla_kernel(qf_ref, kf_ref, vf_ref, laf_ref, qb_ref, kb_ref, vb_ref, lab_ref, s0f_ref, s0b_ref,
                of_ref, ob_ref, sff_ref, sfb_ref, st_ref):
    @pl.when(pl.program_id(1) == 0)
    def _():
        st_ref[0] = s0f_ref[...]
        st_ref[1] = s0b_ref[...]

    n_chunks = qf_ref.shape[0] // GLA_C
    dirs = ((0, False, qf_ref, kf_ref, vf_ref, laf_ref, of_ref, _gla_consts(False)),
            (1, True, qb_ref, kb_ref, vb_ref, lab_ref, ob_ref, _gla_consts(True)))

    def body(ci, carry):
        for di, rev, q_ref, k_ref, v_ref, la_ref, o_ref, consts in dirs:
            cc = (n_chunks - 1 - ci) if rev else ci
            rows = pl.ds(pl.multiple_of(cc * GLA_C, GLA_C), GLA_C)
            for hh in range(GLA_HEADS):
                kc = slice(hh * GLA_DKH, (hh + 1) * GLA_DKH)
                vc = slice(hh * GLA_DVH, (hh + 1) * GLA_DVH)
                o, st_new = _gla_chunk(q_ref[rows, kc], k_ref[rows, kc], v_ref[rows, vc],
                                       la_ref[rows, kc], st_ref[di, hh], rev, consts)
                o_ref[rows, vc] = o.astype(BF16)
                st_ref[di, hh] = st_new
        return carry

    lax.fori_loop(0, n_chunks, body, 0)
    sff_ref[...] = st_ref[0]
    sfb_ref[...] = st_ref[1]


def _gla_scan(gq, gk, gv, la_f, la_b, s0_f, s0_b, batch, seq_len):
    rows = min(GLA_ROWS, seq_len)
    nblk = seq_len // rows
    fwd = lambda b, i: (b * nblk + i, 0)
    bwd = lambda b, i: (b * nblk + (nblk - 1 - i), 0)
    smap = lambda b, i: (b, 0, 0, 0)
    state_spec = pl.BlockSpec((None, GLA_HEADS, GLA_DVH, GLA_DKH), smap)
    state_shape = jax.ShapeDtypeStruct((batch, GLA_HEADS, GLA_DVH, GLA_DKH), F32)

    def views(index_map):
        return [pl.BlockSpec((rows, GLA_DK), index_map), pl.BlockSpec((rows, GLA_DK), index_map),
                pl.BlockSpec((rows, GLA_DV), index_map), pl.BlockSpec((rows, GLA_DK), index_map)]

    return pl.pallas_call(
        _gla_kernel,
        out_shape=[jax.ShapeDtypeStruct(gv.shape, BF16), jax.ShapeDtypeStruct(gv.shape, BF16),
                   state_shape, state_shape],
        grid=(batch, nblk),
        in_specs=views(fwd) + views(bwd) + [state_spec, state_spec],
        out_specs=[pl.BlockSpec((rows, GLA_DV), fwd), pl.BlockSpec((rows, GLA_DV), bwd),
                   state_spec, state_spec],
        scratch_shapes=[pltpu.VMEM((2, GLA_HEADS, GLA_DVH, GLA_DKH), F32)],
        compiler_params=_params("parallel", "arbitrary"),
        name="gla_bidir",
    )(gq, gk, gv, la_f, gq, gk, gv, la_b, s0_f, s0_b)


def _merge_kernel(ua_ref, uprev_ref, unext_ref, wa_ref, att_ref, sza_ref, of_ref, ob_ref, szg_ref,
                  gate_ref, x_ref, mod_ref, convw_ref, glag_ref, gpost_ref,
                  wconv_ref, watt_ref, wgla_ref, wout_ref, o_ref, *, tiles_per_seq):
    tm = x_ref.shape[0]
    ti = pl.program_id(0) % tiles_per_seq
    u = ua_ref[...].astype(F32)
    prev_row = jnp.where(ti == 0, 0.0, uprev_ref[BF16_ROWS - 1:BF16_ROWS, :].astype(F32))
    next_row = jnp.where(ti == tiles_per_seq - 1, 0.0, unext_ref[0:1, :].astype(F32))
    rid = lax.broadcasted_iota(jnp.int32, (tm, 1), 0)
    u_prev = jnp.where(rid == 0, prev_row, pltpu.roll(u, 1, 0))
    u_next = jnp.where(rid == tm - 1, next_row, pltpu.roll(u, tm - 1, 0))
    cw = convw_ref[...]
    conv = cw[0:1, :] * u_prev + cw[1:2, :] * u + cw[2:3, :] * u_next
    br_a = _dot((wa_ref[...].astype(F32) * conv).astype(BF16), wconv_ref[...])
    br_b = _dot((att_ref[...].astype(F32) * sza_ref[...].astype(F32)).astype(BF16), watt_ref[...])
    parts = []
    for hh in range(GLA_HEADS):
        cols = slice(hh * GLA_DVH, (hh + 1) * GLA_DVH)
        oh = of_ref[:, cols].astype(F32) + ob_ref[:, cols].astype(F32)
        parts.append((_head_norm(oh, glag_ref[...]) * szg_ref[:, cols].astype(F32)).astype(BF16))
    br_c = _dot(jnp.concatenate(parts, axis=1), wgla_ref[...])
    merged = (gate_ref[:, :D_MODEL].astype(F32) * br_a
              + gate_ref[:, D_MODEL:2 * D_MODEL].astype(F32) * br_b
              + gate_ref[:, 2 * D_MODEL:].astype(F32) * br_c)
    out = _dot(merged.astype(BF16), wout_ref[...])
    gate = mod_ref[:, 2 * D_MODEL:]
    o_ref[...] = x_ref[...] + gate * _head_norm(out, gpost_ref[...])


def _merge(ua, wa, att, sza, o_f, o_b, szg, gates, x2, mod3, mod_row0, seq_len,
           convw, glag, gpost, wconv, watt, wgla, wout):
    n = x2.shape[0]
    tm = PROJ_TM
    tps = seq_len // tm
    halo = tm // BF16_ROWS
    n_halo = n // BF16_ROWS
    const = lambda i: (0, 0)
    row = lambda i: (i, 0)
    if mod_row0 is None:
        mod_map = lambda i: (i // tps, 0, 0)
    else:
        mod_map = lambda i: (mod_row0, 0, 0)
    tok = lambda w: pl.BlockSpec((tm, w), row)
    wspec = pl.BlockSpec((D_MODEL, D_MODEL), const)
    return pl.pallas_call(
        functools.partial(_merge_kernel, tiles_per_seq=tps),
        out_shape=jax.ShapeDtypeStruct((n, D_MODEL), F32),
        grid=(n // tm,),
        in_specs=[
            tok(CONV_W),
            pl.BlockSpec((BF16_ROWS, CONV_W), lambda i: (jnp.maximum(i * halo - 1, 0), 0)),
            pl.BlockSpec((BF16_ROWS, CONV_W), lambda i: (jnp.minimum((i + 1) * halo, n_halo - 1), 0)),
            tok(CONV_W), tok(Q_W), tok(Q_W), tok(GLA_DV), tok(GLA_DV), tok(GLA_DV),
            tok(N_BRANCH * D_MODEL), tok(D_MODEL),
            pl.BlockSpec((None, 1, 3 * D_MODEL), mod_map),
            pl.BlockSpec((3, CONV_W), const),
            pl.BlockSpec((1, GLA_DVH), const),
            pl.BlockSpec((1, D_MODEL), const),
            wspec, wspec, wspec, wspec,
        ],
        out_specs=tok(D_MODEL),
        compiler_params=_params("parallel"),
        name="merge_out",
    )(ua, ua, ua, wa, att, sza, o_f, o_b, szg, gates, x2, mod3, convw, glag, gpost,
      wconv, watt, wgla, wout)


def _rope_tables(n_tokens):
    n_rows = n_tokens // GRID_W
    row = jnp.repeat(jnp.arange(n_rows, dtype=F32), GRID_W)
    col = jnp.tile(jnp.arange(GRID_W, dtype=F32), n_rows)
    freqs = ROPE_THETA ** (-jnp.arange(ROPE_PAIRS, dtype=F32) * 2.0 / ROPE_AXIS_DIM)
    ar, ac = row[:, None] * freqs, col[:, None] * freqs
    cos_t = jnp.concatenate([jnp.cos(ar), jnp.cos(ar), jnp.cos(ac), jnp.cos(ac)], axis=1)
    sin_t = jnp.concatenate([-jnp.sin(ar), jnp.sin(ar), -jnp.sin(ac), jnp.sin(ac)], axis=1)
    return cos_t, sin_t


def _pack_w_in(w):
    tail = ORIG_R + 2 * GLA_RANK
    pad = jnp.zeros((D_MODEL, LANES - 2 * GLA_RANK), w.dtype)
    return jnp.concatenate([w[:, :ORIG_R], w[:, tail:], w[:, ORIG_R:tail], pad], axis=1).astype(BF16)


def _pack_decay(w_f, b_f, w_b, b_b):
    wd = jnp.zeros((LANES, 2 * GLA_DK), F32)
    wd = wd.at[:GLA_RANK, :GLA_DK].set(w_f).at[GLA_RANK:2 * GLA_RANK, GLA_DK:].set(w_b)
    return wd.astype(BF16), jnp.concatenate([b_f, b_b])[None, :]


def kernel(x, c, ctx, c_ctx, w_ada, b_ada, g_pre, g_post, w_in, conv_w, q_norm_g, k_norm_g,
           w_decay_fwd, b_decay_fwd, w_decay_bwd, b_decay_bwd, gla_norm_g,
           w_br_conv, w_br_attn, w_br_gla, b_gate, w_out):
    batch, seq, _ = x.shape
    ctx_len = ctx.shape[1]
    assert seq % max(PROJ_TM, ATT_TQ, ATT_KV, GLA_ROWS) == 0 and seq % GRID_W == 0
    assert ctx_len % max(PROJ_TM, ATT_TQ) == 0 and ctx_len % GLA_C == 0

    mod_rows = -(-(batch + 1) // SUBLANES) * SUBLANES
    cvec = jnp.zeros((mod_rows, D_MODEL), F32).at[:batch].set(c).at[batch].set(c_ctx)
    mod = _modulation(cvec, w_ada, b_ada)
    cos_t, sin_t = _rope_tables(seq)
    zero_state = jnp.zeros((batch, GLA_HEADS, GLA_DVH, GLA_DKH), F32)

    xl = x.reshape(batch * seq, D_MODEL)
    xc = ctx.reshape(batch * ctx_len, D_MODEL)
    for l in range(DEPTH):
        last = l == DEPTH - 1
        mod3 = mod[l][:, None, :]
        w_pk = _pack_w_in(w_in[l])
        wdec, bdec = _pack_decay(w_decay_fwd[l], b_decay_fwd[l], w_decay_bwd[l], b_decay_bwd[l])
        shared_in = (g_pre[l][None], w_pk, wdec, bdec, q_norm_g[l][None], k_norm_g[l][None])
        bgate = b_gate[l][None]
        pc = _projection(xc, mod3, batch, ctx_len, *shared_in, cos_t, sin_t, bgate, use_rope=False)
        pl_ = _projection(xl, mod3, None, seq, *shared_in, cos_t, sin_t, bgate, use_rope=True)
        (ua_c, wa_c, q_c, k_c, v_c, sza_c, gq_c, gk_c, gv_c, laf_c, lab_c, szg_c, gate_c) = pc
        (ua_l, wa_l, q_l, k_l, v_l, sza_l, gq_l, gk_l, gv_l, laf_l, lab_l, szg_l, gate_l) = pl_

        att_l = _attention(q_l, k_l, v_l, k_c, v_c, batch, seq, seq, ctx_len)
        of_c, ob_c, s_f, s_b = _gla_scan(gq_c, gk_c, gv_c, laf_c, lab_c, zero_state, zero_state,
                                         batch, ctx_len)
        of_l, ob_l, _, _ = _gla_scan(gq_l, gk_l, gv_l, laf_l, lab_l, s_f, s_b, batch, seq)

        shared_out = (conv_w[l], gla_norm_g[l][None], g_post[l][None],
                      w_br_conv[l].astype(BF16), w_br_attn[l].astype(BF16),
                      w_br_gla[l].astype(BF16), w_out[l].astype(BF16))
        if not last:
            att_c = _attention(q_c, None, None, k_c, v_c, batch, ctx_len, 0, ctx_len)
            xc = _merge(ua_c, wa_c, att_c, sza_c, of_c, ob_c, szg_c, gate_c, xc, mod3, batch, ctx_len,
                        *shared_out)
        xl = _merge(ua_l, wa_l, att_l, sza_l, of_l, ob_l, szg_l, gate_l, xl, mod3, None, seq,
                    *shared_out)
    return xl.reshape(batch, seq, D_MODEL)
```

```python
import functools

import jax
import jax.numpy as jnp
import numpy as np
from jax import lax
from jax.experimental import pallas as pl
from jax.experimental.pallas import tpu as pltpu

F32 = jnp.float32
BF16 = jnp.bfloat16

D_MODEL = 1024
DEPTH = 2
GRID_W = 64
CONV_W = 1024
N_HEADS = 8
N_KV_HEADS = 2
HEAD_DIM = 128
GROUP = N_HEADS // N_KV_HEADS
ROPE_THETA = 10000.0
ROPE_AXIS_DIM = HEAD_DIM // 2
ROPE_PAIRS = ROPE_AXIS_DIM // 2
ATTN_SCALE = HEAD_DIM ** -0.5
LOG2_E = 1.4426950408889634
GLA_HEADS = 4
GLA_DK = D_MODEL // 2
GLA_DV = D_MODEL
GLA_DKH = GLA_DK // GLA_HEADS
GLA_DVH = GLA_DV // GLA_HEADS
GLA_RANK = 16
GLA_TAU = 16.0
N_BRANCH = 3
EPS = 1e-6

Q_W = N_HEADS * HEAD_DIM
KV_W = N_KV_HEADS * HEAD_DIM

LANES = 128
SUBLANES = 8
BF16_ROWS = 16
VMEM_LIMIT = 56 * 1024 * 1024

OFF_A_B = 0
OFF_A_C = OFF_A_B + CONV_W
OFF_A_X = OFF_A_C + CONV_W
OFF_A_Z = OFF_A_X + CONV_W
OFF_Q = OFF_A_Z + CONV_W
OFF_K = OFF_Q + Q_W
OFF_V = OFF_K + KV_W
OFF_Z_ATT = OFF_V + KV_W
OFF_GQ = OFF_Z_ATT + Q_W
OFF_GK = OFF_GQ + GLA_DK
OFF_GV = OFF_GK + GLA_DK
OFF_Z_GLA = OFF_GV + GLA_DV
OFF_MG = OFF_Z_GLA + GLA_DV
OFF_R = OFF_MG + N_BRANCH * D_MODEL
W_PACKED = OFF_R + LANES
ORIG_R = OFF_Z_GLA

PROJ_TM = 256
COL_BLK = 256
ATT_TQ = 256
ATT_KV = 512
ATT_ROW_TILE = 256
GLA_C = 64
GLA_FINE = 4
GLA_ROWS = 512
GLA_LEVELS = tuple(GLA_C >> (i + 1) for i in range(int(np.log2(GLA_C // GLA_FINE))))
assert len(GLA_LEVELS) * GLA_C % LANES == 0 and 2 * GLA_C == LANES and GLA_FINE <= SUBLANES


def _sigmoid(x):
    return jax.nn.sigmoid(x)


def _silu(x):
    return x * _sigmoid(x)


def _dot(a, b):
    return jnp.dot(a, b, preferred_element_type=F32)


def _dot_nt(a, b):
    return lax.dot_general(a, b, (((1,), (1,)), ((), ())), preferred_element_type=F32)


def _dot_tn(a, b):
    return lax.dot_general(a, b, (((0,), (0,)), ((), ())), preferred_element_type=F32)


def _params(*sem):
    return pltpu.CompilerParams(dimension_semantics=sem, vmem_limit_bytes=VMEM_LIMIT)


def _mod_kernel(c_ref, w_ref, b_ref, o_ref):
    s = _silu(c_ref[...])
    o_ref[...] = _dot(s.astype(BF16), w_ref[...].astype(BF16)) + b_ref[...]


def _modulation(cvec, w_ada, b_ada):
    rows = cvec.shape[0]
    n_col = 3 * D_MODEL // D_MODEL
    return pl.pallas_call(
        _mod_kernel,
        out_shape=jax.ShapeDtypeStruct((DEPTH, rows, 3 * D_MODEL), F32),
        grid=(DEPTH, n_col),
        in_specs=[
            pl.BlockSpec((rows, D_MODEL), lambda l, j: (0, 0)),
            pl.BlockSpec((None, D_MODEL, D_MODEL), lambda l, j: (l, 0, j)),
            pl.BlockSpec((None, 1, D_MODEL), lambda l, j: (l, 0, j)),
        ],
        out_specs=pl.BlockSpec((None, rows, D_MODEL), lambda l, j: (l, 0, j)),
        compiler_params=_params("parallel", "parallel"),
        name="adaln_mod",
    )(cvec, w_ada, b_ada.reshape(DEPTH, 1, 3 * D_MODEL))


def _head_norm(xh, g):
    ms = jnp.mean(xh * xh, axis=-1, keepdims=True)
    return xh * lax.rsqrt(ms + EPS) * g


def _rope(xh, cos, sin):
    lane = lax.broadcasted_iota(jnp.int32, xh.shape, 1)
    first_half = (lane % ROPE_AXIS_DIM) < ROPE_PAIRS
    partner = jnp.where(first_half,
                        pltpu.roll(xh, HEAD_DIM - ROPE_PAIRS, 1),
                        pltpu.roll(xh, ROPE_PAIRS, 1))
    return xh * cos + partner * sin


def _log_sigmoid(x):
    return jnp.minimum(x, 0.0) - jnp.log1p(jnp.exp(-jnp.abs(x)))


def _proj_kernel(x_ref, mod_ref, gpre_ref, w_ref, wdec_ref, bdec_ref, qg_ref, kg_ref,
                 cos_ref, sin_ref, bgate_ref,
                 ua_ref, wa_ref, q_ref, k_ref, v_ref, sza_ref, gq_ref, gk_ref, gv_ref,
                 laf_ref, lab_ref, szg_ref, gate_ref, h_ref, *, use_rope):
    x = x_ref[...]
    ms = jnp.mean(x * x, axis=-1, keepdims=True)
    y = x * lax.rsqrt(ms + EPS) * gpre_ref[...]
    mod = mod_ref[...]
    shift = mod[:, :D_MODEL]
    scale = mod[:, D_MODEL:2 * D_MODEL]
    h_ref[...] = (y * (1.0 + scale) + shift).astype(BF16)

    def proj(off, width=COL_BLK):
        return _dot(h_ref[...], w_ref[:, off:off + width])

    for o in range(0, CONV_W, COL_BLK):
        cols = slice(o, o + COL_BLK)
        ua_ref[:, cols] = (proj(OFF_A_C + o) * proj(OFF_A_X + o)).astype(BF16)
        wa_ref[:, cols] = (proj(OFF_A_B + o) * _silu(proj(OFF_A_Z + o))).astype(BF16)

    def heads(off, width, gain, out_ref, out_scale):
        for o in range(0, width, COL_BLK):
            blk = proj(off + o)
            for hh in range(COL_BLK // HEAD_DIM):
                xh = _head_norm(blk[:, hh * HEAD_DIM:(hh + 1) * HEAD_DIM], gain)
                if use_rope:
                    xh = _rope(xh, cos_ref[...], sin_ref[...])
                c0 = o + hh * HEAD_DIM
                out_ref[:, c0:c0 + HEAD_DIM] = (xh * out_scale).astype(BF16)

    heads(OFF_Q, Q_W, qg_ref[...], q_ref, ATTN_SCALE * LOG2_E)
    heads(OFF_K, KV_W, kg_ref[...], k_ref, 1.0)
    v_ref[...] = proj(OFF_V, KV_W).astype(BF16)
    for o in range(0, Q_W, COL_BLK):
        sza_ref[:, o:o + COL_BLK] = _silu(proj(OFF_Z_ATT + o)).astype(BF16)

    for o in range(0, GLA_DK, COL_BLK):
        gq_ref[:, o:o + COL_BLK] = (proj(OFF_GQ + o) * (GLA_DKH ** -0.5)).astype(BF16)
        gk_ref[:, o:o + COL_BLK] = proj(OFF_GK + o).astype(BF16)
    for o in range(0, GLA_DV, COL_BLK):
        gv_ref[:, o:o + COL_BLK] = proj(OFF_GV + o).astype(BF16)
        szg_ref[:, o:o + COL_BLK] = _silu(proj(OFF_Z_GLA + o)).astype(BF16)
    r = proj(OFF_R, LANES).astype(BF16)
    for o in range(0, GLA_DK, COL_BLK):
        laf_ref[:, o:o + COL_BLK] = _log_sigmoid(
            _dot(r, wdec_ref[:, o:o + COL_BLK]) + bdec_ref[:, o:o + COL_BLK]) * (LOG2_E / GLA_TAU)
        ob = GLA_DK + o
        lab_ref[:, o:o + COL_BLK] = _log_sigmoid(
            _dot(r, wdec_ref[:, ob:ob + COL_BLK]) + bdec_ref[:, ob:ob + COL_BLK]) * (LOG2_E / GLA_TAU)

    for o in range(0, N_BRANCH * D_MODEL, COL_BLK):
        gate_ref[:, o:o + COL_BLK] = _sigmoid(
            proj(OFF_MG + o) + bgate_ref[:, o:o + COL_BLK]).astype(BF16)


def _projection(x2, mod3, mod_row0, seq_len, gpre, w_pk, wdec, bdec, qg, kg, cos_t, sin_t, bgate,
                use_rope):
    n = x2.shape[0]
    tm = PROJ_TM
    tps = seq_len // tm
    const = lambda i: (0, 0)
    row = lambda i: (i, 0)
    pos = lambda i: (i % tps, 0)
    if mod_row0 is None:
        mod_map = lambda i: (i // tps, 0, 0)
    else:
        mod_map = lambda i: (mod_row0, 0, 0)
    widths = [(CONV_W, BF16), (CONV_W, BF16), (Q_W, BF16), (KV_W, BF16), (KV_W, BF16), (Q_W, BF16),
              (GLA_DK, BF16), (GLA_DK, BF16), (GLA_DV, BF16), (GLA_DK, F32), (GLA_DK, F32),
              (GLA_DV, BF16), (N_BRANCH * D_MODEL, BF16)]
    return pl.pallas_call(
        functools.partial(_proj_kernel, use_rope=use_rope),
        out_shape=[jax.ShapeDtypeStruct((n, w), dt) for w, dt in widths],
        grid=(n // tm,),
        in_specs=[
            pl.BlockSpec((tm, D_MODEL), row),
            pl.BlockSpec((None, 1, 3 * D_MODEL), mod_map),
            pl.BlockSpec((1, D_MODEL), const),
            pl.BlockSpec((D_MODEL, W_PACKED), const, pipeline_mode=pl.Buffered(1)),
            pl.BlockSpec((LANES, 2 * GLA_DK), const),
            pl.BlockSpec((1, 2 * GLA_DK), const),
            pl.BlockSpec((1, HEAD_DIM), const),
            pl.BlockSpec((1, HEAD_DIM), const),
            pl.BlockSpec((tm, HEAD_DIM), pos),
            pl.BlockSpec((tm, HEAD_DIM), pos),
            pl.BlockSpec((1, N_BRANCH * D_MODEL), const),
        ],
        out_specs=[pl.BlockSpec((tm, w), row) for w, _ in widths],
        scratch_shapes=[pltpu.VMEM((tm, D_MODEL), BF16)],
        compiler_params=_params("parallel"),
        name="in_proj_rope" if use_rope else "in_proj",
    )(x2, mod3, gpre, w_pk, wdec, bdec, qg, kg, cos_t, sin_t, bgate)


def _attn_kernel(*refs, n_lat_blocks):
    if n_lat_blocks:
        (q_ref, kl_ref, vl_ref, kc_ref, vc_ref, o_ref, qs_ref, m_ref, l_ref, alpha_ref, acc_ref,
         sc_ref, pc_ref, sa_ref, sb_ref, pa_ref, pb_ref) = refs
    else:
        (q_ref, kc_ref, vc_ref, o_ref, qs_ref, m_ref, l_ref, alpha_ref, acc_ref,
         sc_ref, pc_ref) = refs
    tq = q_ref.shape[0]
    for g in range(GROUP):
        qs_ref[g * tq:(g + 1) * tq, :] = q_ref[:, g * HEAD_DIM:(g + 1) * HEAD_DIM]
    m_ref[...] = jnp.full(m_ref.shape, -jnp.inf, F32)
    l_ref[...] = jnp.zeros(l_ref.shape, F32)
    acc_ref[...] = jnp.zeros(acc_ref.shape, F32)

    def step(scores=None, softmax=None, pv=None):
        for r in range(0, GROUP * tq, ATT_ROW_TILE):
            rows = slice(r, r + ATT_ROW_TILE)
            if scores is not None:
                k, s_dst = scores
                s_dst[rows, :] = _dot_nt(qs_ref[rows, :], k)
            if pv is not None:
                p_src, v = pv
                acc_ref[rows, :] = alpha_ref[rows, :] * acc_ref[rows, :] + _dot(p_src[rows, :], v)
            if softmax is not None:
                s_src, p_dst = softmax
                m_old = m_ref[rows, :]
                m_new = jnp.maximum(m_old, jnp.max(s_src[rows, :], axis=-1, keepdims=True))
                alpha = jnp.exp2(m_old - m_new)
                p = jnp.exp2(s_src[rows, :] - jnp.tile(m_new, (1, s_src.shape[1] // LANES)))
                l_ref[rows, :] = alpha * l_ref[rows, :] + jnp.sum(p, axis=-1, keepdims=True)
                p_dst[rows, :] = p.astype(BF16)
                alpha_ref[rows, :] = alpha
                m_ref[rows, :] = m_new

    def lat(ref, j):
        return ref[pl.ds(pl.multiple_of(j * ATT_KV, ATT_KV), ATT_KV), :]

    if n_lat_blocks:
        def pair(i, carry):
            t = 2 * i
            step(scores=(lat(kl_ref, t), sa_ref), pv=(pa_ref, lat(vl_ref, t - 2)),
                 softmax=(sb_ref, pb_ref))
            step(scores=(lat(kl_ref, t + 1), sb_ref), pv=(pb_ref, lat(vl_ref, t - 1)),
                 softmax=(sa_ref, pa_ref))
            return carry

        n = n_lat_blocks
        step(scores=(lat(kl_ref, 0), sa_ref))
        step(scores=(lat(kl_ref, 1), sb_ref), softmax=(sa_ref, pa_ref))
        lax.fori_loop(1, n // 2, pair, 0)
        step(scores=(kc_ref[...], sc_ref), pv=(pa_ref, lat(vl_ref, n - 2)), softmax=(sb_ref, pb_ref))
        step(pv=(pb_ref, lat(vl_ref, n - 1)), softmax=(sc_ref, pc_ref))
    else:
        step(scores=(kc_ref[...], sc_ref))
        step(softmax=(sc_ref, pc_ref))
    step(pv=(pc_ref, vc_ref[...]))
    out = acc_ref[...] / l_ref[...]
    for g in range(GROUP):
        o_ref[:, g * HEAD_DIM:(g + 1) * HEAD_DIM] = out[g * tq:(g + 1) * tq, :].astype(BF16)


def _attention(q2, k_lat, v_lat, k_ctx, v_ctx, batch, q_len, lat_len, ctx_len):
    tq = ATT_TQ
    tiles = q_len // tq
    qmap = lambda b, kv, i: (b * tiles + i, kv)
    kvmap = lambda b, kv, i: (b, kv)
    in_specs = [pl.BlockSpec((tq, GROUP * HEAD_DIM), qmap)]
    args = [q2]
    n_lat_blocks = 0
    if k_lat is not None:
        n_lat_blocks = lat_len // ATT_KV
        in_specs += [pl.BlockSpec((lat_len, HEAD_DIM), kvmap), pl.BlockSpec((lat_len, HEAD_DIM), kvmap)]
        args += [k_lat, v_lat]
    in_specs += [pl.BlockSpec((ctx_len, HEAD_DIM), kvmap), pl.BlockSpec((ctx_len, HEAD_DIM), kvmap)]
    args += [k_ctx, v_ctx]
    rows = GROUP * tq
    stat = pltpu.VMEM((rows, LANES), F32)
    scratch = [pltpu.VMEM((rows, HEAD_DIM), BF16), stat, stat, stat,
               pltpu.VMEM((rows, HEAD_DIM), F32),
               pltpu.VMEM((rows, ctx_len), F32), pltpu.VMEM((rows, ctx_len), BF16)]
    if n_lat_blocks:
        assert n_lat_blocks % 2 == 0
        scratch += [pltpu.VMEM((rows, ATT_KV), F32)] * 2 + [pltpu.VMEM((rows, ATT_KV), BF16)] * 2
    return pl.pallas_call(
        functools.partial(_attn_kernel, n_lat_blocks=n_lat_blocks),
        out_shape=jax.ShapeDtypeStruct(q2.shape, BF16),
        grid=(batch, N_KV_HEADS, tiles),
        in_specs=in_specs,
        out_specs=pl.BlockSpec((tq, GROUP * HEAD_DIM), qmap),
        scratch_shapes=scratch,
        compiler_params=_params("parallel", "parallel", "parallel"),
        name="gqa_lat" if n_lat_blocks else "gqa_ctx",
    )(*args)


def _split3(x):
    hi = x.astype(BF16)
    r1 = x - hi.astype(F32)
    mid = r1.astype(BF16)
    lo = (r1 - mid.astype(F32)).astype(BF16)
    return hi, mid, lo


def _gla_consts(rev):
    c = GLA_C
    tri_r = lax.broadcasted_iota(jnp.int32, (c, c), 0)
    tri_c = lax.broadcasted_iota(jnp.int32, (c, c), 1)
    tri = jnp.where((tri_c >= tri_r) if rev else (tri_c <= tri_r), 1.0, 0.0).astype(BF16)
    row = lax.broadcasted_iota(jnp.int32, (c, LANES), 0)
    lane = lax.broadcasted_iota(jnp.int32, (c, LANES), 1)
    level_masks = []
    for i, m in enumerate(GLA_LEVELS):
        key = lane - c * (i % 2)
        in_half = (key >= 0) & (key < c)
        same = (row // (2 * m)) == (key // (2 * m))
        q_upper = (row % (2 * m)) >= m
        k_upper = (key % (2 * m)) >= m
        pair = (~q_upper & k_upper) if rev else (q_upper & ~k_upper)
        level_masks.append(in_half & same & pair)
    sub = row % GLA_FINE
    d_of_lane = lane if rev else (LANES - lane) % LANES
    ok = (sub + d_of_lane < GLA_FINE) if rev else (sub >= d_of_lane)
    lane_code = jnp.where((d_of_lane < GLA_FINE) & ok, d_of_lane, -1)
    return tri, level_masks, lane_code


def _gla_kernel(qf_ref, kf_ref, vf_ref, laf_ref, qb_ref, kb_ref, vb_ref, lab_ref, s0f_ref, s0b_ref,
                of_ref, ob_ref, sff_ref, sfb_ref, st_ref):
    @pl.when(pl.program_id(1) == 0)
    def _():
        st_ref[0] = s0f_ref[...]
        st_ref[1] = s0b_ref[...]

    c = GLA_C
    n_chunks = qf_ref.shape[0] // c
    dirs = ((0, False, qf_ref, kf_ref, vf_ref, laf_ref, of_ref, _gla_consts(False)),
            (1, True, qb_ref, kb_ref, vb_ref, lab_ref, ob_ref, _gla_consts(True)))

    def body(ci, carry):
        chains = []
        for di, rev, q_ref, k_ref, v_ref, la_ref, o_ref, consts in dirs:
            cc = (n_chunks - 1 - ci) if rev else ci
            rows = pl.ds(pl.multiple_of(cc * c, c), c)
            cum3 = _dot(consts[0], jnp.concatenate(_split3(la_ref[rows, :]), axis=1))
            b_all = cum3[:, :GLA_DK] + cum3[:, GLA_DK:2 * GLA_DK] + cum3[:, 2 * GLA_DK:]
            for hh in range(GLA_HEADS):
                kc = slice(hh * GLA_DKH, (hh + 1) * GLA_DKH)
                vc = slice(hh * GLA_DVH, (hh + 1) * GLA_DVH)
                chains.append(dict(di=di, hh=hh, rev=rev, rows=rows, vc=vc, o_ref=o_ref,
                                   masks=consts[1], lane_code=consts[2], b=b_all[:, kc],
                                   qf=q_ref[rows, kc].astype(F32), kf=k_ref[rows, kc].astype(F32),
                                   v=v_ref[rows, vc]))

        for w in chains:
            b, qf, kf, rev = w["b"], w["qf"], w["kf"], w["rev"]
            st = st_ref[w["di"], w["hh"]]
            tot = b[0:1, :] if rev else b[c - 1:c, :]
            w["o"] = _dot_nt((qf * jnp.exp2(b)).astype(BF16), st.astype(BF16))
            kd = (kf * jnp.exp2(tot - b)).astype(BF16)
            st_ref[w["di"], w["hh"]] = st * jnp.exp2(tot) + _dot_tn(w["v"], kd)
            qs, ks = [], []
            for m in GLA_LEVELS:
                pivot = (m - 1) if rev else m
                bm = b.reshape(c // (2 * m), 2 * m, GLA_DKH)
                f = jnp.exp2(-jnp.abs(bm - bm[:, pivot:pivot + 1, :])).reshape(c, GLA_DKH)
                qs.append((qf * f).astype(BF16))
                ks.append((kf * f).astype(BF16))
            w["coarse"] = _dot_nt(jnp.concatenate(qs, axis=0), jnp.concatenate(ks, axis=0))

        grouped = (c // SUBLANES, SUBLANES, GLA_DKH)
        for w in chains:
            qf, kf, rev, lane_code = w["qf"], w["kf"], w["rev"], w["lane_code"]
            w_slots = jnp.where(lane_code == 0, jnp.sum(qf * kf, axis=-1, keepdims=True), 0.0)
            q3, k3, b3 = qf.reshape(grouped), kf.reshape(grouped), w["b"].reshape(grouped)
            for d in range(1, GLA_FINE):
                shift = (SUBLANES - d) if rev else d
                kr = pltpu.roll(k3, shift, 1)
                br = pltpu.roll(b3, shift, 1)
                wd = jnp.sum(q3 * kr * jnp.exp2(jnp.minimum(b3 - br, 0.0)), axis=-1, keepdims=True)
                w_slots = jnp.where(lane_code == d, wd.reshape(c, 1), w_slots)
            w["att"] = pltpu.roll(w_slots, 0, 1, stride=1, stride_axis=0)

        for w in chains:
            att = w["att"]
            for i, mask in enumerate(w["masks"]):
                col0 = (i * c) // LANES * LANES
                att = jnp.where(mask, w["coarse"][i * c:(i + 1) * c, col0:col0 + LANES], att)
            o = w["o"] + _dot(att.astype(BF16), jnp.concatenate([w["v"], w["v"]], axis=0))
            w["o_ref"][w["rows"], w["vc"]] = o.astype(BF16)
        return carry

    lax.fori_loop(0, n_chunks, body, 0)
    sff_ref[...] = st_ref[0]
    sfb_ref[...] = st_ref[1]


def _gla_scan(gq, gk, gv, la_f, la_b, s0_f, s0_b, batch, seq_len):
    rows = min(GLA_ROWS, seq_len)
    nblk = seq_len // rows
    fwd = lambda b, i: (b * nblk + i, 0)
    bwd = lambda b, i: (b * nblk + (nblk - 1 - i), 0)
    smap = lambda b, i: (b, 0, 0, 0)
    state_spec = pl.BlockSpec((None, GLA_HEADS, GLA_DVH, GLA_DKH), smap)
    state_shape = jax.ShapeDtypeStruct((batch, GLA_HEADS, GLA_DVH, GLA_DKH), F32)

    def views(index_map):
        return [pl.BlockSpec((rows, GLA_DK), index_map), pl.BlockSpec((rows, GLA_DK), index_map),
                pl.BlockSpec((rows, GLA_DV), index_map), pl.BlockSpec((rows, GLA_DK), index_map)]

    return pl.pallas_call(
        _gla_kernel,
        out_shape=[jax.ShapeDtypeStruct(gv.shape, BF16), jax.ShapeDtypeStruct(gv.shape, BF16),
                   state_shape, state_shape],
        grid=(batch, nblk),
        in_specs=views(fwd) + views(bwd) + [state_spec, state_spec],
        out_specs=[pl.BlockSpec((rows, GLA_DV), fwd), pl.BlockSpec((rows, GLA_DV), bwd),
                   state_spec, state_spec],
        scratch_shapes=[pltpu.VMEM((2, GLA_HEADS, GLA_DVH, GLA_DKH), F32)],
        compiler_params=_params("parallel", "arbitrary"),
        name="gla_bidir",
    )(gq, gk, gv, la_f, gq, gk, gv, la_b, s0_f, s0_b)


def _merge_kernel(ua_ref, uprev_ref, unext_ref, wa_ref, att_ref, sza_ref, of_ref, ob_ref, szg_ref,
                  gate_ref, x_ref, mod_ref, convw_ref, glag_ref, gpost_ref,
                  wconv_ref, watt_ref, wgla_ref, wout_ref, o_ref, *, tiles_per_seq):
    tm = x_ref.shape[0]
    ti = pl.program_id(0) % tiles_per_seq
    u = ua_ref[...].astype(F32)
    prev_row = jnp.where(ti == 0, 0.0, uprev_ref[BF16_ROWS - 1:BF16_ROWS, :].astype(F32))
    next_row = jnp.where(ti == tiles_per_seq - 1, 0.0, unext_ref[0:1, :].astype(F32))
    rid = lax.broadcasted_iota(jnp.int32, (tm, 1), 0)
    u_prev = jnp.where(rid == 0, prev_row, pltpu.roll(u, 1, 0))
    u_next = jnp.where(rid == tm - 1, next_row, pltpu.roll(u, tm - 1, 0))
    cw = convw_ref[...]
    conv = cw[0:1, :] * u_prev + cw[1:2, :] * u + cw[2:3, :] * u_next
    br_a = _dot((wa_ref[...].astype(F32) * conv).astype(BF16), wconv_ref[...])
    br_b = _dot((att_ref[...].astype(F32) * sza_ref[...].astype(F32)).astype(BF16), watt_ref[...])
    parts = []
    for hh in range(GLA_HEADS):
        cols = slice(hh * GLA_DVH, (hh + 1) * GLA_DVH)
        oh = of_ref[:, cols].astype(F32) + ob_ref[:, cols].astype(F32)
        parts.append((_head_norm(oh, glag_ref[...]) * szg_ref[:, cols].astype(F32)).astype(BF16))
    br_c = _dot(jnp.concatenate(parts, axis=1), wgla_ref[...])
    merged = (gate_ref[:, :D_MODEL].astype(F32) * br_a
              + gate_ref[:, D_MODEL:2 * D_MODEL].astype(F32) * br_b
              + gate_ref[:, 2 * D_MODEL:].astype(F32) * br_c)
    out = _dot(merged.astype(BF16), wout_ref[...])
    gate = mod_ref[:, 2 * D_MODEL:]
    o_ref[...] = x_ref[...] + gate * _head_norm(out, gpost_ref[...])


def _merge(ua, wa, att, sza, o_f, o_b, szg, gates, x2, mod3, mod_row0, seq_len,
           convw, glag, gpost, wconv, watt, wgla, wout):
    n = x2.shape[0]
    tm = PROJ_TM
    tps = seq_len // tm
    halo = tm // BF16_ROWS
    n_halo = n // BF16_ROWS
    const = lambda i: (0, 0)
    row = lambda i: (i, 0)
    if mod_row0 is None:
        mod_map = lambda i: (i // tps, 0, 0)
    else:
        mod_map = lambda i: (mod_row0, 0, 0)
    tok = lambda w: pl.BlockSpec((tm, w), row)
    wspec = pl.BlockSpec((D_MODEL, D_MODEL), const)
    return pl.pallas_call(
        functools.partial(_merge_kernel, tiles_per_seq=tps),
        out_shape=jax.ShapeDtypeStruct((n, D_MODEL), F32),
        grid=(n // tm,),
        in_specs=[
            tok(CONV_W),
            pl.BlockSpec((BF16_ROWS, CONV_W), lambda i: (jnp.maximum(i * halo - 1, 0), 0)),
            pl.BlockSpec((BF16_ROWS, CONV_W), lambda i: (jnp.minimum((i + 1) * halo, n_halo - 1), 0)),
            tok(CONV_W), tok(Q_W), tok(Q_W), tok(GLA_DV), tok(GLA_DV), tok(GLA_DV),
            tok(N_BRANCH * D_MODEL), tok(D_MODEL),
            pl.BlockSpec((None, 1, 3 * D_MODEL), mod_map),
            pl.BlockSpec((3, CONV_W), const),
            pl.BlockSpec((1, GLA_DVH), const),
            pl.BlockSpec((1, D_MODEL), const),
            wspec, wspec, wspec, wspec,
        ],
        out_specs=tok(D_MODEL),
        compiler_params=_params("parallel"),
        name="merge_out",
    )(ua, ua, ua, wa, att, sza, o_f, o_b, szg, gates, x2, mod3, convw, glag, gpost,
      wconv, watt, wgla, wout)


def _rope_tables(n_tokens):
    n_rows = n_tokens // GRID_W
    row = jnp.repeat(jnp.arange(n_rows, dtype=F32), GRID_W)
    col = jnp.tile(jnp.arange(GRID_W, dtype=F32), n_rows)
    freqs = ROPE_THETA ** (-jnp.arange(ROPE_PAIRS, dtype=F32) * 2.0 / ROPE_AXIS_DIM)
    ar, ac = row[:, None] * freqs, col[:, None] * freqs
    cos_t = jnp.concatenate([jnp.cos(ar), jnp.cos(ar), jnp.cos(ac), jnp.cos(ac)], axis=1)
    sin_t = jnp.concatenate([-jnp.sin(ar), jnp.sin(ar), -jnp.sin(ac), jnp.sin(ac)], axis=1)
    return cos_t, sin_t


def _pack_w_in(w):
    tail = ORIG_R + 2 * GLA_RANK
    pad = jnp.zeros((D_MODEL, LANES - 2 * GLA_RANK), w.dtype)
    return jnp.concatenate([w[:, :ORIG_R], w[:, tail:], w[:, ORIG_R:tail], pad], axis=1).astype(BF16)


def _pack_decay(w_f, b_f, w_b, b_b):
    wd = jnp.zeros((LANES, 2 * GLA_DK), F32)
    wd = wd.at[:GLA_RANK, :GLA_DK].set(w_f).at[GLA_RANK:2 * GLA_RANK, GLA_DK:].set(w_b)
    return wd.astype(BF16), jnp.concatenate([b_f, b_b])[None, :]


def kernel(x, c, ctx, c_ctx, w_ada, b_ada, g_pre, g_post, w_in, conv_w, q_norm_g, k_norm_g,
           w_decay_fwd, b_decay_fwd, w_decay_bwd, b_decay_bwd, gla_norm_g,
           w_br_conv, w_br_attn, w_br_gla, b_gate, w_out):
    batch, seq, _ = x.shape
    ctx_len = ctx.shape[1]
    assert seq % max(PROJ_TM, ATT_TQ, ATT_KV, GLA_ROWS) == 0 and seq % GRID_W == 0
    assert ctx_len % max(PROJ_TM, ATT_TQ) == 0 and ctx_len % GLA_C == 0

    mod_rows = -(-(batch + 1) // SUBLANES) * SUBLANES
    cvec = jnp.zeros((mod_rows, D_MODEL), F32).at[:batch].set(c).at[batch].set(c_ctx)
    mod = _modulation(cvec, w_ada, b_ada)
    cos_t, sin_t = _rope_tables(seq)
    zero_state = jnp.zeros((batch, GLA_HEADS, GLA_DVH, GLA_DKH), F32)

    xl = x.reshape(batch * seq, D_MODEL)
    xc = ctx.reshape(batch * ctx_len, D_MODEL)
    for l in range(DEPTH):
        last = l == DEPTH - 1
        mod3 = mod[l][:, None, :]
        w_pk = _pack_w_in(w_in[l])
        wdec, bdec = _pack_decay(w_decay_fwd[l], b_decay_fwd[l], w_decay_bwd[l], b_decay_bwd[l])
        shared_in = (g_pre[l][None], w_pk, wdec, bdec, q_norm_g[l][None], k_norm_g[l][None])
        bgate = b_gate[l][None]
        pc = _projection(xc, mod3, batch, ctx_len, *shared_in, cos_t, sin_t, bgate, use_rope=False)
        pl_ = _projection(xl, mod3, None, seq, *shared_in, cos_t, sin_t, bgate, use_rope=True)
        (ua_c, wa_c, q_c, k_c, v_c, sza_c, gq_c, gk_c, gv_c, laf_c, lab_c, szg_c, gate_c) = pc
        (ua_l, wa_l, q_l, k_l, v_l, sza_l, gq_l, gk_l, gv_l, laf_l, lab_l, szg_l, gate_l) = pl_

        att_l = _attention(q_l, k_l, v_l, k_c, v_c, batch, seq, seq, ctx_len)
        of_c, ob_c, s_f, s_b = _gla_scan(gq_c, gk_c, gv_c, laf_c, lab_c, zero_state, zero_state,
                                         batch, ctx_len)
        of_l, ob_l, _, _ = _gla_scan(gq_l, gk_l, gv_l, laf_l, lab_l, s_f, s_b, batch, seq)

        shared_out = (conv_w[l], gla_norm_g[l][None], g_post[l][None],
                      w_br_conv[l].astype(BF16), w_br_attn[l].astype(BF16),
                      w_br_gla[l].astype(BF16), w_out[l].astype(BF16))
        if not last:
            att_c = _attention(q_c, None, None, k_c, v_c, batch, ctx_len, 0, ctx_len)
            xc = _merge(ua_c, wa_c, att_c, sza_c, of_c, ob_c, szg_c, gate_c, xc, mod3, batch, ctx_len,
                        *shared_out)
        xl = _merge(ua_l, wa_l, att_l, sza_l, of_l, ob_l, szg_l, gate_l, xl, mod3, None, seq,
                    *shared_out)
    return xl.reshape(batch, seq, D_MODEL)
```

```python
import functools

import jax
import jax.numpy as jnp
import numpy as np
from jax import lax
from jax.experimental import pallas as pl
from jax.experimental.pallas import tpu as pltpu

F32 = jnp.float32
BF16 = jnp.bfloat16

D_MODEL = 1024
DEPTH = 2
GRID_W = 64
CONV_W = 1024
N_HEADS = 8
N_KV_HEADS = 2
HEAD_DIM = 128
GROUP = N_HEADS // N_KV_HEADS
ROPE_THETA = 10000.0
ROPE_AXIS_DIM = HEAD_DIM // 2
ROPE_PAIRS = ROPE_AXIS_DIM // 2
ATTN_SCALE = HEAD_DIM ** -0.5
LOG2_E = 1.4426950408889634
GLA_HEADS = 4
GLA_DK = D_MODEL // 2
GLA_DV = D_MODEL
GLA_DKH = GLA_DK // GLA_HEADS
GLA_DVH = GLA_DV // GLA_HEADS
GLA_RANK = 16
GLA_TAU = 16.0
N_BRANCH = 3
EPS = 1e-6

Q_W = N_HEADS * HEAD_DIM
KV_W = N_KV_HEADS * HEAD_DIM

LANES = 128
SUBLANES = 8
BF16_ROWS = 16
VMEM_LIMIT = 56 * 1024 * 1024

OFF_A_B = 0
OFF_A_C = OFF_A_B + CONV_W
OFF_A_X = OFF_A_C + CONV_W
OFF_A_Z = OFF_A_X + CONV_W
OFF_Q = OFF_A_Z + CONV_W
OFF_K = OFF_Q + Q_W
OFF_V = OFF_K + KV_W
OFF_Z_ATT = OFF_V + KV_W
OFF_GQ = OFF_Z_ATT + Q_W
OFF_GK = OFF_GQ + GLA_DK
OFF_GV = OFF_GK + GLA_DK
OFF_Z_GLA = OFF_GV + GLA_DV
OFF_MG = OFF_Z_GLA + GLA_DV
OFF_R = OFF_MG + N_BRANCH * D_MODEL
W_PACKED = OFF_R + LANES
ORIG_R = OFF_Z_GLA

PROJ_TM = 256
COL_BLK = 256
ATT_TQ = 256
ATT_KV = 512
ATT_ROW_TILE = 256
GLA_C = 64
GLA_FINE = 4
GLA_ROWS = 512
GLA_LEVELS = tuple(GLA_C >> (i + 1) for i in range(int(np.log2(GLA_C // GLA_FINE))))
assert len(GLA_LEVELS) * GLA_C % LANES == 0 and 2 * GLA_C == LANES and GLA_FINE <= SUBLANES


def _sigmoid(x):
    return jax.nn.sigmoid(x)


def _silu(x):
    return x * _sigmoid(x)


def _dot(a, b):
    return jnp.dot(a, b, preferred_element_type=F32)


def _dot_nt(a, b):
    return lax.dot_general(a, b, (((1,), (1,)), ((), ())), preferred_element_type=F32)


def _dot_tn(a, b):
    return lax.dot_general(a, b, (((0,), (0,)), ((), ())), preferred_element_type=F32)


def _params(*sem):
    return pltpu.CompilerParams(dimension_semantics=sem, vmem_limit_bytes=VMEM_LIMIT)


def _mod_kernel(c_ref, w_ref, b_ref, o_ref):
    s = _silu(c_ref[...])
    o_ref[...] = _dot(s.astype(BF16), w_ref[...].astype(BF16)) + b_ref[...]


def _modulation(cvec, w_ada, b_ada):
    rows = cvec.shape[0]
    n_col = 3 * D_MODEL // D_MODEL
    return pl.pallas_call(
        _mod_kernel,
        out_shape=jax.ShapeDtypeStruct((DEPTH, rows, 3 * D_MODEL), F32),
        grid=(DEPTH, n_col),
        in_specs=[
            pl.BlockSpec((rows, D_MODEL), lambda l, j: (0, 0)),
            pl.BlockSpec((None, D_MODEL, D_MODEL), lambda l, j: (l, 0, j)),
            pl.BlockSpec((None, 1, D_MODEL), lambda l, j: (l, 0, j)),
        ],
        out_specs=pl.BlockSpec((None, rows, D_MODEL), lambda l, j: (l, 0, j)),
        compiler_params=_params("parallel", "parallel"),
        name="adaln_mod",
    )(cvec, w_ada, b_ada.reshape(DEPTH, 1, 3 * D_MODEL))


def _head_norm(xh, g):
    ms = jnp.mean(xh * xh, axis=-1, keepdims=True)
    return xh * lax.rsqrt(ms + EPS) * g


def _rope(xh, cos, sin):
    lane = lax.broadcasted_iota(jnp.int32, xh.shape, 1)
    first_half = (lane % ROPE_AXIS_DIM) < ROPE_PAIRS
    partner = jnp.where(first_half,
                        pltpu.roll(xh, HEAD_DIM - ROPE_PAIRS, 1),
                        pltpu.roll(xh, ROPE_PAIRS, 1))
    return xh * cos + partner * sin


def _log_sigmoid(x):
    return jnp.minimum(x, 0.0) - jnp.log1p(jnp.exp(-jnp.abs(x)))


def _proj_kernel(x_ref, mod_ref, gpre_ref, w_ref, wdec_ref, bdec_ref, qg_ref, kg_ref,
                 cos_ref, sin_ref, bgate_ref,
                 ua_ref, wa_ref, q_ref, k_ref, vt_ref, sza_ref, gq_ref, gk_ref, gv_ref,
                 laf_ref, lab_ref, szg_ref, gate_ref, h_ref, v_ref, *, use_rope):
    x = x_ref[...]
    ms = jnp.mean(x * x, axis=-1, keepdims=True)
    y = x * lax.rsqrt(ms + EPS) * gpre_ref[...]
    mod = mod_ref[...]
    shift = mod[:, :D_MODEL]
    scale = mod[:, D_MODEL:2 * D_MODEL]
    h_ref[...] = (y * (1.0 + scale) + shift).astype(BF16)

    def proj(off, width=COL_BLK):
        return _dot(h_ref[...], w_ref[:, off:off + width])

    for o in range(0, CONV_W, COL_BLK):
        cols = slice(o, o + COL_BLK)
        ua_ref[:, cols] = (proj(OFF_A_C + o) * proj(OFF_A_X + o)).astype(BF16)
        wa_ref[:, cols] = (proj(OFF_A_B + o) * _silu(proj(OFF_A_Z + o))).astype(BF16)

    def heads(off, width, gain, out_ref, out_scale):
        for o in range(0, width, COL_BLK):
            blk = proj(off + o)
            for hh in range(COL_BLK // HEAD_DIM):
                xh = _head_norm(blk[:, hh * HEAD_DIM:(hh + 1) * HEAD_DIM], gain)
                if use_rope:
                    xh = _rope(xh, cos_ref[...], sin_ref[...])
                c0 = o + hh * HEAD_DIM
                out_ref[:, c0:c0 + HEAD_DIM] = (xh * out_scale).astype(BF16)

    heads(OFF_Q, Q_W, qg_ref[...], q_ref, ATTN_SCALE * LOG2_E)
    heads(OFF_K, KV_W, kg_ref[...], k_ref, 1.0)
    v_ref[...] = proj(OFF_V, KV_W)
    vt_ref[...] = v_ref[...].T.astype(BF16)
    for o in range(0, Q_W, COL_BLK):
        sza_ref[:, o:o + COL_BLK] = _silu(proj(OFF_Z_ATT + o)).astype(BF16)

    for o in range(0, GLA_DK, COL_BLK):
        gq_ref[:, o:o + COL_BLK] = (proj(OFF_GQ + o) * (GLA_DKH ** -0.5)).astype(BF16)
        gk_ref[:, o:o + COL_BLK] = proj(OFF_GK + o).astype(BF16)
    for o in range(0, GLA_DV, COL_BLK):
        gv_ref[:, o:o + COL_BLK] = proj(OFF_GV + o).astype(BF16)
        szg_ref[:, o:o + COL_BLK] = _silu(proj(OFF_Z_GLA + o)).astype(BF16)
    r = proj(OFF_R, LANES).astype(BF16)
    for o in range(0, GLA_DK, COL_BLK):
        laf_ref[:, o:o + COL_BLK] = _log_sigmoid(
            _dot(r, wdec_ref[:, o:o + COL_BLK]) + bdec_ref[:, o:o + COL_BLK]) * (LOG2_E / GLA_TAU)
        ob = GLA_DK + o
        lab_ref[:, o:o + COL_BLK] = _log_sigmoid(
            _dot(r, wdec_ref[:, ob:ob + COL_BLK]) + bdec_ref[:, ob:ob + COL_BLK]) * (LOG2_E / GLA_TAU)

    for o in range(0, N_BRANCH * D_MODEL, COL_BLK):
        gate_ref[:, o:o + COL_BLK] = _sigmoid(
            proj(OFF_MG + o) + bgate_ref[:, o:o + COL_BLK]).astype(BF16)


def _projection(x2, mod3, mod_row0, seq_len, gpre, w_pk, wdec, bdec, qg, kg, cos_t, sin_t, bgate,
                use_rope):
    n = x2.shape[0]
    tm = PROJ_TM
    tps = seq_len // tm
    const = lambda i: (0, 0)
    row = lambda i: (i, 0)
    pos = lambda i: (i % tps, 0)
    if mod_row0 is None:
        mod_map = lambda i: (i // tps, 0, 0)
    else:
        mod_map = lambda i: (mod_row0, 0, 0)
    VT_OUT = 4
    widths = [(CONV_W, BF16), (CONV_W, BF16), (Q_W, BF16), (KV_W, BF16), (KV_W, BF16), (Q_W, BF16),
              (GLA_DK, BF16), (GLA_DK, BF16), (GLA_DV, BF16), (GLA_DK, F32), (GLA_DK, F32),
              (GLA_DV, BF16), (N_BRANCH * D_MODEL, BF16)]
    return pl.pallas_call(
        functools.partial(_proj_kernel, use_rope=use_rope),
        out_shape=[jax.ShapeDtypeStruct((w, n) if i == VT_OUT else (n, w), dt)
                   for i, (w, dt) in enumerate(widths)],
        grid=(n // tm,),
        in_specs=[
            pl.BlockSpec((tm, D_MODEL), row),
            pl.BlockSpec((None, 1, 3 * D_MODEL), mod_map),
            pl.BlockSpec((1, D_MODEL), const),
            pl.BlockSpec((D_MODEL, W_PACKED), const, pipeline_mode=pl.Buffered(1)),
            pl.BlockSpec((LANES, 2 * GLA_DK), const),
            pl.BlockSpec((1, 2 * GLA_DK), const),
            pl.BlockSpec((1, HEAD_DIM), const),
            pl.BlockSpec((1, HEAD_DIM), const),
            pl.BlockSpec((tm, HEAD_DIM), pos),
            pl.BlockSpec((tm, HEAD_DIM), pos),
            pl.BlockSpec((1, N_BRANCH * D_MODEL), const),
        ],
        out_specs=[pl.BlockSpec((w, tm), lambda i: (0, i)) if j == VT_OUT else pl.BlockSpec((tm, w), row)
                   for j, (w, _) in enumerate(widths)],
        scratch_shapes=[pltpu.VMEM((tm, D_MODEL), BF16), pltpu.VMEM((tm, KV_W), F32)],
        compiler_params=_params("parallel"),
        name="in_proj_rope" if use_rope else "in_proj",
    )(x2, mod3, gpre, w_pk, wdec, bdec, qg, kg, cos_t, sin_t, bgate)


def _attn_kernel(*refs, n_lat_blocks):
    if n_lat_blocks:
        (q_ref, kl_ref, vtl_ref, kc_ref, vtc_ref, o_ref, m_ref, l_ref, alpha_ref, acc_ref,
         sc_ref, pc_ref, sa_ref, sb_ref, pa_ref, pb_ref) = refs
    else:
        q_ref, kc_ref, vtc_ref, o_ref, m_ref, l_ref, alpha_ref, acc_ref, sc_ref, pc_ref = refs
    tq = q_ref.shape[0]
    m_ref[...] = jnp.full(m_ref.shape, -jnp.inf, F32)
    l_ref[...] = jnp.zeros(l_ref.shape, F32)
    acc_ref[...] = jnp.zeros(acc_ref.shape, F32)

    def step(scores=None, softmax=None, values=None):
        heads = [slice(g * tq, (g + 1) * tq) for g in range(GROUP)]
        if scores is not None:
            k, s_dst = scores
            for g, cols in enumerate(heads):
                s_dst[:, cols] = _dot_nt(k, q_ref[:, g * HEAD_DIM:(g + 1) * HEAD_DIM])
        if values is not None:
            p_src, vt = values
            for cols in heads:
                acc_ref[:, cols] = alpha_ref[:, cols] * acc_ref[:, cols] + _dot(vt, p_src[:, cols])
        if softmax is not None:
            s_src, p_dst = softmax
            for cols in heads:
                m_old = m_ref[:, cols]
                m_new = jnp.maximum(m_old, jnp.max(s_src[:, cols], axis=0, keepdims=True))
                alpha = jnp.exp2(m_old - m_new)
                p = jnp.exp2(s_src[:, cols] - m_new)
                l_ref[:, cols] = alpha * l_ref[:, cols] + jnp.sum(p, axis=0, keepdims=True)
                p_dst[:, cols] = p.astype(BF16)
                alpha_ref[:, cols] = alpha
                m_ref[:, cols] = m_new

    def keys(j):
        return pl.ds(pl.multiple_of(j * ATT_KV, ATT_KV), ATT_KV)

    if n_lat_blocks:
        def pair(i, carry):
            t = 2 * i
            step(scores=(kl_ref[keys(t), :], sa_ref), values=(pa_ref, vtl_ref[:, keys(t - 2)]),
                 softmax=(sb_ref, pb_ref))
            step(scores=(kl_ref[keys(t + 1), :], sb_ref), values=(pb_ref, vtl_ref[:, keys(t - 1)]),
                 softmax=(sa_ref, pa_ref))
            return carry

        n = n_lat_blocks
        step(scores=(kl_ref[keys(0), :], sa_ref))
        step(scores=(kl_ref[keys(1), :], sb_ref), softmax=(sa_ref, pa_ref))
        lax.fori_loop(1, n // 2, pair, 0)
        step(scores=(kc_ref[...], sc_ref), values=(pa_ref, vtl_ref[:, keys(n - 2)]),
             softmax=(sb_ref, pb_ref))
        step(values=(pb_ref, vtl_ref[:, keys(n - 1)]), softmax=(sc_ref, pc_ref))
    else:
        step(scores=(kc_ref[...], sc_ref))
        step(softmax=(sc_ref, pc_ref))
    step(values=(pc_ref, vtc_ref[...]))
    out_t = acc_ref[...] / l_ref[...]
    for g in range(GROUP):
        o_ref[:, g * HEAD_DIM:(g + 1) * HEAD_DIM] = out_t[:, g * tq:(g + 1) * tq].T.astype(BF16)


def _attention(q2, k_lat, vt_lat, k_ctx, vt_ctx, batch, q_len, lat_len, ctx_len):
    tq = ATT_TQ
    tiles = q_len // tq
    qmap = lambda b, kv, i: (b * tiles + i, kv)
    kmap = lambda b, kv, i: (b, kv)
    vtmap = lambda b, kv, i: (kv, b)
    in_specs = [pl.BlockSpec((tq, GROUP * HEAD_DIM), qmap)]
    args = [q2]
    n_lat_blocks = 0
    if k_lat is not None:
        n_lat_blocks = lat_len // ATT_KV
        in_specs += [pl.BlockSpec((lat_len, HEAD_DIM), kmap), pl.BlockSpec((HEAD_DIM, lat_len), vtmap)]
        args += [k_lat, vt_lat]
    in_specs += [pl.BlockSpec((ctx_len, HEAD_DIM), kmap), pl.BlockSpec((HEAD_DIM, ctx_len), vtmap)]
    args += [k_ctx, vt_ctx]
    rows = GROUP * tq
    stat = pltpu.VMEM((1, rows), F32)
    scratch = [stat, stat, stat, pltpu.VMEM((HEAD_DIM, rows), F32),
               pltpu.VMEM((ctx_len, rows), F32), pltpu.VMEM((ctx_len, rows), BF16)]
    if n_lat_blocks:
        assert n_lat_blocks % 2 == 0
        scratch += [pltpu.VMEM((ATT_KV, rows), F32)] * 2 + [pltpu.VMEM((ATT_KV, rows), BF16)] * 2
    return pl.pallas_call(
        functools.partial(_attn_kernel, n_lat_blocks=n_lat_blocks),
        out_shape=jax.ShapeDtypeStruct(q2.shape, BF16),
        grid=(batch, N_KV_HEADS, tiles),
        in_specs=in_specs,
        out_specs=pl.BlockSpec((tq, GROUP * HEAD_DIM), qmap),
        scratch_shapes=scratch,
        compiler_params=_params("parallel", "parallel", "parallel"),
        name="gqa_lat" if n_lat_blocks else "gqa_ctx",
    )(*args)


def _split3(x):
    hi = x.astype(BF16)
    r1 = x - hi.astype(F32)
    mid = r1.astype(BF16)
    lo = (r1 - mid.astype(F32)).astype(BF16)
    return hi, mid, lo


def _gla_consts(rev):
    c = GLA_C
    tri_r = lax.broadcasted_iota(jnp.int32, (c, c), 0)
    tri_c = lax.broadcasted_iota(jnp.int32, (c, c), 1)
    tri = jnp.where((tri_c >= tri_r) if rev else (tri_c <= tri_r), 1.0, 0.0).astype(BF16)
    row = lax.broadcasted_iota(jnp.int32, (c, LANES), 0)
    lane = lax.broadcasted_iota(jnp.int32, (c, LANES), 1)
    level_masks = []
    for i, m in enumerate(GLA_LEVELS):
        key = lane - c * (i % 2)
        in_half = (key >= 0) & (key < c)
        same = (row // (2 * m)) == (key // (2 * m))
        q_upper = (row % (2 * m)) >= m
        k_upper = (key % (2 * m)) >= m
        pair = (~q_upper & k_upper) if rev else (q_upper & ~k_upper)
        level_masks.append(in_half & same & pair)
    sub = row % GLA_FINE
    d_of_lane = lane if rev else (LANES - lane) % LANES
    ok = (sub + d_of_lane < GLA_FINE) if rev else (sub >= d_of_lane)
    lane_code = jnp.where((d_of_lane < GLA_FINE) & ok, d_of_lane, -1)
    return tri, level_masks, lane_code


def _gla_kernel(qf_ref, kf_ref, vf_ref, laf_ref, qb_ref, kb_ref, vb_ref, lab_ref, s0f_ref, s0b_ref,
                of_ref, ob_ref, sff_ref, sfb_ref, st_ref):
    @pl.when(pl.program_id(1) == 0)
    def _():
        st_ref[0] = s0f_ref[...]
        st_ref[1] = s0b_ref[...]

    c = GLA_C
    n_chunks = qf_ref.shape[0] // c
    dirs = ((0, False, qf_ref, kf_ref, vf_ref, laf_ref, of_ref, _gla_consts(False)),
            (1, True, qb_ref, kb_ref, vb_ref, lab_ref, ob_ref, _gla_consts(True)))

    def body(ci, carry):
        chains = []
        for di, rev, q_ref, k_ref, v_ref, la_ref, o_ref, consts in dirs:
            cc = (n_chunks - 1 - ci) if rev else ci
            rows = pl.ds(pl.multiple_of(cc * c, c), c)
            cum3 = _dot(consts[0], jnp.concatenate(_split3(la_ref[rows, :]), axis=1))
            b_all = cum3[:, :GLA_DK] + cum3[:, GLA_DK:2 * GLA_DK] + cum3[:, 2 * GLA_DK:]
            for hh in range(GLA_HEADS):
                kc = slice(hh * GLA_DKH, (hh + 1) * GLA_DKH)
                vc = slice(hh * GLA_DVH, (hh + 1) * GLA_DVH)
                chains.append(dict(di=di, hh=hh, rev=rev, rows=rows, vc=vc, o_ref=o_ref,
                                   masks=consts[1], lane_code=consts[2], b=b_all[:, kc],
                                   qf=q_ref[rows, kc].astype(F32), kf=k_ref[rows, kc].astype(F32),
                                   v=v_ref[rows, vc]))

        for w in chains:
            b, qf, kf, rev = w["b"], w["qf"], w["kf"], w["rev"]
            st = st_ref[w["di"], w["hh"]]
            tot = b[0:1, :] if rev else b[c - 1:c, :]
            w["o"] = _dot_nt((qf * jnp.exp2(b)).astype(BF16), st.astype(BF16))
            kd = (kf * jnp.exp2(tot - b)).astype(BF16)
            st_ref[w["di"], w["hh"]] = st * jnp.exp2(tot) + _dot_tn(w["v"], kd)
            qs, ks = [], []
            for m in GLA_LEVELS:
                pivot = (m - 1) if rev else m
                bm = b.reshape(c // (2 * m), 2 * m, GLA_DKH)
                f = jnp.exp2(-jnp.abs(bm - bm[:, pivot:pivot + 1, :])).reshape(c, GLA_DKH)
                qs.append((qf * f).astype(BF16))
                ks.append((kf * f).astype(BF16))
            w["coarse"] = _dot_nt(jnp.concatenate(qs, axis=0), jnp.concatenate(ks, axis=0))

        grouped = (c // SUBLANES, SUBLANES, GLA_DKH)
        for w in chains:
            qf, kf, rev, lane_code = w["qf"], w["kf"], w["rev"], w["lane_code"]
            w_slots = jnp.where(lane_code == 0, jnp.sum(qf * kf, axis=-1, keepdims=True), 0.0)
            q3, k3, b3 = qf.reshape(grouped), kf.reshape(grouped), w["b"].reshape(grouped)
            for d in range(1, GLA_FINE):
                shift = (SUBLANES - d) if rev else d
                kr = pltpu.roll(k3, shift, 1)
                br = pltpu.roll(b3, shift, 1)
                wd = jnp.sum(q3 * kr * jnp.exp2(jnp.minimum(b3 - br, 0.0)), axis=-1, keepdims=True)
                w_slots = jnp.where(lane_code == d, wd.reshape(c, 1), w_slots)
            w["att"] = pltpu.roll(w_slots, 0, 1, stride=1, stride_axis=0)

        for w in chains:
            att = w["att"]
            for i, mask in enumerate(w["masks"]):
                col0 = (i * c) // LANES * LANES
                att = jnp.where(mask, w["coarse"][i * c:(i + 1) * c, col0:col0 + LANES], att)
            o = w["o"] + _dot(att.astype(BF16), jnp.concatenate([w["v"], w["v"]], axis=0))
            w["o_ref"][w["rows"], w["vc"]] = o.astype(BF16)
        return carry

    lax.fori_loop(0, n_chunks, body, 0)
    sff_ref[...] = st_ref[0]
    sfb_ref[...] = st_ref[1]


def _gla_scan(gq, gk, gv, la_f, la_b, s0_f, s0_b, batch, seq_len):
    rows = min(GLA_ROWS, seq_len)
    nblk = seq_len // rows
    fwd = lambda b, i: (b * nblk + i, 0)
    bwd = lambda b, i: (b * nblk + (nblk - 1 - i), 0)
    smap = lambda b, i: (b, 0, 0, 0)
    state_spec = pl.BlockSpec((None, GLA_HEADS, GLA_DVH, GLA_DKH), smap)
    state_shape = jax.ShapeDtypeStruct((batch, GLA_HEADS, GLA_DVH, GLA_DKH), F32)

    def views(index_map):
        return [pl.BlockSpec((rows, GLA_DK), index_map), pl.BlockSpec((rows, GLA_DK), index_map),
                pl.BlockSpec((rows, GLA_DV), index_map), pl.BlockSpec((rows, GLA_DK), index_map)]

    return pl.pallas_call(
        _gla_kernel,
        out_shape=[jax.ShapeDtypeStruct(gv.shape, BF16), jax.ShapeDtypeStruct(gv.shape, BF16),
                   state_shape, state_shape],
        grid=(batch, nblk),
        in_specs=views(fwd) + views(bwd) + [state_spec, state_spec],
        out_specs=[pl.BlockSpec((rows, GLA_DV), fwd), pl.BlockSpec((rows, GLA_DV), bwd),
                   state_spec, state_spec],
        scratch_shapes=[pltpu.VMEM((2, GLA_HEADS, GLA_DVH, GLA_DKH), F32)],
        compiler_params=_params("parallel", "arbitrary"),
        name="gla_bidir",
    )(gq, gk, gv, la_f, gq, gk, gv, la_b, s0_f, s0_b)


def _merge_kernel(ua_ref, uprev_ref, unext_ref, wa_ref, att_ref, sza_ref, of_ref, ob_ref, szg_ref,
                  gate_ref, x_ref, mod_ref, convw_ref, glag_ref, gpost_ref,
                  wconv_ref, watt_ref, wgla_ref, wout_ref, o_ref, *, tiles_per_seq):
    tm = x_ref.shape[0]
    ti = pl.program_id(0) % tiles_per_seq
    u = ua_ref[...].astype(F32)
    prev_row = jnp.where(ti == 0, 0.0, uprev_ref[BF16_ROWS - 1:BF16_ROWS, :].astype(F32))
    next_row = jnp.where(ti == tiles_per_seq - 1, 0.0, unext_ref[0:1, :].astype(F32))
    rid = lax.broadcasted_iota(jnp.int32, (tm, 1), 0)
    u_prev = jnp.where(rid == 0, prev_row, pltpu.roll(u, 1, 0))
    u_next = jnp.where(rid == tm - 1, next_row, pltpu.roll(u, tm - 1, 0))
    cw = convw_ref[...]
    conv = cw[0:1, :] * u_prev + cw[1:2, :] * u + cw[2:3, :] * u_next
    br_a = _dot((wa_ref[...].astype(F32) * conv).astype(BF16), wconv_ref[...])
    br_b = _dot(att_ref[...] * sza_ref[...], watt_ref[...])
    parts = []
    for hh in range(GLA_HEADS):
        cols = slice(hh * GLA_DVH, (hh + 1) * GLA_DVH)
        oh = of_ref[:, cols].astype(F32) + ob_ref[:, cols].astype(F32)
        parts.append((_head_norm(oh, glag_ref[...]) * szg_ref[:, cols].astype(F32)).astype(BF16))
    br_c = _dot(jnp.concatenate(parts, axis=1), wgla_ref[...])
    merged = (gate_ref[:, :D_MODEL].astype(F32) * br_a
              + gate_ref[:, D_MODEL:2 * D_MODEL].astype(F32) * br_b
              + gate_ref[:, 2 * D_MODEL:].astype(F32) * br_c)
    out = _dot(merged.astype(BF16), wout_ref[...])
    gate = mod_ref[:, 2 * D_MODEL:]
    o_ref[...] = x_ref[...] + gate * _head_norm(out, gpost_ref[...])


def _merge(ua, wa, att, sza, o_f, o_b, szg, gates, x2, mod3, mod_row0, seq_len,
           convw, glag, gpost, wconv, watt, wgla, wout):
    n = x2.shape[0]
    tm = PROJ_TM
    tps = seq_len // tm
    halo = tm // BF16_ROWS
    n_halo = n // BF16_ROWS
    const = lambda i: (0, 0)
    row = lambda i: (i, 0)
    if mod_row0 is None:
        mod_map = lambda i: (i // tps, 0, 0)
    else:
        mod_map = lambda i: (mod_row0, 0, 0)
    tok = lambda w: pl.BlockSpec((tm, w), row)
    wspec = pl.BlockSpec((D_MODEL, D_MODEL), const)
    return pl.pallas_call(
        functools.partial(_merge_kernel, tiles_per_seq=tps),
        out_shape=jax.ShapeDtypeStruct((n, D_MODEL), F32),
        grid=(n // tm,),
        in_specs=[
            tok(CONV_W),
            pl.BlockSpec((BF16_ROWS, CONV_W), lambda i: (jnp.maximum(i * halo - 1, 0), 0)),
            pl.BlockSpec((BF16_ROWS, CONV_W), lambda i: (jnp.minimum((i + 1) * halo, n_halo - 1), 0)),
            tok(CONV_W), tok(Q_W), tok(Q_W), tok(GLA_DV), tok(GLA_DV), tok(GLA_DV),
            tok(N_BRANCH * D_MODEL), tok(D_MODEL),
            pl.BlockSpec((None, 1, 3 * D_MODEL), mod_map),
            pl.BlockSpec((3, CONV_W), const),
            pl.BlockSpec((1, GLA_DVH), const),
            pl.BlockSpec((1, D_MODEL), const),
            wspec, wspec, wspec, wspec,
        ],
        out_specs=tok(D_MODEL),
        compiler_params=_params("parallel"),
        name="merge_out",
    )(ua, ua, ua, wa, att, sza, o_f, o_b, szg, gates, x2, mod3, convw, glag, gpost,
      wconv, watt, wgla, wout)


def _rope_tables(n_tokens):
    n_rows = n_tokens // GRID_W
    row = jnp.repeat(jnp.arange(n_rows, dtype=F32), GRID_W)
    col = jnp.tile(jnp.arange(GRID_W, dtype=F32), n_rows)
    freqs = ROPE_THETA ** (-jnp.arange(ROPE_PAIRS, dtype=F32) * 2.0 / ROPE_AXIS_DIM)
    ar, ac = row[:, None] * freqs, col[:, None] * freqs
    cos_t = jnp.concatenate([jnp.cos(ar), jnp.cos(ar), jnp.cos(ac), jnp.cos(ac)], axis=1)
    sin_t = jnp.concatenate([-jnp.sin(ar), jnp.sin(ar), -jnp.sin(ac), jnp.sin(ac)], axis=1)
    return cos_t, sin_t


def _pack_w_in(w):
    tail = ORIG_R + 2 * GLA_RANK
    pad = jnp.zeros((D_MODEL, LANES - 2 * GLA_RANK), w.dtype)
    return jnp.concatenate([w[:, :ORIG_R], w[:, tail:], w[:, ORIG_R:tail], pad], axis=1).astype(BF16)


def _pack_decay(w_f, b_f, w_b, b_b):
    wd = jnp.zeros((LANES, 2 * GLA_DK), F32)
    wd = wd.at[:GLA_RANK, :GLA_DK].set(w_f).at[GLA_RANK:2 * GLA_RANK, GLA_DK:].set(w_b)
    return wd.astype(BF16), jnp.concatenate([b_f, b_b])[None, :]


def kernel(x, c, ctx, c_ctx, w_ada, b_ada, g_pre, g_post, w_in, conv_w, q_norm_g, k_norm_g,
           w_decay_fwd, b_decay_fwd, w_decay_bwd, b_decay_bwd, gla_norm_g,
           w_br_conv, w_br_attn, w_br_gla, b_gate, w_out):
    batch, seq, _ = x.shape
    ctx_len = ctx.shape[1]
    assert seq % max(PROJ_TM, ATT_TQ, ATT_KV, GLA_ROWS) == 0 and seq % GRID_W == 0
    assert ctx_len % max(PROJ_TM, ATT_TQ) == 0 and ctx_len % GLA_C == 0

    mod_rows = -(-(batch + 1) // SUBLANES) * SUBLANES
    cvec = jnp.zeros((mod_rows, D_MODEL), F32).at[:batch].set(c).at[batch].set(c_ctx)
    mod = _modulation(cvec, w_ada, b_ada)
    cos_t, sin_t = _rope_tables(seq)
    zero_state = jnp.zeros((batch, GLA_HEADS, GLA_DVH, GLA_DKH), F32)

    xl = x.reshape(batch * seq, D_MODEL)
    xc = ctx.reshape(batch * ctx_len, D_MODEL)
    for l in range(DEPTH):
        last = l == DEPTH - 1
        mod3 = mod[l][:, None, :]
        w_pk = _pack_w_in(w_in[l])
        wdec, bdec = _pack_decay(w_decay_fwd[l], b_decay_fwd[l], w_decay_bwd[l], b_decay_bwd[l])
        shared_in = (g_pre[l][None], w_pk, wdec, bdec, q_norm_g[l][None], k_norm_g[l][None])
        bgate = b_gate[l][None]
        pc = _projection(xc, mod3, batch, ctx_len, *shared_in, cos_t, sin_t, bgate, use_rope=False)
        pl_ = _projection(xl, mod3, None, seq, *shared_in, cos_t, sin_t, bgate, use_rope=True)
        (ua_c, wa_c, q_c, k_c, vt_c, sza_c, gq_c, gk_c, gv_c, laf_c, lab_c, szg_c, gate_c) = pc
        (ua_l, wa_l, q_l, k_l, vt_l, sza_l, gq_l, gk_l, gv_l, laf_l, lab_l, szg_l, gate_l) = pl_

        att_l = _attention(q_l, k_l, vt_l, k_c, vt_c, batch, seq, seq, ctx_len)
        of_c, ob_c, s_f, s_b = _gla_scan(gq_c, gk_c, gv_c, laf_c, lab_c, zero_state, zero_state,
                                         batch, ctx_len)
        of_l, ob_l, _, _ = _gla_scan(gq_l, gk_l, gv_l, laf_l, lab_l, s_f, s_b, batch, seq)

        shared_out = (conv_w[l], gla_norm_g[l][None], g_post[l][None],
                      w_br_conv[l].astype(BF16), w_br_attn[l].astype(BF16),
                      w_br_gla[l].astype(BF16), w_out[l].astype(BF16))
        if not last:
            att_c = _attention(q_c, None, None, k_c, vt_c, batch, ctx_len, 0, ctx_len)
            xc = _merge(ua_c, wa_c, att_c, sza_c, of_c, ob_c, szg_c, gate_c, xc, mod3, batch, ctx_len,
                        *shared_out)
        xl = _merge(ua_l, wa_l, att_l, sza_l, of_l, ob_l, szg_l, gate_l, xl, mod3, None, seq,
                    *shared_out)
    return xl.reshape(batch, seq, D_MODEL)
```

```python
import functools

import jax
import jax.numpy as jnp
import numpy as np
from jax import lax
from jax.experimental import pallas as pl
from jax.experimental.pallas import tpu as pltpu

F32 = jnp.float32
BF16 = jnp.bfloat16

D_MODEL = 1024
DEPTH = 2
GRID_W = 64
CONV_W = 1024
N_HEADS = 8
N_KV_HEADS = 2
HEAD_DIM = 128
GROUP = N_HEADS // N_KV_HEADS
ROPE_THETA = 10000.0
ROPE_AXIS_DIM = HEAD_DIM // 2
ROPE_PAIRS = ROPE_AXIS_DIM // 2
ATTN_SCALE = HEAD_DIM ** -0.5
LOG2_E = 1.4426950408889634
GLA_HEADS = 4
GLA_DK = D_MODEL // 2
GLA_DV = D_MODEL
GLA_DKH = GLA_DK // GLA_HEADS
GLA_DVH = GLA_DV // GLA_HEADS
GLA_RANK = 16
GLA_TAU = 16.0
N_BRANCH = 3
EPS = 1e-6

Q_W = N_HEADS * HEAD_DIM
KV_W = N_KV_HEADS * HEAD_DIM

LANES = 128
SUBLANES = 8
BF16_ROWS = 16
VT_ROWS = HEAD_DIM + BF16_ROWS
VMEM_LIMIT = 56 * 1024 * 1024

OFF_A_B = 0
OFF_A_C = OFF_A_B + CONV_W
OFF_A_X = OFF_A_C + CONV_W
OFF_A_Z = OFF_A_X + CONV_W
OFF_Q = OFF_A_Z + CONV_W
OFF_K = OFF_Q + Q_W
OFF_V = OFF_K + KV_W
OFF_Z_ATT = OFF_V + KV_W
OFF_GQ = OFF_Z_ATT + Q_W
OFF_GK = OFF_GQ + GLA_DK
OFF_GV = OFF_GK + GLA_DK
OFF_Z_GLA = OFF_GV + GLA_DV
OFF_MG = OFF_Z_GLA + GLA_DV
OFF_R = OFF_MG + N_BRANCH * D_MODEL
ORIG_R = OFF_Z_GLA

PROJ_TM = 256
COL_BLK = 256
ATT_TQ = 256
ATT_KV = 512
GLA_C = 64
GLA_FINE = 4
GLA_ROWS = 512
GLA_LEVELS = tuple(GLA_C >> (i + 1) for i in range(int(np.log2(GLA_C // GLA_FINE))))
assert len(GLA_LEVELS) * GLA_C % LANES == 0 and 2 * GLA_C == LANES and GLA_FINE <= SUBLANES


def _sigmoid(x):
    return jax.nn.sigmoid(x)


def _silu(x):
    return x * _sigmoid(x)


def _dot(a, b):
    return jnp.dot(a, b, preferred_element_type=F32)


def _dot_nt(a, b):
    return lax.dot_general(a, b, (((1,), (1,)), ((), ())), preferred_element_type=F32)


def _dot_tn(a, b):
    return lax.dot_general(a, b, (((0,), (0,)), ((), ())), preferred_element_type=F32)


def _params(*sem):
    return pltpu.CompilerParams(dimension_semantics=sem, vmem_limit_bytes=VMEM_LIMIT)


def _mod_kernel(c_ref, w_ref, b_ref, o_ref):
    s = _silu(c_ref[...])
    o_ref[...] = _dot(s.astype(BF16), w_ref[...].astype(BF16)) + b_ref[...]


def _modulation(cvec, w_ada, b_ada):
    rows = cvec.shape[0]
    n_col = 3 * D_MODEL // D_MODEL
    return pl.pallas_call(
        _mod_kernel,
        out_shape=jax.ShapeDtypeStruct((DEPTH, rows, 3 * D_MODEL), F32),
        grid=(DEPTH, n_col),
        in_specs=[
            pl.BlockSpec((rows, D_MODEL), lambda l, j: (0, 0)),
            pl.BlockSpec((None, D_MODEL, D_MODEL), lambda l, j: (l, 0, j)),
            pl.BlockSpec((None, 1, D_MODEL), lambda l, j: (l, 0, j)),
        ],
        out_specs=pl.BlockSpec((None, rows, D_MODEL), lambda l, j: (l, 0, j)),
        compiler_params=_params("parallel", "parallel"),
        name="adaln_mod",
    )(cvec, w_ada, b_ada.reshape(DEPTH, 1, 3 * D_MODEL))


def _head_norm(xh, g):
    ms = jnp.mean(xh * xh, axis=-1, keepdims=True)
    return xh * lax.rsqrt(ms + EPS) * g


def _rope(xh, cos, sin):
    lane = lax.broadcasted_iota(jnp.int32, xh.shape, 1)
    first_half = (lane % ROPE_AXIS_DIM) < ROPE_PAIRS
    partner = jnp.where(first_half,
                        pltpu.roll(xh, HEAD_DIM - ROPE_PAIRS, 1),
                        pltpu.roll(xh, ROPE_PAIRS, 1))
    return xh * cos + partner * sin


def _log_sigmoid(x):
    return jnp.minimum(x, 0.0) - jnp.log1p(jnp.exp(-jnp.abs(x)))


def _proj_kernel(x_ref, mod_ref, gpre_ref, w_lo_ref, w_hi_ref, w_rank_ref, wdec_ref, bdec_ref, qg_ref, kg_ref,
                 cos_ref, sin_ref, bgate_ref,
                 ua_ref, wa_ref, q_ref, k_ref, vt_ref, sza_ref, gq_ref, gk_ref, gv_ref,
                 laf_ref, lab_ref, szg_ref, gate_ref, h_ref, v_ref, *, use_rope):
    x = x_ref[...]
    ms = jnp.mean(x * x, axis=-1, keepdims=True)
    y = x * lax.rsqrt(ms + EPS) * gpre_ref[...]
    mod = mod_ref[...]
    shift = mod[:, :D_MODEL]
    scale = mod[:, D_MODEL:2 * D_MODEL]
    h_ref[...] = (y * (1.0 + scale) + shift).astype(BF16)

    def proj(off, width=COL_BLK):
        for ref, base in ((w_rank_ref, OFF_R), (w_hi_ref, OFF_Z_GLA), (w_lo_ref, 0)):
            if off >= base:
                return _dot(h_ref[...], ref[:, off - base:off - base + width])

    for o in range(0, CONV_W, COL_BLK):
        cols = slice(o, o + COL_BLK)
        ua_ref[:, cols] = (proj(OFF_A_C + o) * proj(OFF_A_X + o)).astype(BF16)
        wa_ref[:, cols] = (proj(OFF_A_B + o) * _silu(proj(OFF_A_Z + o))).astype(BF16)

    def heads(off, width, gain, out_ref, out_scale):
        for o in range(0, width, COL_BLK):
            blk = proj(off + o)
            for hh in range(COL_BLK // HEAD_DIM):
                xh = _head_norm(blk[:, hh * HEAD_DIM:(hh + 1) * HEAD_DIM], gain)
                if use_rope:
                    xh = _rope(xh, cos_ref[...], sin_ref[...])
                c0 = o + hh * HEAD_DIM
                out_ref[:, c0:c0 + HEAD_DIM] = (xh * out_scale).astype(BF16)

    heads(OFF_Q, Q_W, qg_ref[...], q_ref, ATTN_SCALE * LOG2_E)
    heads(OFF_K, KV_W, kg_ref[...], k_ref, 1.0)
    v_ref[...] = proj(OFF_V, KV_W)
    ones_tile = jnp.where(lax.broadcasted_iota(jnp.int32, (BF16_ROWS, x.shape[0]), 0) == 0, 1.0, 0.0)
    for hh in range(N_KV_HEADS):
        vt_ref[hh * VT_ROWS:hh * VT_ROWS + HEAD_DIM, :] = (
            v_ref[:, hh * HEAD_DIM:(hh + 1) * HEAD_DIM].T.astype(BF16))
        vt_ref[hh * VT_ROWS + HEAD_DIM:(hh + 1) * VT_ROWS, :] = ones_tile.astype(BF16)
    for o in range(0, Q_W, COL_BLK):
        sza_ref[:, o:o + COL_BLK] = _silu(proj(OFF_Z_ATT + o)).astype(BF16)

    for o in range(0, GLA_DK, COL_BLK):
        gq_ref[:, o:o + COL_BLK] = (proj(OFF_GQ + o) * (GLA_DKH ** -0.5)).astype(BF16)
        gk_ref[:, o:o + COL_BLK] = proj(OFF_GK + o).astype(BF16)
    for o in range(0, GLA_DV, COL_BLK):
        gv_ref[:, o:o + COL_BLK] = proj(OFF_GV + o).astype(BF16)
        szg_ref[:, o:o + COL_BLK] = _silu(proj(OFF_Z_GLA + o)).astype(BF16)
    r = proj(OFF_R, LANES).astype(BF16)
    for o in range(0, GLA_DK, COL_BLK):
        laf_ref[:, o:o + COL_BLK] = _log_sigmoid(
            _dot(r, wdec_ref[:, o:o + COL_BLK]) + bdec_ref[:, o:o + COL_BLK]) * (LOG2_E / GLA_TAU)
        ob = GLA_DK + o
        lab_ref[:, o:o + COL_BLK] = _log_sigmoid(
            _dot(r, wdec_ref[:, ob:ob + COL_BLK]) + bdec_ref[:, ob:ob + COL_BLK]) * (LOG2_E / GLA_TAU)

    for o in range(0, N_BRANCH * D_MODEL, COL_BLK):
        gate_ref[:, o:o + COL_BLK] = _sigmoid(
            proj(OFF_MG + o) + bgate_ref[:, o:o + COL_BLK]).astype(BF16)


def _projection(x2, mod3, mod_row0, seq_len, gpre, w_parts, wdec, bdec, qg, kg, cos_t, sin_t, bgate,
                use_rope):
    n = x2.shape[0]
    tm = PROJ_TM
    tps = seq_len // tm
    const = lambda i: (0, 0)
    row = lambda i: (i, 0)
    pos = lambda i: (i % tps, 0)
    if mod_row0 is None:
        mod_map = lambda i: (i // tps, 0, 0)
    else:
        mod_map = lambda i: (mod_row0, 0, 0)
    VT_OUT = 4
    widths = [(CONV_W, BF16), (CONV_W, BF16), (Q_W, BF16), (KV_W, BF16), (N_KV_HEADS * VT_ROWS, BF16), (Q_W, BF16),
              (GLA_DK, BF16), (GLA_DK, BF16), (GLA_DV, BF16), (GLA_DK, F32), (GLA_DK, F32),
              (GLA_DV, BF16), (N_BRANCH * D_MODEL, BF16)]
    return pl.pallas_call(
        functools.partial(_proj_kernel, use_rope=use_rope),
        out_shape=[jax.ShapeDtypeStruct((w, n) if i == VT_OUT else (n, w), dt)
                   for i, (w, dt) in enumerate(widths)],
        grid=(n // tm,),
        in_specs=[
            pl.BlockSpec((tm, D_MODEL), row),
            pl.BlockSpec((None, 1, 3 * D_MODEL), mod_map),
            pl.BlockSpec((1, D_MODEL), const),
            pl.BlockSpec((D_MODEL, OFF_Z_GLA), const, pipeline_mode=pl.Buffered(1)),
            pl.BlockSpec((D_MODEL, OFF_R - OFF_Z_GLA), const, pipeline_mode=pl.Buffered(1)),
            pl.BlockSpec((D_MODEL, LANES), const, pipeline_mode=pl.Buffered(1)),
            pl.BlockSpec((LANES, 2 * GLA_DK), const),
            pl.BlockSpec((1, 2 * GLA_DK), const),
            pl.BlockSpec((1, HEAD_DIM), const),
            pl.BlockSpec((1, HEAD_DIM), const),
            pl.BlockSpec((tm, HEAD_DIM), pos),
            pl.BlockSpec((tm, HEAD_DIM), pos),
            pl.BlockSpec((1, N_BRANCH * D_MODEL), const),
        ],
        out_specs=[pl.BlockSpec((w, tm), lambda i: (0, i)) if j == VT_OUT else pl.BlockSpec((tm, w), row)
                   for j, (w, _) in enumerate(widths)],
        scratch_shapes=[pltpu.VMEM((tm, D_MODEL), BF16), pltpu.VMEM((tm, KV_W), F32)],
        compiler_params=_params("parallel"),
        name="in_proj_rope" if use_rope else "in_proj",
    )(x2, mod3, gpre, *w_parts, wdec, bdec, qg, kg, cos_t, sin_t, bgate)


def _attn_kernel(*refs, n_lat_blocks):
    if n_lat_blocks:
        (q_ref, kl_ref, vtl_ref, kc_ref, vtc_ref, o_ref, m_ref, alpha_ref, acc_ref,
         sc_ref, pc_ref, sa_ref, sb_ref, pa_ref, pb_ref) = refs
    else:
        q_ref, kc_ref, vtc_ref, o_ref, m_ref, alpha_ref, acc_ref, sc_ref, pc_ref = refs
    tq = q_ref.shape[0]
    m_ref[...] = jnp.full(m_ref.shape, -jnp.inf, F32)
    acc_ref[...] = jnp.zeros(acc_ref.shape, F32)

    def step(scores=None, softmax=None, values=None):
        heads = [slice(g * tq, (g + 1) * tq) for g in range(GROUP)]
        if scores is not None:
            k, s_dst = scores
            for g, cols in enumerate(heads):
                s_dst[:, cols] = _dot_nt(k, q_ref[:, g * HEAD_DIM:(g + 1) * HEAD_DIM])
        if values is not None:
            p_src, vt = values
            for cols in heads:
                acc_ref[:, cols] = alpha_ref[:, cols] * acc_ref[:, cols] + _dot(vt, p_src[:, cols])
        if softmax is not None:
            s_src, p_dst = softmax
            for cols in heads:
                m_old = m_ref[:, cols]
                m_new = jnp.maximum(m_old, jnp.max(s_src[:, cols], axis=0, keepdims=True))
                alpha = jnp.exp2(m_old - m_new)
                p_dst[:, cols] = jnp.exp2(s_src[:, cols] - m_new).astype(BF16)
                alpha_ref[:, cols] = alpha
                m_ref[:, cols] = m_new

    def keys(j):
        return pl.ds(pl.multiple_of(j * ATT_KV, ATT_KV), ATT_KV)

    if n_lat_blocks:
        def pair(i, carry):
            t = 2 * i
            step(scores=(kl_ref[keys(t), :], sa_ref), values=(pa_ref, vtl_ref[:, keys(t - 2)]),
                 softmax=(sb_ref, pb_ref))
            step(scores=(kl_ref[keys(t + 1), :], sb_ref), values=(pb_ref, vtl_ref[:, keys(t - 1)]),
                 softmax=(sa_ref, pa_ref))
            return carry

        n = n_lat_blocks
        step(scores=(kl_ref[keys(0), :], sa_ref))
        step(scores=(kl_ref[keys(1), :], sb_ref), softmax=(sa_ref, pa_ref))
        lax.fori_loop(1, n // 2, pair, 0)
        step(scores=(kc_ref[...], sc_ref), values=(pa_ref, vtl_ref[:, keys(n - 2)]),
             softmax=(sb_ref, pb_ref))
        step(values=(pb_ref, vtl_ref[:, keys(n - 1)]), softmax=(sc_ref, pc_ref))
    else:
        step(scores=(kc_ref[...], sc_ref))
        step(softmax=(sc_ref, pc_ref))
    step(values=(pc_ref, vtc_ref[...]))
    out_t = acc_ref[:HEAD_DIM, :] / acc_ref[HEAD_DIM:HEAD_DIM + 1, :]
    for g in range(GROUP):
        o_ref[:, g * HEAD_DIM:(g + 1) * HEAD_DIM] = out_t[:, g * tq:(g + 1) * tq].T.astype(BF16)


def _attention(q2, k_lat, vt_lat, k_ctx, vt_ctx, batch, q_len, lat_len, ctx_len):
    tq = ATT_TQ
    tiles = q_len // tq
    qmap = lambda b, kv, i: (b * tiles + i, kv)
    kmap = lambda b, kv, i: (b, kv)
    vtmap = lambda b, kv, i: (kv, b)
    in_specs = [pl.BlockSpec((tq, GROUP * HEAD_DIM), qmap)]
    args = [q2]
    n_lat_blocks = 0
    if k_lat is not None:
        n_lat_blocks = lat_len // ATT_KV
        in_specs += [pl.BlockSpec((lat_len, HEAD_DIM), kmap), pl.BlockSpec((VT_ROWS, lat_len), vtmap)]
        args += [k_lat, vt_lat]
    in_specs += [pl.BlockSpec((ctx_len, HEAD_DIM), kmap), pl.BlockSpec((VT_ROWS, ctx_len), vtmap)]
    args += [k_ctx, vt_ctx]
    rows = GROUP * tq
    stat = pltpu.VMEM((1, rows), F32)
    scratch = [stat, stat, pltpu.VMEM((VT_ROWS, rows), F32),
               pltpu.VMEM((ctx_len, rows), F32), pltpu.VMEM((ctx_len, rows), BF16)]
    if n_lat_blocks:
        assert n_lat_blocks % 2 == 0
        scratch += [pltpu.VMEM((ATT_KV, rows), F32)] * 2 + [pltpu.VMEM((ATT_KV, rows), BF16)] * 2
    return pl.pallas_call(
        functools.partial(_attn_kernel, n_lat_blocks=n_lat_blocks),
        out_shape=jax.ShapeDtypeStruct(q2.shape, BF16),
        grid=(batch, N_KV_HEADS, tiles),
        in_specs=in_specs,
        out_specs=pl.BlockSpec((tq, GROUP * HEAD_DIM), qmap),
        scratch_shapes=scratch,
        compiler_params=_params("parallel", "parallel", "parallel"),
        name="gqa_lat" if n_lat_blocks else "gqa_ctx",
    )(*args)


def _split3(x):
    hi = x.astype(BF16)
    r1 = x - hi.astype(F32)
    mid = r1.astype(BF16)
    lo = (r1 - mid.astype(F32)).astype(BF16)
    return hi, mid, lo


def _gla_consts(rev):
    c = GLA_C
    tri_r = lax.broadcasted_iota(jnp.int32, (c, c), 0)
    tri_c = lax.broadcasted_iota(jnp.int32, (c, c), 1)
    tri = jnp.where((tri_c >= tri_r) if rev else (tri_c <= tri_r), 1.0, 0.0).astype(BF16)
    row = lax.broadcasted_iota(jnp.int32, (c, LANES), 0)
    lane = lax.broadcasted_iota(jnp.int32, (c, LANES), 1)
    level_masks = []
    for i, m in enumerate(GLA_LEVELS):
        key = lane - c * (i % 2)
        in_half = (key >= 0) & (key < c)
        same = (row // (2 * m)) == (key // (2 * m))
        q_upper = (row % (2 * m)) >= m
        k_upper = (key % (2 * m)) >= m
        pair = (~q_upper & k_upper) if rev else (q_upper & ~k_upper)
        level_masks.append(in_half & same & pair)
    sub = row % GLA_FINE
    d_of_lane = lane if rev else (LANES - lane) % LANES
    ok = (sub + d_of_lane < GLA_FINE) if rev else (sub >= d_of_lane)
    lane_code = jnp.where((d_of_lane < GLA_FINE) & ok, d_of_lane, -1)
    return tri, level_masks, lane_code


def _gla_kernel(qf_ref, kf_ref, vf_ref, laf_ref, qb_ref, kb_ref, vb_ref, lab_ref, s0f_ref, s0b_ref,
                of_ref, ob_ref, sff_ref, sfb_ref, st_ref):
    @pl.when(pl.program_id(1) == 0)
    def _():
        st_ref[0] = s0f_ref[...]
        st_ref[1] = s0b_ref[...]

    c = GLA_C
    n_chunks = qf_ref.shape[0] // c
    dirs = ((0, False, qf_ref, kf_ref, vf_ref, laf_ref, of_ref, _gla_consts(False)),
            (1, True, qb_ref, kb_ref, vb_ref, lab_ref, ob_ref, _gla_consts(True)))

    def body(ci, carry):
        chains = []
        for di, rev, q_ref, k_ref, v_ref, la_ref, o_ref, consts in dirs:
            cc = (n_chunks - 1 - ci) if rev else ci
            rows = pl.ds(pl.multiple_of(cc * c, c), c)
            cum3 = _dot(consts[0], jnp.concatenate(_split3(la_ref[rows, :]), axis=1))
            b_all = cum3[:, :GLA_DK] + cum3[:, GLA_DK:2 * GLA_DK] + cum3[:, 2 * GLA_DK:]
            for hh in range(GLA_HEADS):
                kc = slice(hh * GLA_DKH, (hh + 1) * GLA_DKH)
                vc = slice(hh * GLA_DVH, (hh + 1) * GLA_DVH)
                chains.append(dict(di=di, hh=hh, rev=rev, rows=rows, vc=vc, o_ref=o_ref,
                                   masks=consts[1], lane_code=consts[2], b=b_all[:, kc],
                                   qf=q_ref[rows, kc].astype(F32), kf=k_ref[rows, kc].astype(F32),
                                   v=v_ref[rows, vc]))

        for w in chains:
            b, qf, kf, rev = w["b"], w["qf"], w["kf"], w["rev"]
            st = st_ref[w["di"], w["hh"]]
            tot = b[0:1, :] if rev else b[c - 1:c, :]
            w["o"] = _dot_nt((qf * jnp.exp2(b)).astype(BF16), st.astype(BF16))
            kd = (kf * jnp.exp2(tot - b)).astype(BF16)
            st_ref[w["di"], w["hh"]] = st * jnp.exp2(tot) + _dot_tn(w["v"], kd)
            qs, ks = [], []
            for m in GLA_LEVELS:
                pivot = (m - 1) if rev else m
                bm = b.reshape(c // (2 * m), 2 * m, GLA_DKH)
                f = jnp.exp2(-jnp.abs(bm - bm[:, pivot:pivot + 1, :])).reshape(c, GLA_DKH)
                qs.append((qf * f).astype(BF16))
                ks.append((kf * f).astype(BF16))
            w["coarse"] = _dot_nt(jnp.concatenate(qs, axis=0), jnp.concatenate(ks, axis=0))

        grouped = (c // SUBLANES, SUBLANES, GLA_DKH)
        for w in chains:
            qf, kf, rev, lane_code = w["qf"], w["kf"], w["rev"], w["lane_code"]
            w_slots = jnp.where(lane_code == 0, jnp.sum(qf * kf, axis=-1, keepdims=True), 0.0)
            q3, k3, b3 = qf.reshape(grouped), kf.reshape(grouped), w["b"].reshape(grouped)
            for d in range(1, GLA_FINE):
                shift = (SUBLANES - d) if rev else d
                kr = pltpu.roll(k3, shift, 1)
                br = pltpu.roll(b3, shift, 1)
                wd = jnp.sum(q3 * kr * jnp.exp2(jnp.minimum(b3 - br, 0.0)), axis=-1, keepdims=True)
                w_slots = jnp.where(lane_code == d, wd.reshape(c, 1), w_slots)
            w["att"] = pltpu.roll(w_slots, 0, 1, stride=1, stride_axis=0)

        for w in chains:
            att = w["att"]
            for i, mask in enumerate(w["masks"]):
                col0 = (i * c) // LANES * LANES
                att = jnp.where(mask, w["coarse"][i * c:(i + 1) * c, col0:col0 + LANES], att)
            o = w["o"] + _dot(att.astype(BF16), jnp.concatenate([w["v"], w["v"]], axis=0))
            w["o_ref"][w["rows"], w["vc"]] = o.astype(BF16)
        return carry

    lax.fori_loop(0, n_chunks, body, 0)
    sff_ref[...] = st_ref[0]
    sfb_ref[...] = st_ref[1]


def _gla_scan(gq, gk, gv, la_f, la_b, s0_f, s0_b, batch, seq_len):
    rows = min(GLA_ROWS, seq_len)
    nblk = seq_len // rows
    fwd = lambda b, i: (b * nblk + i, 0)
    bwd = lambda b, i: (b * nblk + (nblk - 1 - i), 0)
    smap = lambda b, i: (b, 0, 0, 0)
    state_spec = pl.BlockSpec((None, GLA_HEADS, GLA_DVH, GLA_DKH), smap)
    state_shape = jax.ShapeDtypeStruct((batch, GLA_HEADS, GLA_DVH, GLA_DKH), F32)

    def views(index_map):
        return [pl.BlockSpec((rows, GLA_DK), index_map), pl.BlockSpec((rows, GLA_DK), index_map),
                pl.BlockSpec((rows, GLA_DV), index_map), pl.BlockSpec((rows, GLA_DK), index_map)]

    return pl.pallas_call(
        _gla_kernel,
        out_shape=[jax.ShapeDtypeStruct(gv.shape, BF16), jax.ShapeDtypeStruct(gv.shape, BF16),
                   state_shape, state_shape],
        grid=(batch, nblk),
        in_specs=views(fwd) + views(bwd) + [state_spec, state_spec],
        out_specs=[pl.BlockSpec((rows, GLA_DV), fwd), pl.BlockSpec((rows, GLA_DV), bwd),
                   state_spec, state_spec],
        scratch_shapes=[pltpu.VMEM((2, GLA_HEADS, GLA_DVH, GLA_DKH), F32)],
        compiler_params=_params("parallel", "arbitrary"),
        name="gla_bidir",
    )(gq, gk, gv, la_f, gq, gk, gv, la_b, s0_f, s0_b)


def _merge_kernel(ua_ref, uprev_ref, unext_ref, wa_ref, att_ref, sza_ref, of_ref, ob_ref, szg_ref,
                  gate_ref, x_ref, mod_ref, convw_ref, glag_ref, gpost_ref,
                  wconv_ref, watt_ref, wgla_ref, wout_ref, o_ref, *, tiles_per_seq):
    tm = x_ref.shape[0]
    ti = pl.program_id(0) % tiles_per_seq
    u = ua_ref[...].astype(F32)
    prev_row = jnp.where(ti == 0, 0.0, uprev_ref[BF16_ROWS - 1:BF16_ROWS, :].astype(F32))
    next_row = jnp.where(ti == tiles_per_seq - 1, 0.0, unext_ref[0:1, :].astype(F32))
    rid = lax.broadcasted_iota(jnp.int32, (tm, 1), 0)
    u_prev = jnp.where(rid == 0, prev_row, pltpu.roll(u, 1, 0))
    u_next = jnp.where(rid == tm - 1, next_row, pltpu.roll(u, tm - 1, 0))
    cw = convw_ref[...]
    conv = cw[0:1, :] * u_prev + cw[1:2, :] * u + cw[2:3, :] * u_next
    br_a = _dot((wa_ref[...].astype(F32) * conv).astype(BF16), wconv_ref[...])
    br_b = _dot(att_ref[...] * sza_ref[...], watt_ref[...])
    parts = []
    for hh in range(GLA_HEADS):
        cols = slice(hh * GLA_DVH, (hh + 1) * GLA_DVH)
        oh = of_ref[:, cols].astype(F32) + ob_ref[:, cols].astype(F32)
        parts.append((_head_norm(oh, glag_ref[...]) * szg_ref[:, cols].astype(F32)).astype(BF16))
    br_c = _dot(jnp.concatenate(parts, axis=1), wgla_ref[...])
    merged = (gate_ref[:, :D_MODEL].astype(F32) * br_a
              + gate_ref[:, D_MODEL:2 * D_MODEL].astype(F32) * br_b
              + gate_ref[:, 2 * D_MODEL:].astype(F32) * br_c)
    out = _dot(merged.astype(BF16), wout_ref[...])
    gate = mod_ref[:, 2 * D_MODEL:]
    o_ref[...] = x_ref[...] + gate * _head_norm(out, gpost_ref[...])


def _merge(ua, wa, att, sza, o_f, o_b, szg, gates, x2, mod3, mod_row0, seq_len,
           convw, glag, gpost, wconv, watt, wgla, wout):
    n = x2.shape[0]
    tm = PROJ_TM
    tps = seq_len // tm
    halo = tm // BF16_ROWS
    n_halo = n // BF16_ROWS
    const = lambda i: (0, 0)
    row = lambda i: (i, 0)
    if mod_row0 is None:
        mod_map = lambda i: (i // tps, 0, 0)
    else:
        mod_map = lambda i: (mod_row0, 0, 0)
    tok = lambda w: pl.BlockSpec((tm, w), row)
    wspec = pl.BlockSpec((D_MODEL, D_MODEL), const)
    return pl.pallas_call(
        functools.partial(_merge_kernel, tiles_per_seq=tps),
        out_shape=jax.ShapeDtypeStruct((n, D_MODEL), F32),
        grid=(n // tm,),
        in_specs=[
            tok(CONV_W),
            pl.BlockSpec((BF16_ROWS, CONV_W), lambda i: (jnp.maximum(i * halo - 1, 0), 0)),
            pl.BlockSpec((BF16_ROWS, CONV_W), lambda i: (jnp.minimum((i + 1) * halo, n_halo - 1), 0)),
            tok(CONV_W), tok(Q_W), tok(Q_W), tok(GLA_DV), tok(GLA_DV), tok(GLA_DV),
            tok(N_BRANCH * D_MODEL), tok(D_MODEL),
            pl.BlockSpec((None, 1, 3 * D_MODEL), mod_map),
            pl.BlockSpec((3, CONV_W), const),
            pl.BlockSpec((1, GLA_DVH), const),
            pl.BlockSpec((1, D_MODEL), const),
            wspec, wspec, wspec, wspec,
        ],
        out_specs=tok(D_MODEL),
        compiler_params=_params("parallel"),
        name="merge_out",
    )(ua, ua, ua, wa, att, sza, o_f, o_b, szg, gates, x2, mod3, convw, glag, gpost,
      wconv, watt, wgla, wout)


def _rope_tables(n_tokens):
    n_rows = n_tokens // GRID_W
    row = np.repeat(np.arange(n_rows, dtype=np.float32), GRID_W)
    col = np.tile(np.arange(GRID_W, dtype=np.float32), n_rows)
    freqs = (np.float32(ROPE_THETA) ** (-np.arange(ROPE_PAIRS, dtype=np.float32) * np.float32(2.0)
                                        / np.float32(ROPE_AXIS_DIM))).astype(np.float32)
    ar, ac = row[:, None] * freqs, col[:, None] * freqs
    cos_t = np.concatenate([np.cos(ar), np.cos(ar), np.cos(ac), np.cos(ac)], axis=1)
    sin_t = np.concatenate([-np.sin(ar), np.sin(ar), -np.sin(ac), np.sin(ac)], axis=1)
    return jnp.asarray(cos_t, F32), jnp.asarray(sin_t, F32)


def _split_w_in(w):
    tail = ORIG_R + 2 * GLA_RANK
    low_rank = jnp.pad(w[:, ORIG_R:tail], ((0, 0), (0, LANES - 2 * GLA_RANK)))
    return w[:, :ORIG_R].astype(BF16), w[:, tail:].astype(BF16), low_rank.astype(BF16)


def _pack_decay(w_f, b_f, w_b, b_b):
    wd = jnp.zeros((LANES, 2 * GLA_DK), F32)
    wd = wd.at[:GLA_RANK, :GLA_DK].set(w_f).at[GLA_RANK:2 * GLA_RANK, GLA_DK:].set(w_b)
    return wd.astype(BF16), jnp.concatenate([b_f, b_b])[None, :]


def kernel(x, c, ctx, c_ctx, w_ada, b_ada, g_pre, g_post, w_in, conv_w, q_norm_g, k_norm_g,
           w_decay_fwd, b_decay_fwd, w_decay_bwd, b_decay_bwd, gla_norm_g,
           w_br_conv, w_br_attn, w_br_gla, b_gate, w_out):
    batch, seq, _ = x.shape
    ctx_len = ctx.shape[1]
    assert seq % max(PROJ_TM, ATT_TQ, ATT_KV, GLA_ROWS) == 0 and seq % GRID_W == 0
    assert ctx_len % max(PROJ_TM, ATT_TQ) == 0 and ctx_len % GLA_C == 0

    mod_rows = -(-(batch + 1) // SUBLANES) * SUBLANES
    cvec = jnp.zeros((mod_rows, D_MODEL), F32).at[:batch].set(c).at[batch].set(c_ctx)
    mod = _modulation(cvec, w_ada, b_ada)
    cos_t, sin_t = _rope_tables(seq)
    zero_state = jnp.zeros((batch, GLA_HEADS, GLA_DVH, GLA_DKH), F32)

    xl = x.reshape(batch * seq, D_MODEL)
    xc = ctx.reshape(batch * ctx_len, D_MODEL)
    for l in range(DEPTH):
        last = l == DEPTH - 1
        mod3 = mod[l][:, None, :]
        w_parts = _split_w_in(w_in[l])
        wdec, bdec = _pack_decay(w_decay_fwd[l], b_decay_fwd[l], w_decay_bwd[l], b_decay_bwd[l])
        shared_in = (g_pre[l][None], w_parts, wdec, bdec, q_norm_g[l][None], k_norm_g[l][None])
        bgate = b_gate[l][None]
        pc = _projection(xc, mod3, batch, ctx_len, *shared_in, cos_t, sin_t, bgate, use_rope=False)
        pl_ = _projection(xl, mod3, None, seq, *shared_in, cos_t, sin_t, bgate, use_rope=True)
        (ua_c, wa_c, q_c, k_c, vt_c, sza_c, gq_c, gk_c, gv_c, laf_c, lab_c, szg_c, gate_c) = pc
        (ua_l, wa_l, q_l, k_l, vt_l, sza_l, gq_l, gk_l, gv_l, laf_l, lab_l, szg_l, gate_l) = pl_

        att_l = _attention(q_l, k_l, vt_l, k_c, vt_c, batch, seq, seq, ctx_len)
        of_c, ob_c, s_f, s_b = _gla_scan(gq_c, gk_c, gv_c, laf_c, lab_c, zero_state, zero_state,
                                         batch, ctx_len)
        of_l, ob_l, _, _ = _gla_scan(gq_l, gk_l, gv_l, laf_l, lab_l, s_f, s_b, batch, seq)

        shared_out = (conv_w[l], gla_norm_g[l][None], g_post[l][None],
                      w_br_conv[l].astype(BF16), w_br_attn[l].astype(BF16),
                      w_br_gla[l].astype(BF16), w_out[l].astype(BF16))
        if not last:
            att_c = _attention(q_c, None, None, k_c, vt_c, batch, ctx_len, 0, ctx_len)
            xc = _merge(ua_c, wa_c, att_c, sza_c, of_c, ob_c, szg_c, gate_c, xc, mod3, batch, ctx_len,
                        *shared_out)
        xl = _merge(ua_l, wa_l, att_l, sza_l, of_l, ob_l, szg_l, gate_l, xl, mod3, None, seq,
                    *shared_out)
    return xl.reshape(batch, seq, D_MODEL)
```

```python
import functools

import jax
import jax.numpy as jnp
import numpy as np
from jax import lax
from jax.experimental import pallas as pl
from jax.experimental.pallas import tpu as pltpu

F32 = jnp.float32
BF16 = jnp.bfloat16

D_MODEL = 1024
DEPTH = 2
GRID_W = 64
CONV_W = 1024
N_HEADS = 8
N_KV_HEADS = 2
HEAD_DIM = 128
GROUP = N_HEADS // N_KV_HEADS
ROPE_THETA = 10000.0
ROPE_AXIS_DIM = HEAD_DIM // 2
ROPE_PAIRS = ROPE_AXIS_DIM // 2
ATTN_SCALE = HEAD_DIM ** -0.5
LOG2_E = 1.4426950408889634
GLA_HEADS = 4
GLA_DK = D_MODEL // 2
GLA_DV = D_MODEL
GLA_DKH = GLA_DK // GLA_HEADS
GLA_DVH = GLA_DV // GLA_HEADS
GLA_RANK = 16
GLA_TAU = 16.0
N_BRANCH = 3
EPS = 1e-6

Q_W = N_HEADS * HEAD_DIM
KV_W = N_KV_HEADS * HEAD_DIM

LANES = 128
SUBLANES = 8
BF16_ROWS = 16
VT_ROWS = HEAD_DIM + BF16_ROWS
VMEM_LIMIT = 56 * 1024 * 1024

OFF_A_B = 0
OFF_A_C = OFF_A_B + CONV_W
OFF_A_X = OFF_A_C + CONV_W
OFF_A_Z = OFF_A_X + CONV_W
OFF_Q = OFF_A_Z + CONV_W
OFF_K = OFF_Q + Q_W
OFF_V = OFF_K + KV_W
OFF_Z_ATT = OFF_V + KV_W
OFF_GQ = OFF_Z_ATT + Q_W
OFF_GK = OFF_GQ + GLA_DK
OFF_GV = OFF_GK + GLA_DK
OFF_Z_GLA = OFF_GV + GLA_DV
OFF_MG = OFF_Z_GLA + GLA_DV
OFF_R = OFF_MG + N_BRANCH * D_MODEL
ORIG_R = OFF_Z_GLA

PROJ_TM = 256
COL_BLK = 256
ATT_TQ = 256
ATT_KV = 512
GLA_C = 64
GLA_FINE = 4
GLA_ROWS = 512
GLA_LEVELS = tuple(GLA_C >> (i + 1) for i in range(int(np.log2(GLA_C // GLA_FINE))))
assert len(GLA_LEVELS) * GLA_C % LANES == 0 and 2 * GLA_C == LANES and GLA_FINE <= SUBLANES


def _sigmoid(x):
    return jax.nn.sigmoid(x)


def _silu(x):
    return x * _sigmoid(x)


def _dot(a, b):
    return jnp.dot(a, b, preferred_element_type=F32)


def _dot_nt(a, b):
    return lax.dot_general(a, b, (((1,), (1,)), ((), ())), preferred_element_type=F32)


def _dot_tn(a, b):
    return lax.dot_general(a, b, (((0,), (0,)), ((), ())), preferred_element_type=F32)


def _params(*sem):
    return pltpu.CompilerParams(dimension_semantics=sem, vmem_limit_bytes=VMEM_LIMIT)


def _mod_kernel(c_ref, w_ref, b_ref, o_ref):
    s = _silu(c_ref[...])
    o_ref[...] = _dot(s.astype(BF16), w_ref[...].astype(BF16)) + b_ref[...]


def _modulation(cvec, w_ada, b_ada):
    rows = cvec.shape[0]
    n_col = 3 * D_MODEL // D_MODEL
    return pl.pallas_call(
        _mod_kernel,
        out_shape=jax.ShapeDtypeStruct((DEPTH, rows, 3 * D_MODEL), F32),
        grid=(DEPTH, n_col),
        in_specs=[
            pl.BlockSpec((rows, D_MODEL), lambda l, j: (0, 0)),
            pl.BlockSpec((None, D_MODEL, D_MODEL), lambda l, j: (l, 0, j)),
            pl.BlockSpec((None, 1, D_MODEL), lambda l, j: (l, 0, j)),
        ],
        out_specs=pl.BlockSpec((None, rows, D_MODEL), lambda l, j: (l, 0, j)),
        compiler_params=_params("parallel", "parallel"),
        name="adaln_mod",
    )(cvec, w_ada, b_ada.reshape(DEPTH, 1, 3 * D_MODEL))


def _head_norm(xh, g):
    ms = jnp.mean(xh * xh, axis=-1, keepdims=True)
    return xh * lax.rsqrt(ms + EPS) * g


def _rope(xh, cos, sin):
    lane = lax.broadcasted_iota(jnp.int32, xh.shape, 1)
    first_half = (lane % ROPE_AXIS_DIM) < ROPE_PAIRS
    partner = jnp.where(first_half,
                        pltpu.roll(xh, HEAD_DIM - ROPE_PAIRS, 1),
                        pltpu.roll(xh, ROPE_PAIRS, 1))
    return xh * cos + partner * sin


def _log_sigmoid(x):
    return jnp.minimum(x, 0.0) - jnp.log1p(jnp.exp(-jnp.abs(x)))


def _proj_kernel(x_ref, mod_ref, gpre_ref, w_lo_ref, w_hi_ref, w_rank_ref, wdec_ref, bdec_ref, qg_ref, kg_ref,
                 cos_ref, sin_ref, bgate_ref,
                 ua_ref, wa_ref, q_ref, k_ref, vt_ref, sza_ref, gq_ref, gk_ref, gv_ref,
                 laf_ref, lab_ref, szg_ref, gate_ref, h_ref, v_ref, *, use_rope):
    x = x_ref[...]
    ms = jnp.mean(x * x, axis=-1, keepdims=True)
    y = x * lax.rsqrt(ms + EPS) * gpre_ref[...]
    mod = mod_ref[...]
    shift = mod[:, :D_MODEL]
    scale = mod[:, D_MODEL:2 * D_MODEL]
    h_ref[...] = (y * (1.0 + scale) + shift).astype(BF16)

    def proj(off, width=COL_BLK):
        for ref, base in ((w_rank_ref, OFF_R), (w_hi_ref, OFF_Z_GLA), (w_lo_ref, 0)):
            if off >= base:
                return _dot(h_ref[...], ref[:, off - base:off - base + width])

    for o in range(0, CONV_W, COL_BLK):
        cols = slice(o, o + COL_BLK)
        ua_ref[:, cols] = (proj(OFF_A_C + o) * proj(OFF_A_X + o)).astype(BF16)
        wa_ref[:, cols] = (proj(OFF_A_B + o) * _silu(proj(OFF_A_Z + o))).astype(BF16)

    def heads(off, width, gain, out_ref, out_scale):
        for o in range(0, width, COL_BLK):
            blk = proj(off + o)
            for hh in range(COL_BLK // HEAD_DIM):
                xh = _head_norm(blk[:, hh * HEAD_DIM:(hh + 1) * HEAD_DIM], gain)
                if use_rope:
                    xh = _rope(xh, cos_ref[...], sin_ref[...])
                c0 = o + hh * HEAD_DIM
                out_ref[:, c0:c0 + HEAD_DIM] = (xh * out_scale).astype(BF16)

    heads(OFF_Q, Q_W, qg_ref[...], q_ref, ATTN_SCALE * LOG2_E)
    heads(OFF_K, KV_W, kg_ref[...], k_ref, 1.0)
    v_ref[...] = proj(OFF_V, KV_W)
    ones_tile = jnp.where(lax.broadcasted_iota(jnp.int32, (BF16_ROWS, x.shape[0]), 0) == 0, 1.0, 0.0)
    for hh in range(N_KV_HEADS):
        vt_ref[hh * VT_ROWS:hh * VT_ROWS + HEAD_DIM, :] = (
            v_ref[:, hh * HEAD_DIM:(hh + 1) * HEAD_DIM].T.astype(BF16))
        vt_ref[hh * VT_ROWS + HEAD_DIM:(hh + 1) * VT_ROWS, :] = ones_tile.astype(BF16)
    for o in range(0, Q_W, COL_BLK):
        sza_ref[:, o:o + COL_BLK] = _silu(proj(OFF_Z_ATT + o)).astype(BF16)

    for o in range(0, GLA_DK, COL_BLK):
        gq_ref[:, o:o + COL_BLK] = (proj(OFF_GQ + o) * (GLA_DKH ** -0.5)).astype(BF16)
        gk_ref[:, o:o + COL_BLK] = proj(OFF_GK + o).astype(BF16)
    for o in range(0, GLA_DV, COL_BLK):
        gv_ref[:, o:o + COL_BLK] = proj(OFF_GV + o).astype(BF16)
        szg_ref[:, o:o + COL_BLK] = _silu(proj(OFF_Z_GLA + o)).astype(BF16)
    r = proj(OFF_R, LANES).astype(BF16)
    for o in range(0, GLA_DK, COL_BLK):
        laf_ref[:, o:o + COL_BLK] = _log_sigmoid(
            _dot(r, wdec_ref[:, o:o + COL_BLK]) + bdec_ref[:, o:o + COL_BLK]) * (LOG2_E / GLA_TAU)
        ob = GLA_DK + o
        lab_ref[:, o:o + COL_BLK] = _log_sigmoid(
            _dot(r, wdec_ref[:, ob:ob + COL_BLK]) + bdec_ref[:, ob:ob + COL_BLK]) * (LOG2_E / GLA_TAU)

    for o in range(0, N_BRANCH * D_MODEL, COL_BLK):
        gate_ref[:, o:o + COL_BLK] = _sigmoid(
            proj(OFF_MG + o) + bgate_ref[:, o:o + COL_BLK]).astype(BF16)


def _projection(x2, mod3, mod_row0, seq_len, gpre, w_parts, wdec, bdec, qg, kg, cos_t, sin_t, bgate,
                use_rope):
    n = x2.shape[0]
    tm = PROJ_TM
    tps = seq_len // tm
    const = lambda i: (0, 0)
    row = lambda i: (i, 0)
    pos = lambda i: (i % tps, 0)
    if mod_row0 is None:
        mod_map = lambda i: (i // tps, 0, 0)
    else:
        mod_map = lambda i: (mod_row0, 0, 0)
    VT_OUT = 4
    widths = [(CONV_W, BF16), (CONV_W, BF16), (Q_W, BF16), (KV_W, BF16), (N_KV_HEADS * VT_ROWS, BF16), (Q_W, BF16),
              (GLA_DK, BF16), (GLA_DK, BF16), (GLA_DV, BF16), (GLA_DK, F32), (GLA_DK, F32),
              (GLA_DV, BF16), (N_BRANCH * D_MODEL, BF16)]
    return pl.pallas_call(
        functools.partial(_proj_kernel, use_rope=use_rope),
        out_shape=[jax.ShapeDtypeStruct((w, n) if i == VT_OUT else (n, w), dt)
                   for i, (w, dt) in enumerate(widths)],
        grid=(n // tm,),
        in_specs=[
            pl.BlockSpec((tm, D_MODEL), row),
            pl.BlockSpec((None, 1, 3 * D_MODEL), mod_map),
            pl.BlockSpec((1, D_MODEL), const),
            pl.BlockSpec((D_MODEL, OFF_Z_GLA), const, pipeline_mode=pl.Buffered(1)),
            pl.BlockSpec((D_MODEL, OFF_R - OFF_Z_GLA), const, pipeline_mode=pl.Buffered(1)),
            pl.BlockSpec((D_MODEL, LANES), const, pipeline_mode=pl.Buffered(1)),
            pl.BlockSpec((LANES, 2 * GLA_DK), const),
            pl.BlockSpec((1, 2 * GLA_DK), const),
            pl.BlockSpec((1, HEAD_DIM), const),
            pl.BlockSpec((1, HEAD_DIM), const),
            pl.BlockSpec((tm, HEAD_DIM), pos),
            pl.BlockSpec((tm, HEAD_DIM), pos),
            pl.BlockSpec((1, N_BRANCH * D_MODEL), const),
        ],
        out_specs=[pl.BlockSpec((w, tm), lambda i: (0, i)) if j == VT_OUT else pl.BlockSpec((tm, w), row)
                   for j, (w, _) in enumerate(widths)],
        scratch_shapes=[pltpu.VMEM((tm, D_MODEL), BF16), pltpu.VMEM((tm, KV_W), F32)],
        compiler_params=_params("parallel"),
        name="in_proj_rope" if use_rope else "in_proj",
    )(x2, mod3, gpre, *w_parts, wdec, bdec, qg, kg, cos_t, sin_t, bgate)


def _attn_kernel(*refs, n_lat_blocks):
    if n_lat_blocks:
        (q_ref, kl_ref, vtl_ref, kc_ref, vtc_ref, o_ref, m_ref, alpha_ref, mc_ref, acc_ref,
         sc_ref, pc_ref, ma_ref, mb_ref, sa_ref, sb_ref, pa_ref, pb_ref) = refs
    else:
        q_ref, kc_ref, vtc_ref, o_ref, m_ref, alpha_ref, mc_ref, acc_ref, sc_ref, pc_ref = refs
    tq = q_ref.shape[0]
    m_ref[...] = jnp.full(m_ref.shape, -jnp.inf, F32)
    acc_ref[...] = jnp.zeros(acc_ref.shape, F32)

    def step(scores=None, softmax=None, values=None):
        heads = [slice(g * tq, (g + 1) * tq) for g in range(GROUP)]
        if scores is not None:
            k, s_dst, smax_dst = scores
            for g, cols in enumerate(heads):
                s = _dot_nt(k, q_ref[:, g * HEAD_DIM:(g + 1) * HEAD_DIM])
                s_dst[:, cols] = s
                smax_dst[:, cols] = jnp.max(s, axis=0, keepdims=True)
        if values is not None:
            p_src, vt = values
            for cols in heads:
                acc_ref[:, cols] = alpha_ref[:, cols] * acc_ref[:, cols] + _dot(vt, p_src[:, cols])
        if softmax is not None:
            s_src, smax_src, p_dst = softmax
            for cols in heads:
                m_old = m_ref[:, cols]
                m_new = jnp.maximum(m_old, smax_src[:, cols])
                alpha = jnp.exp2(m_old - m_new)
                p_dst[:, cols] = jnp.exp2(s_src[:, cols] - m_new).astype(BF16)
                alpha_ref[:, cols] = alpha
                m_ref[:, cols] = m_new

    def keys(j):
        return pl.ds(pl.multiple_of(j * ATT_KV, ATT_KV), ATT_KV)

    if n_lat_blocks:
        buf_a, buf_b, buf_c = (sa_ref, ma_ref), (sb_ref, mb_ref), (sc_ref, mc_ref)

        def pair(i, carry):
            t = 2 * i
            step(scores=(kl_ref[keys(t), :], *buf_a), values=(pa_ref, vtl_ref[:, keys(t - 2)]),
                 softmax=(*buf_b, pb_ref))
            step(scores=(kl_ref[keys(t + 1), :], *buf_b), values=(pb_ref, vtl_ref[:, keys(t - 1)]),
                 softmax=(*buf_a, pa_ref))
            return carry

        n = n_lat_blocks
        step(scores=(kl_ref[keys(0), :], *buf_a))
        step(scores=(kl_ref[keys(1), :], *buf_b), softmax=(*buf_a, pa_ref))
        lax.fori_loop(1, n // 2, pair, 0)
        step(scores=(kc_ref[...], *buf_c), values=(pa_ref, vtl_ref[:, keys(n - 2)]),
             softmax=(*buf_b, pb_ref))
        step(values=(pb_ref, vtl_ref[:, keys(n - 1)]), softmax=(*buf_c, pc_ref))
    else:
        step(scores=(kc_ref[...], sc_ref, mc_ref))
        step(softmax=(sc_ref, mc_ref, pc_ref))
    step(values=(pc_ref, vtc_ref[...]))
    out_t = acc_ref[:HEAD_DIM, :] / acc_ref[HEAD_DIM:HEAD_DIM + 1, :]
    for g in range(GROUP):
        o_ref[:, g * HEAD_DIM:(g + 1) * HEAD_DIM] = out_t[:, g * tq:(g + 1) * tq].T.astype(BF16)


def _attention(q2, k_lat, vt_lat, k_ctx, vt_ctx, batch, q_len, lat_len, ctx_len):
    tq = ATT_TQ
    tiles = q_len // tq
    qmap = lambda b, kv, i: (b * tiles + i, kv)
    kmap = lambda b, kv, i: (b, kv)
    vtmap = lambda b, kv, i: (kv, b)
    in_specs = [pl.BlockSpec((tq, GROUP * HEAD_DIM), qmap)]
    args = [q2]
    n_lat_blocks = 0
    if k_lat is not None:
        n_lat_blocks = lat_len // ATT_KV
        in_specs += [pl.BlockSpec((lat_len, HEAD_DIM), kmap), pl.BlockSpec((VT_ROWS, lat_len), vtmap)]
        args += [k_lat, vt_lat]
    in_specs += [pl.BlockSpec((ctx_len, HEAD_DIM), kmap), pl.BlockSpec((VT_ROWS, ctx_len), vtmap)]
    args += [k_ctx, vt_ctx]
    rows = GROUP * tq
    stat = pltpu.VMEM((1, rows), F32)
    scratch = [stat, stat, stat, pltpu.VMEM((VT_ROWS, rows), F32),
               pltpu.VMEM((ctx_len, rows), F32), pltpu.VMEM((ctx_len, rows), BF16)]
    if n_lat_blocks:
        assert n_lat_blocks % 2 == 0
        scratch += [stat, stat] + [pltpu.VMEM((ATT_KV, rows), F32)] * 2 + [pltpu.VMEM((ATT_KV, rows), BF16)] * 2
    return pl.pallas_call(
        functools.partial(_attn_kernel, n_lat_blocks=n_lat_blocks),
        out_shape=jax.ShapeDtypeStruct(q2.shape, BF16),
        grid=(batch, N_KV_HEADS, tiles),
        in_specs=in_specs,
        out_specs=pl.BlockSpec((tq, GROUP * HEAD_DIM), qmap),
        scratch_shapes=scratch,
        compiler_params=_params("parallel", "parallel", "parallel"),
        name="gqa_lat" if n_lat_blocks else "gqa_ctx",
    )(*args)


def _split3(x):
    hi = x.astype(BF16)
    r1 = x - hi.astype(F32)
    mid = r1.astype(BF16)
    lo = (r1 - mid.astype(F32)).astype(BF16)
    return hi, mid, lo


def _gla_consts(rev):
    c = GLA_C
    tri_r = lax.broadcasted_iota(jnp.int32, (c, c), 0)
    tri_c = lax.broadcasted_iota(jnp.int32, (c, c), 1)
    tri = jnp.where((tri_c >= tri_r) if rev else (tri_c <= tri_r), 1.0, 0.0).astype(BF16)
    row = lax.broadcasted_iota(jnp.int32, (c, LANES), 0)
    lane = lax.broadcasted_iota(jnp.int32, (c, LANES), 1)
    level_masks = []
    for i, m in enumerate(GLA_LEVELS):
        key = lane - c * (i % 2)
        in_half = (key >= 0) & (key < c)
        same = (row // (2 * m)) == (key // (2 * m))
        q_upper = (row % (2 * m)) >= m
        k_upper = (key % (2 * m)) >= m
        pair = (~q_upper & k_upper) if rev else (q_upper & ~k_upper)
        level_masks.append(in_half & same & pair)
    sub = row % GLA_FINE
    d_of_lane = lane if rev else (LANES - lane) % LANES
    ok = (sub + d_of_lane < GLA_FINE) if rev else (sub >= d_of_lane)
    lane_code = jnp.where((d_of_lane < GLA_FINE) & ok, d_of_lane, -1)
    return tri, level_masks, lane_code


def _gla_kernel(qf_ref, kf_ref, vf_ref, laf_ref, qb_ref, kb_ref, vb_ref, lab_ref, s0f_ref, s0b_ref,
                of_ref, ob_ref, sff_ref, sfb_ref, st_ref):
    @pl.when(pl.program_id(1) == 0)
    def _():
        st_ref[0] = s0f_ref[...]
        st_ref[1] = s0b_ref[...]

    c = GLA_C
    n_chunks = qf_ref.shape[0] // c
    dirs = ((0, False, qf_ref, kf_ref, vf_ref, laf_ref, of_ref, _gla_consts(False)),
            (1, True, qb_ref, kb_ref, vb_ref, lab_ref, ob_ref, _gla_consts(True)))

    def body(ci, carry):
        chains = []
        for di, rev, q_ref, k_ref, v_ref, la_ref, o_ref, consts in dirs:
            cc = (n_chunks - 1 - ci) if rev else ci
            rows = pl.ds(pl.multiple_of(cc * c, c), c)
            cum3 = _dot(consts[0], jnp.concatenate(_split3(la_ref[rows, :]), axis=1))
            b_all = cum3[:, :GLA_DK] + cum3[:, GLA_DK:2 * GLA_DK] + cum3[:, 2 * GLA_DK:]
            for hh in range(GLA_HEADS):
                kc = slice(hh * GLA_DKH, (hh + 1) * GLA_DKH)
                vc = slice(hh * GLA_DVH, (hh + 1) * GLA_DVH)
                chains.append(dict(di=di, hh=hh, rev=rev, rows=rows, vc=vc, o_ref=o_ref,
                                   masks=consts[1], lane_code=consts[2], b=b_all[:, kc],
                                   qf=q_ref[rows, kc].astype(F32), kf=k_ref[rows, kc].astype(F32),
                                   v=v_ref[rows, vc]))

        for w in chains:
            b, qf, kf, rev = w["b"], w["qf"], w["kf"], w["rev"]
            st = st_ref[w["di"], w["hh"]]
            tot = b[0:1, :] if rev else b[c - 1:c, :]
            w["o"] = _dot_nt((qf * jnp.exp2(b)).astype(BF16), st.astype(BF16))
            kd = (kf * jnp.exp2(tot - b)).astype(BF16)
            st_ref[w["di"], w["hh"]] = st * jnp.exp2(tot) + _dot_tn(w["v"], kd)
            qs, ks = [], []
            for m in GLA_LEVELS:
                pivot = (m - 1) if rev else m
                bm = b.reshape(c // (2 * m), 2 * m, GLA_DKH)
                f = jnp.exp2(-jnp.abs(bm - bm[:, pivot:pivot + 1, :])).reshape(c, GLA_DKH)
                qs.append((qf * f).astype(BF16))
                ks.append((kf * f).astype(BF16))
            w["coarse"] = _dot_nt(jnp.concatenate(qs, axis=0), jnp.concatenate(ks, axis=0))

        grouped = (c // SUBLANES, SUBLANES, GLA_DKH)
        for w in chains:
            qf, kf, rev, lane_code = w["qf"], w["kf"], w["rev"], w["lane_code"]
            w_slots = jnp.where(lane_code == 0, jnp.sum(qf * kf, axis=-1, keepdims=True), 0.0)
            q3, k3, b3 = qf.reshape(grouped), kf.reshape(grouped), w["b"].reshape(grouped)
            for d in range(1, GLA_FINE):
                shift = (SUBLANES - d) if rev else d
                kr = pltpu.roll(k3, shift, 1)
                br = pltpu.roll(b3, shift, 1)
                wd = jnp.sum(q3 * kr * jnp.exp2(jnp.minimum(b3 - br, 0.0)), axis=-1, keepdims=True)
                w_slots = jnp.where(lane_code == d, wd.reshape(c, 1), w_slots)
            w["att"] = pltpu.roll(w_slots, 0, 1, stride=1, stride_axis=0)

        for w in chains:
            att = w["att"]
            for i, mask in enumerate(w["masks"]):
                col0 = (i * c) // LANES * LANES
                att = jnp.where(mask, w["coarse"][i * c:(i + 1) * c, col0:col0 + LANES], att)
            o = w["o"] + _dot(att.astype(BF16), jnp.concatenate([w["v"], w["v"]], axis=0))
            w["o_ref"][w["rows"], w["vc"]] = o.astype(BF16)
        return carry

    lax.fori_loop(0, n_chunks, body, 0)
    sff_ref[...] = st_ref[0]
    sfb_ref[...] = st_ref[1]


def _gla_scan(gq, gk, gv, la_f, la_b, s0_f, s0_b, batch, seq_len):
    rows = min(GLA_ROWS, seq_len)
    nblk = seq_len // rows
    fwd = lambda b, i: (b * nblk + i, 0)
    bwd = lambda b, i: (b * nblk + (nblk - 1 - i), 0)
    smap = lambda b, i: (b, 0, 0, 0)
    state_spec = pl.BlockSpec((None, GLA_HEADS, GLA_DVH, GLA_DKH), smap)
    state_shape = jax.ShapeDtypeStruct((batch, GLA_HEADS, GLA_DVH, GLA_DKH), F32)

    def views(index_map):
        return [pl.BlockSpec((rows, GLA_DK), index_map), pl.BlockSpec((rows, GLA_DK), index_map),
                pl.BlockSpec((rows, GLA_DV), index_map), pl.BlockSpec((rows, GLA_DK), index_map)]

    return pl.pallas_call(
        _gla_kernel,
        out_shape=[jax.ShapeDtypeStruct(gv.shape, BF16), jax.ShapeDtypeStruct(gv.shape, BF16),
                   state_shape, state_shape],
        grid=(batch, nblk),
        in_specs=views(fwd) + views(bwd) + [state_spec, state_spec],
        out_specs=[pl.BlockSpec((rows, GLA_DV), fwd), pl.BlockSpec((rows, GLA_DV), bwd),
                   state_spec, state_spec],
        scratch_shapes=[pltpu.VMEM((2, GLA_HEADS, GLA_DVH, GLA_DKH), F32)],
        compiler_params=_params("parallel", "arbitrary"),
        name="gla_bidir",
    )(gq, gk, gv, la_f, gq, gk, gv, la_b, s0_f, s0_b)


def _merge_kernel(ua_ref, uprev_ref, unext_ref, wa_ref, att_ref, sza_ref, of_ref, ob_ref, szg_ref,
                  gate_ref, x_ref, mod_ref, convw_ref, glag_ref, gpost_ref,
                  wconv_ref, watt_ref, wgla_ref, wout_ref, o_ref, *, tiles_per_seq):
    tm = x_ref.shape[0]
    ti = pl.program_id(0) % tiles_per_seq
    u = ua_ref[...].astype(F32)
    prev_row = jnp.where(ti == 0, 0.0, uprev_ref[BF16_ROWS - 1:BF16_ROWS, :].astype(F32))
    next_row = jnp.where(ti == tiles_per_seq - 1, 0.0, unext_ref[0:1, :].astype(F32))
    rid = lax.broadcasted_iota(jnp.int32, (tm, 1), 0)
    u_prev = jnp.where(rid == 0, prev_row, pltpu.roll(u, 1, 0))
    u_next = jnp.where(rid == tm - 1, next_row, pltpu.roll(u, tm - 1, 0))
    cw = convw_ref[...]
    conv = cw[0:1, :] * u_prev + cw[1:2, :] * u + cw[2:3, :] * u_next
    br_a = _dot((wa_ref[...].astype(F32) * conv).astype(BF16), wconv_ref[...])
    br_b = _dot(att_ref[...] * sza_ref[...], watt_ref[...])
    parts = []
    for hh in range(GLA_HEADS):
        cols = slice(hh * GLA_DVH, (hh + 1) * GLA_DVH)
        oh = of_ref[:, cols].astype(F32) + ob_ref[:, cols].astype(F32)
        parts.append((_head_norm(oh, glag_ref[...]) * szg_ref[:, cols].astype(F32)).astype(BF16))
    br_c = _dot(jnp.concatenate(parts, axis=1), wgla_ref[...])
    merged = (gate_ref[:, :D_MODEL].astype(F32) * br_a
              + gate_ref[:, D_MODEL:2 * D_MODEL].astype(F32) * br_b
              + gate_ref[:, 2 * D_MODEL:].astype(F32) * br_c)
    out = _dot(merged.astype(BF16), wout_ref[...])
    gate = mod_ref[:, 2 * D_MODEL:]
    o_ref[...] = x_ref[...] + gate * _head_norm(out, gpost_ref[...])


def _merge(ua, wa, att, sza, o_f, o_b, szg, gates, x2, mod3, mod_row0, seq_len,
           convw, glag, gpost, wconv, watt, wgla, wout):
    n = x2.shape[0]
    tm = PROJ_TM
    tps = seq_len // tm
    halo = tm // BF16_ROWS
    n_halo = n // BF16_ROWS
    const = lambda i: (0, 0)
    row = lambda i: (i, 0)
    if mod_row0 is None:
        mod_map = lambda i: (i // tps, 0, 0)
    else:
        mod_map = lambda i: (mod_row0, 0, 0)
    tok = lambda w: pl.BlockSpec((tm, w), row)
    wspec = pl.BlockSpec((D_MODEL, D_MODEL), const)
    return pl.pallas_call(
        functools.partial(_merge_kernel, tiles_per_seq=tps),
        out_shape=jax.ShapeDtypeStruct((n, D_MODEL), F32),
        grid=(n // tm,),
        in_specs=[
            tok(CONV_W),
            pl.BlockSpec((BF16_ROWS, CONV_W), lambda i: (jnp.maximum(i * halo - 1, 0), 0)),
            pl.BlockSpec((BF16_ROWS, CONV_W), lambda i: (jnp.minimum((i + 1) * halo, n_halo - 1), 0)),
            tok(CONV_W), tok(Q_W), tok(Q_W), tok(GLA_DV), tok(GLA_DV), tok(GLA_DV),
            tok(N_BRANCH * D_MODEL), tok(D_MODEL),
            pl.BlockSpec((None, 1, 3 * D_MODEL), mod_map),
            pl.BlockSpec((3, CONV_W), const),
            pl.BlockSpec((1, GLA_DVH), const),
            pl.BlockSpec((1, D_MODEL), const),
            wspec, wspec, wspec, wspec,
        ],
        out_specs=tok(D_MODEL),
        compiler_params=_params("parallel"),
        name="merge_out",
    )(ua, ua, ua, wa, att, sza, o_f, o_b, szg, gates, x2, mod3, convw, glag, gpost,
      wconv, watt, wgla, wout)


def _rope_tables(n_tokens):
    n_rows = n_tokens // GRID_W
    row = np.repeat(np.arange(n_rows, dtype=np.float32), GRID_W)
    col = np.tile(np.arange(GRID_W, dtype=np.float32), n_rows)
    freqs = (np.float32(ROPE_THETA) ** (-np.arange(ROPE_PAIRS, dtype=np.float32) * np.float32(2.0)
                                        / np.float32(ROPE_AXIS_DIM))).astype(np.float32)
    ar, ac = row[:, None] * freqs, col[:, None] * freqs
    cos_t = np.concatenate([np.cos(ar), np.cos(ar), np.cos(ac), np.cos(ac)], axis=1)
    sin_t = np.concatenate([-np.sin(ar), np.sin(ar), -np.sin(ac), np.sin(ac)], axis=1)
    return jnp.asarray(cos_t, F32), jnp.asarray(sin_t, F32)


def _split_w_in(w):
    tail = ORIG_R + 2 * GLA_RANK
    low_rank = jnp.pad(w[:, ORIG_R:tail], ((0, 0), (0, LANES - 2 * GLA_RANK)))
    return w[:, :ORIG_R].astype(BF16), w[:, tail:].astype(BF16), low_rank.astype(BF16)


def _pack_decay(w_f, b_f, w_b, b_b):
    wd = jnp.zeros((LANES, 2 * GLA_DK), F32)
    wd = wd.at[:GLA_RANK, :GLA_DK].set(w_f).at[GLA_RANK:2 * GLA_RANK, GLA_DK:].set(w_b)
    return wd.astype(BF16), jnp.concatenate([b_f, b_b])[None, :]


def kernel(x, c, ctx, c_ctx, w_ada, b_ada, g_pre, g_post, w_in, conv_w, q_norm_g, k_norm_g,
           w_decay_fwd, b_decay_fwd, w_decay_bwd, b_decay_bwd, gla_norm_g,
           w_br_conv, w_br_attn, w_br_gla, b_gate, w_out):
    batch, seq, _ = x.shape
    ctx_len = ctx.shape[1]
    assert seq % max(PROJ_TM, ATT_TQ, ATT_KV, GLA_ROWS) == 0 and seq % GRID_W == 0
    assert ctx_len % max(PROJ_TM, ATT_TQ) == 0 and ctx_len % GLA_C == 0

    mod_rows = -(-(batch + 1) // SUBLANES) * SUBLANES
    cvec = jnp.zeros((mod_rows, D_MODEL), F32).at[:batch].set(c).at[batch].set(c_ctx)
    mod = _modulation(cvec, w_ada, b_ada)
    cos_t, sin_t = _rope_tables(seq)
    zero_state = jnp.zeros((batch, GLA_HEADS, GLA_DVH, GLA_DKH), F32)

    xl = x.reshape(batch * seq, D_MODEL)
    xc = ctx.reshape(batch * ctx_len, D_MODEL)
    for l in range(DEPTH):
        last = l == DEPTH - 1
        mod3 = mod[l][:, None, :]
        w_parts = _split_w_in(w_in[l])
        wdec, bdec = _pack_decay(w_decay_fwd[l], b_decay_fwd[l], w_decay_bwd[l], b_decay_bwd[l])
        shared_in = (g_pre[l][None], w_parts, wdec, bdec, q_norm_g[l][None], k_norm_g[l][None])
        bgate = b_gate[l][None]
        pc = _projection(xc, mod3, batch, ctx_len, *shared_in, cos_t, sin_t, bgate, use_rope=False)
        pl_ = _projection(xl, mod3, None, seq, *shared_in, cos_t, sin_t, bgate, use_rope=True)
        (ua_c, wa_c, q_c, k_c, vt_c, sza_c, gq_c, gk_c, gv_c, laf_c, lab_c, szg_c, gate_c) = pc
        (ua_l, wa_l, q_l, k_l, vt_l, sza_l, gq_l, gk_l, gv_l, laf_l, lab_l, szg_l, gate_l) = pl_

        att_l = _attention(q_l, k_l, vt_l, k_c, vt_c, batch, seq, seq, ctx_len)
        of_c, ob_c, s_f, s_b = _gla_scan(gq_c, gk_c, gv_c, laf_c, lab_c, zero_state, zero_state,
                                         batch, ctx_len)
        of_l, ob_l, _, _ = _gla_scan(gq_l, gk_l, gv_l, laf_l, lab_l, s_f, s_b, batch, seq)

        shared_out = (conv_w[l], gla_norm_g[l][None], g_post[l][None],
                      w_br_conv[l].astype(BF16), w_br_attn[l].astype(BF16),
                      w_br_gla[l].astype(BF16), w_out[l].astype(BF16))
        if not last:
            att_c = _attention(q_c, None, None, k_c, vt_c, batch, ctx_len, 0, ctx_len)
            xc = _merge(ua_c, wa_c, att_c, sza_c, of_c, ob_c, szg_c, gate_c, xc, mod3, batch, ctx_len,
                        *shared_out)
        xl = _merge(ua_l, wa_l, att_l, sza_l, of_l, ob_l, szg_l, gate_l, xl, mod3, None, seq,
                    *shared_out)
    return xl.reshape(batch, seq, D_MODEL)
```

```python
import functools

import jax
import jax.numpy as jnp
import numpy as np
from jax import lax
from jax.experimental import pallas as pl
from jax.experimental.pallas import tpu as pltpu

F32 = jnp.float32
BF16 = jnp.bfloat16

D_MODEL = 1024
DEPTH = 2
GRID_W = 64
CONV_W = 1024
N_HEADS = 8
N_KV_HEADS = 2
HEAD_DIM = 128
GROUP = N_HEADS // N_KV_HEADS
ROPE_THETA = 10000.0
ROPE_AXIS_DIM = HEAD_DIM // 2
ROPE_PAIRS = ROPE_AXIS_DIM // 2
ATTN_SCALE = HEAD_DIM ** -0.5
LOG2_E = 1.4426950408889634
GLA_HEADS = 4
GLA_DK = D_MODEL // 2
GLA_DV = D_MODEL
GLA_DKH = GLA_DK // GLA_HEADS
GLA_DVH = GLA_DV // GLA_HEADS
GLA_RANK = 16
GLA_TAU = 16.0
N_BRANCH = 3
EPS = 1e-6

Q_W = N_HEADS * HEAD_DIM
KV_W = N_KV_HEADS * HEAD_DIM

LANES = 128
SUBLANES = 8
BF16_ROWS = 16
VT_ROWS = HEAD_DIM + BF16_ROWS
VMEM_LIMIT = 56 * 1024 * 1024

OFF_A_B = 0
OFF_A_C = OFF_A_B + CONV_W
OFF_A_X = OFF_A_C + CONV_W
OFF_A_Z = OFF_A_X + CONV_W
OFF_Q = OFF_A_Z + CONV_W
OFF_K = OFF_Q + Q_W
OFF_V = OFF_K + KV_W
OFF_Z_ATT = OFF_V + KV_W
OFF_GQ = OFF_Z_ATT + Q_W
OFF_GK = OFF_GQ + GLA_DK
OFF_GV = OFF_GK + GLA_DK
OFF_Z_GLA = OFF_GV + GLA_DV
OFF_MG = OFF_Z_GLA + GLA_DV
OFF_R = OFF_MG + N_BRANCH * D_MODEL
ORIG_R = OFF_Z_GLA

PROJ_TM = 256
COL_BLK = 256
ATT_TQ = 512
ATT_KV = 512
GLA_C = 64
GLA_FINE = 4
GLA_ROWS = 512
GLA_LEVELS = tuple(GLA_C >> (i + 1) for i in range(int(np.log2(GLA_C // GLA_FINE))))
assert len(GLA_LEVELS) * GLA_C % LANES == 0 and 2 * GLA_C == LANES and GLA_FINE <= SUBLANES


def _sigmoid(x):
    return jax.nn.sigmoid(x)


def _silu(x):
    return x * _sigmoid(x)


def _dot(a, b):
    return jnp.dot(a, b, preferred_element_type=F32)


def _dot_nt(a, b):
    return lax.dot_general(a, b, (((1,), (1,)), ((), ())), preferred_element_type=F32)


def _dot_tn(a, b):
    return lax.dot_general(a, b, (((0,), (0,)), ((), ())), preferred_element_type=F32)


def _params(*sem):
    return pltpu.CompilerParams(dimension_semantics=sem, vmem_limit_bytes=VMEM_LIMIT)


def _mod_kernel(c_ref, w_ref, b_ref, o_ref):
    s = _silu(c_ref[...])
    o_ref[...] = _dot(s.astype(BF16), w_ref[...].astype(BF16)) + b_ref[...]


def _modulation(cvec, w_ada, b_ada):
    rows = cvec.shape[0]
    n_col = 3 * D_MODEL // D_MODEL
    return pl.pallas_call(
        _mod_kernel,
        out_shape=jax.ShapeDtypeStruct((DEPTH, rows, 3 * D_MODEL), F32),
        grid=(DEPTH, n_col),
        in_specs=[
            pl.BlockSpec((rows, D_MODEL), lambda l, j: (0, 0)),
            pl.BlockSpec((None, D_MODEL, D_MODEL), lambda l, j: (l, 0, j)),
            pl.BlockSpec((None, 1, D_MODEL), lambda l, j: (l, 0, j)),
        ],
        out_specs=pl.BlockSpec((None, rows, D_MODEL), lambda l, j: (l, 0, j)),
        compiler_params=_params("parallel", "parallel"),
        name="adaln_mod",
    )(cvec, w_ada, b_ada.reshape(DEPTH, 1, 3 * D_MODEL))


def _head_norm(xh, g):
    ms = jnp.mean(xh * xh, axis=-1, keepdims=True)
    return xh * lax.rsqrt(ms + EPS) * g


def _rope(xh, cos, sin):
    lane = lax.broadcasted_iota(jnp.int32, xh.shape, 1)
    first_half = (lane % ROPE_AXIS_DIM) < ROPE_PAIRS
    partner = jnp.where(first_half,
                        pltpu.roll(xh, HEAD_DIM - ROPE_PAIRS, 1),
                        pltpu.roll(xh, ROPE_PAIRS, 1))
    return xh * cos + partner * sin


def _log_sigmoid(x):
    return jnp.minimum(x, 0.0) - jnp.log(1.0 + jnp.exp(-jnp.abs(x)))


def _proj_kernel(x_ref, mod_ref, xn_ref, modn_ref, gpre_ref, w_lo_ref, w_hi_ref, w_rank_ref,
                 wdec_ref, bdec_ref, qg_ref, kg_ref, cos_ref, sin_ref, bgate_ref,
                 ua_ref, wa_ref, q_ref, k_ref, vt_ref, sza_ref, gq_ref, gk_ref, gv_ref,
                 laf_ref, lab_ref, szg_ref, gate_ref, h_ref, v_ref, *, use_rope):
    step = pl.program_id(0)
    cur = step % 2

    def modulated_norm(xr, modr, slot):
        xv = xr[...]
        ms = jnp.mean(xv * xv, axis=-1, keepdims=True)
        y = xv * lax.rsqrt(ms + EPS) * gpre_ref[...]
        mod = modr[...]
        h_ref[slot] = (y * (1.0 + mod[:, D_MODEL:2 * D_MODEL]) + mod[:, :D_MODEL]).astype(BF16)

    @pl.when(step == 0)
    def _():
        modulated_norm(x_ref, mod_ref, 0)

    def proj(off, width=COL_BLK):
        for ref, base in ((w_rank_ref, OFF_R), (w_hi_ref, OFF_Z_GLA), (w_lo_ref, 0)):
            if off >= base:
                return _dot(h_ref[cur], ref[:, off - base:off - base + width])

    for o in range(0, CONV_W, COL_BLK):
        cols = slice(o, o + COL_BLK)
        ua_ref[:, cols] = (proj(OFF_A_C + o) * proj(OFF_A_X + o)).astype(BF16)
        wa_ref[:, cols] = (proj(OFF_A_B + o) * _silu(proj(OFF_A_Z + o))).astype(BF16)
        if o == 0:
            modulated_norm(xn_ref, modn_ref, 1 - cur)

    def heads(off, width, gain, out_ref, out_scale):
        for o in range(0, width, COL_BLK):
            blk = proj(off + o)
            for hh in range(COL_BLK // HEAD_DIM):
                xh = _head_norm(blk[:, hh * HEAD_DIM:(hh + 1) * HEAD_DIM], gain)
                if use_rope:
                    xh = _rope(xh, cos_ref[...], sin_ref[...])
                c0 = o + hh * HEAD_DIM
                out_ref[:, c0:c0 + HEAD_DIM] = (xh * out_scale).astype(BF16)

    heads(OFF_Q, Q_W, qg_ref[...], q_ref, ATTN_SCALE * LOG2_E)
    heads(OFF_K, KV_W, kg_ref[...], k_ref, 1.0)
    v_ref[...] = proj(OFF_V, KV_W)
    ones_tile = jnp.where(lax.broadcasted_iota(jnp.int32, (BF16_ROWS, x_ref.shape[0]), 0) == 0, 1.0, 0.0)
    for hh in range(N_KV_HEADS):
        vt_ref[hh * VT_ROWS:hh * VT_ROWS + HEAD_DIM, :] = (
            v_ref[:, hh * HEAD_DIM:(hh + 1) * HEAD_DIM].T.astype(BF16))
        vt_ref[hh * VT_ROWS + HEAD_DIM:(hh + 1) * VT_ROWS, :] = ones_tile.astype(BF16)
    for o in range(0, Q_W, COL_BLK):
        sza_ref[:, o:o + COL_BLK] = _silu(proj(OFF_Z_ATT + o)).astype(BF16)

    for o in range(0, GLA_DK, COL_BLK):
        gq_ref[:, o:o + COL_BLK] = (proj(OFF_GQ + o) * (GLA_DKH ** -0.5)).astype(BF16)
        gk_ref[:, o:o + COL_BLK] = proj(OFF_GK + o).astype(BF16)
    for o in range(0, GLA_DV, COL_BLK):
        szg_ref[:, o:o + COL_BLK] = _silu(proj(OFF_Z_GLA + o)).astype(BF16)

    r = proj(OFF_R, LANES).astype(BF16)

    def decay_piece(out_ref, o, wcol):
        out_ref[:, o:o + COL_BLK] = _log_sigmoid(
            _dot(r, wdec_ref[:, wcol:wcol + COL_BLK]) + bdec_ref[:, wcol:wcol + COL_BLK]
        ) * (LOG2_E / GLA_TAU)

    decay_pieces = ([(laf_ref, o, o) for o in range(0, GLA_DK, COL_BLK)]
                    + [(lab_ref, o, GLA_DK + o) for o in range(0, GLA_DK, COL_BLK)])
    for i, o in enumerate(range(0, N_BRANCH * D_MODEL, COL_BLK)):
        gate_ref[:, o:o + COL_BLK] = _sigmoid(
            proj(OFF_MG + o) + bgate_ref[:, o:o + COL_BLK]).astype(BF16)
        if i < len(decay_pieces):
            decay_piece(*decay_pieces[i])
    assert len(decay_pieces) <= N_BRANCH * D_MODEL // COL_BLK

    for o in range(0, GLA_DV, COL_BLK):
        gv_ref[:, o:o + COL_BLK] = proj(OFF_GV + o).astype(BF16)


def _projection(x2, mod3, mod_row0, seq_len, gpre, w_parts, wdec, bdec, qg, kg, cos_t, sin_t, bgate,
                use_rope):
    n = x2.shape[0]
    tm = PROJ_TM
    tps = seq_len // tm
    const = lambda i: (0, 0)
    row = lambda i: (i, 0)
    pos = lambda i: (i % tps, 0)
    nxt = lambda i: jnp.minimum(i + 1, n // tm - 1)
    if mod_row0 is None:
        mod_map = lambda i: (i // tps, 0, 0)
    else:
        mod_map = lambda i: (mod_row0, 0, 0)
    VT_OUT = 4
    widths = [(CONV_W, BF16), (CONV_W, BF16), (Q_W, BF16), (KV_W, BF16), (N_KV_HEADS * VT_ROWS, BF16), (Q_W, BF16),
              (GLA_DK, BF16), (GLA_DK, BF16), (GLA_DV, BF16), (GLA_DK, F32), (GLA_DK, F32),
              (GLA_DV, BF16), (N_BRANCH * D_MODEL, BF16)]
    return pl.pallas_call(
        functools.partial(_proj_kernel, use_rope=use_rope),
        out_shape=[jax.ShapeDtypeStruct((w, n) if i == VT_OUT else (n, w), dt)
                   for i, (w, dt) in enumerate(widths)],
        grid=(n // tm,),
        in_specs=[
            pl.BlockSpec((tm, D_MODEL), row),
            pl.BlockSpec((None, 1, 3 * D_MODEL), mod_map),
            pl.BlockSpec((tm, D_MODEL), lambda i: (nxt(i), 0)),
            pl.BlockSpec((None, 1, 3 * D_MODEL), lambda i: mod_map(nxt(i))),
            pl.BlockSpec((1, D_MODEL), const),
            pl.BlockSpec((D_MODEL, OFF_Z_GLA), const, pipeline_mode=pl.Buffered(1)),
            pl.BlockSpec((D_MODEL, OFF_R - OFF_Z_GLA), const, pipeline_mode=pl.Buffered(1)),
            pl.BlockSpec((D_MODEL, LANES), const, pipeline_mode=pl.Buffered(1)),
            pl.BlockSpec((LANES, 2 * GLA_DK), const),
            pl.BlockSpec((1, 2 * GLA_DK), const),
            pl.BlockSpec((1, HEAD_DIM), const),
            pl.BlockSpec((1, HEAD_DIM), const),
            pl.BlockSpec((tm, HEAD_DIM), pos),
            pl.BlockSpec((tm, HEAD_DIM), pos),
            pl.BlockSpec((1, N_BRANCH * D_MODEL), const),
        ],
        out_specs=[pl.BlockSpec((w, tm), lambda i: (0, i)) if j == VT_OUT else pl.BlockSpec((tm, w), row)
                   for j, (w, _) in enumerate(widths)],
        scratch_shapes=[pltpu.VMEM((2, tm, D_MODEL), BF16), pltpu.VMEM((tm, KV_W), F32)],
        compiler_params=_params("arbitrary"),
        name="in_proj_rope" if use_rope else "in_proj",
    )(x2, mod3, x2, mod3, gpre, *w_parts, wdec, bdec, qg, kg, cos_t, sin_t, bgate)


def _attn_kernel(*refs, n_lat_blocks):
    if n_lat_blocks:
        (q_ref, kl_ref, vtl_ref, kc_ref, vtc_ref, o_ref, m_ref, alpha_ref, mc_ref, acc_ref,
         sc_ref, pc_ref, ma_ref, mb_ref, sa_ref, sb_ref, pa_ref, pb_ref) = refs
    else:
        q_ref, kc_ref, vtc_ref, o_ref, m_ref, alpha_ref, mc_ref, acc_ref, sc_ref, pc_ref = refs
    tq = q_ref.shape[0]
    m_ref[...] = jnp.full(m_ref.shape, -jnp.inf, F32)
    acc_ref[...] = jnp.zeros(acc_ref.shape, F32)

    def step(scores=None, softmax=None, values=None):
        heads = [slice(g * tq, (g + 1) * tq) for g in range(GROUP)]
        if scores is not None:
            k, s_dst, smax_dst = scores
            for g, cols in enumerate(heads):
                s = _dot_nt(k, q_ref[:, g * HEAD_DIM:(g + 1) * HEAD_DIM])
                s_dst[:, cols] = s
                smax_dst[:, cols] = jnp.max(s, axis=0, keepdims=True)
        if values is not None:
            p_src, vt = values
            for cols in heads:
                acc_ref[:, cols] = alpha_ref[:, cols] * acc_ref[:, cols] + _dot(vt, p_src[:, cols])
        if softmax is not None:
            s_src, smax_src, p_dst = softmax
            for cols in heads:
                m_old = m_ref[:, cols]
                m_new = jnp.maximum(m_old, smax_src[:, cols])
                alpha = jnp.exp2(m_old - m_new)
                p_dst[:, cols] = jnp.exp2(s_src[:, cols] - m_new).astype(BF16)
                alpha_ref[:, cols] = alpha
                m_ref[:, cols] = m_new

    def keys(j):
        return pl.ds(pl.multiple_of(j * ATT_KV, ATT_KV), ATT_KV)

    if n_lat_blocks:
        buf_a, buf_b, buf_c = (sa_ref, ma_ref), (sb_ref, mb_ref), (sc_ref, mc_ref)

        def pair(i, carry):
            t = 2 * i
            step(scores=(kl_ref[keys(t), :], *buf_a), values=(pa_ref, vtl_ref[:, keys(t - 2)]),
                 softmax=(*buf_b, pb_ref))
            step(scores=(kl_ref[keys(t + 1), :], *buf_b), values=(pb_ref, vtl_ref[:, keys(t - 1)]),
                 softmax=(*buf_a, pa_ref))
            return carry

        n = n_lat_blocks
        step(scores=(kl_ref[keys(0), :], *buf_a))
        step(scores=(kl_ref[keys(1), :], *buf_b), softmax=(*buf_a, pa_ref))
        lax.fori_loop(1, n // 2, pair, 0)
        step(scores=(kc_ref[...], *buf_c), values=(pa_ref, vtl_ref[:, keys(n - 2)]),
             softmax=(*buf_b, pb_ref))
        step(values=(pb_ref, vtl_ref[:, keys(n - 1)]), softmax=(*buf_c, pc_ref))
    else:
        step(scores=(kc_ref[...], sc_ref, mc_ref))
        step(softmax=(sc_ref, mc_ref, pc_ref))
    step(values=(pc_ref, vtc_ref[...]))
    out_t = acc_ref[:HEAD_DIM, :] / acc_ref[HEAD_DIM:HEAD_DIM + 1, :]
    for g in range(GROUP):
        o_ref[:, g * HEAD_DIM:(g + 1) * HEAD_DIM] = out_t[:, g * tq:(g + 1) * tq].T.astype(BF16)


def _attention(q2, k_lat, vt_lat, k_ctx, vt_ctx, batch, q_len, lat_len, ctx_len):
    tq = min(ATT_TQ, q_len)
    tiles = q_len // tq
    qmap = lambda b, kv, i: (b * tiles + i, kv)
    kmap = lambda b, kv, i: (b, kv)
    vtmap = lambda b, kv, i: (kv, b)
    in_specs = [pl.BlockSpec((tq, GROUP * HEAD_DIM), qmap)]
    args = [q2]
    n_lat_blocks = 0
    if k_lat is not None:
        n_lat_blocks = lat_len // ATT_KV
        in_specs += [pl.BlockSpec((lat_len, HEAD_DIM), kmap), pl.BlockSpec((VT_ROWS, lat_len), vtmap)]
        args += [k_lat, vt_lat]
    in_specs += [pl.BlockSpec((ctx_len, HEAD_DIM), kmap), pl.BlockSpec((VT_ROWS, ctx_len), vtmap)]
    args += [k_ctx, vt_ctx]
    rows = GROUP * tq
    stat = pltpu.VMEM((1, rows), F32)
    scratch = [stat, stat, stat, pltpu.VMEM((VT_ROWS, rows), F32),
               pltpu.VMEM((ctx_len, rows), F32), pltpu.VMEM((ctx_len, rows), BF16)]
    if n_lat_blocks:
        assert n_lat_blocks % 2 == 0
        scratch += [stat, stat] + [pltpu.VMEM((ATT_KV, rows), F32)] * 2 + [pltpu.VMEM((ATT_KV, rows), BF16)] * 2
    return pl.pallas_call(
        functools.partial(_attn_kernel, n_lat_blocks=n_lat_blocks),
        out_shape=jax.ShapeDtypeStruct(q2.shape, BF16),
        grid=(batch, N_KV_HEADS, tiles),
        in_specs=in_specs,
        out_specs=pl.BlockSpec((tq, GROUP * HEAD_DIM), qmap),
        scratch_shapes=scratch,
        compiler_params=_params("parallel", "parallel", "parallel"),
        name="gqa_lat" if n_lat_blocks else "gqa_ctx",
    )(*args)


def _split3(x):
    hi = x.astype(BF16)
    r1 = x - hi.astype(F32)
    mid = r1.astype(BF16)
    lo = (r1 - mid.astype(F32)).astype(BF16)
    return hi, mid, lo


def _gla_consts(rev):
    c = GLA_C
    tri_r = lax.broadcasted_iota(jnp.int32, (c, c), 0)
    tri_c = lax.broadcasted_iota(jnp.int32, (c, c), 1)
    tri = jnp.where((tri_c >= tri_r) if rev else (tri_c <= tri_r), 1.0, 0.0).astype(BF16)
    row = lax.broadcasted_iota(jnp.int32, (c, LANES), 0)
    lane = lax.broadcasted_iota(jnp.int32, (c, LANES), 1)
    level_masks = []
    for i, m in enumerate(GLA_LEVELS):
        key = lane - c * (i % 2)
        in_half = (key >= 0) & (key < c)
        same = (row // (2 * m)) == (key // (2 * m))
        q_upper = (row % (2 * m)) >= m
        k_upper = (key % (2 * m)) >= m
        pair = (~q_upper & k_upper) if rev else (q_upper & ~k_upper)
        level_masks.append(in_half & same & pair)
    sub = row % GLA_FINE
    d_of_lane = lane if rev else (LANES - lane) % LANES
    ok = (sub + d_of_lane < GLA_FINE) if rev else (sub >= d_of_lane)
    lane_code = jnp.where((d_of_lane < GLA_FINE) & ok, d_of_lane, -1)
    return tri, level_masks, lane_code


def _gla_kernel(qf_ref, kf_ref, vf_ref, laf_ref, qb_ref, kb_ref, vb_ref, lab_ref, s0f_ref, s0b_ref,
                of_ref, ob_ref, sff_ref, sfb_ref, st_ref):
    @pl.when(pl.program_id(1) == 0)
    def _():
        st_ref[0] = s0f_ref[...]
        st_ref[1] = s0b_ref[...]

    c = GLA_C
    n_chunks = qf_ref.shape[0] // c
    dirs = ((0, False, qf_ref, kf_ref, vf_ref, laf_ref, of_ref, _gla_consts(False)),
            (1, True, qb_ref, kb_ref, vb_ref, lab_ref, ob_ref, _gla_consts(True)))

    def body(ci, carry):
        chains = []
        for di, rev, q_ref, k_ref, v_ref, la_ref, o_ref, consts in dirs:
            cc = (n_chunks - 1 - ci) if rev else ci
            rows = pl.ds(pl.multiple_of(cc * c, c), c)
            cum3 = _dot(consts[0], jnp.concatenate(_split3(la_ref[rows, :]), axis=1))
            b_all = cum3[:, :GLA_DK] + cum3[:, GLA_DK:2 * GLA_DK] + cum3[:, 2 * GLA_DK:]
            for hh in range(GLA_HEADS):
                kc = slice(hh * GLA_DKH, (hh + 1) * GLA_DKH)
                vc = slice(hh * GLA_DVH, (hh + 1) * GLA_DVH)
                chains.append(dict(di=di, hh=hh, rev=rev, rows=rows, vc=vc, o_ref=o_ref,
                                   masks=consts[1], lane_code=consts[2], b=b_all[:, kc],
                                   qf=q_ref[rows, kc].astype(F32), kf=k_ref[rows, kc].astype(F32),
                                   v=v_ref[rows, vc]))

        for w in chains:
            b, qf, kf, rev = w["b"], w["qf"], w["kf"], w["rev"]
            st = st_ref[w["di"], w["hh"]]
            tot = b[0:1, :] if rev else b[c - 1:c, :]
            w["o"] = _dot_nt((qf * jnp.exp2(b)).astype(BF16), st.astype(BF16))
            kd = (kf * jnp.exp2(tot - b)).astype(BF16)
            st_ref[w["di"], w["hh"]] = st * jnp.exp2(tot) + _dot_tn(w["v"], kd)
            qs, ks = [], []
            for m in GLA_LEVELS:
                pivot = (m - 1) if rev else m
                bm = b.reshape(c // (2 * m), 2 * m, GLA_DKH)
                f = jnp.exp2(-jnp.abs(bm - bm[:, pivot:pivot + 1, :])).reshape(c, GLA_DKH)
                qs.append((qf * f).astype(BF16))
                ks.append((kf * f).astype(BF16))
            w["coarse"] = _dot_nt(jnp.concatenate(qs, axis=0), jnp.concatenate(ks, axis=0))

        grouped = (c // SUBLANES, SUBLANES, GLA_DKH)
        for w in chains:
            qf, kf, rev, lane_code = w["qf"], w["kf"], w["rev"], w["lane_code"]
            w_slots = jnp.where(lane_code == 0, jnp.sum(qf * kf, axis=-1, keepdims=True), 0.0)
            q3, k3, b3 = qf.reshape(grouped), kf.reshape(grouped), w["b"].reshape(grouped)
            for d in range(1, GLA_FINE):
                shift = (SUBLANES - d) if rev else d
                kr = pltpu.roll(k3, shift, 1)
                br = pltpu.roll(b3, shift, 1)
                wd = jnp.sum(q3 * kr * jnp.exp2(jnp.minimum(b3 - br, 0.0)), axis=-1, keepdims=True)
                w_slots = jnp.where(lane_code == d, wd.reshape(c, 1), w_slots)
            w["att"] = pltpu.roll(w_slots, 0, 1, stride=1, stride_axis=0)

        for w in chains:
            att = w["att"]
            for i, mask in enumerate(w["masks"]):
                col0 = (i * c) // LANES * LANES
                att = jnp.where(mask, w["coarse"][i * c:(i + 1) * c, col0:col0 + LANES], att)
            o = w["o"] + _dot(att.astype(BF16), jnp.concatenate([w["v"], w["v"]], axis=0))
            w["o_ref"][w["rows"], w["vc"]] = o.astype(BF16)
        return carry

    lax.fori_loop(0, n_chunks, body, 0)
    sff_ref[...] = st_ref[0]
    sfb_ref[...] = st_ref[1]


def _gla_scan(gq, gk, gv, la_f, la_b, s0_f, s0_b, batch, seq_len):
    rows = min(GLA_ROWS, seq_len)
    nblk = seq_len // rows
    fwd = lambda b, i: (b * nblk + i, 0)
    bwd = lambda b, i: (b * nblk + (nblk - 1 - i), 0)
    smap = lambda b, i: (b, 0, 0, 0)
    state_spec = pl.BlockSpec((None, GLA_HEADS, GLA_DVH, GLA_DKH), smap)
    state_shape = jax.ShapeDtypeStruct((batch, GLA_HEADS, GLA_DVH, GLA_DKH), F32)

    def views(index_map):
        return [pl.BlockSpec((rows, GLA_DK), index_map), pl.BlockSpec((rows, GLA_DK), index_map),
                pl.BlockSpec((rows, GLA_DV), index_map), pl.BlockSpec((rows, GLA_DK), index_map)]

    return pl.pallas_call(
        _gla_kernel,
        out_shape=[jax.ShapeDtypeStruct(gv.shape, BF16), jax.ShapeDtypeStruct(gv.shape, BF16),
                   state_shape, state_shape],
        grid=(batch, nblk),
        in_specs=views(fwd) + views(bwd) + [state_spec, state_spec],
        out_specs=[pl.BlockSpec((rows, GLA_DV), fwd), pl.BlockSpec((rows, GLA_DV), bwd),
                   state_spec, state_spec],
        scratch_shapes=[pltpu.VMEM((2, GLA_HEADS, GLA_DVH, GLA_DKH), F32)],
        compiler_params=_params("parallel", "arbitrary"),
        name="gla_bidir",
    )(gq, gk, gv, la_f, gq, gk, gv, la_b, s0_f, s0_b)


def _merge_kernel(ua_ref, uprev_ref, unext_ref, wa_ref, att_ref, sza_ref, of_ref, ob_ref, szg_ref,
                  gate_ref, x_ref, mod_ref, convw_ref, glag_ref, gpost_ref,
                  wconv_ref, watt_ref, wgla_ref, wout_ref, o_ref, *, tiles_per_seq):
    tm = x_ref.shape[0]
    ti = pl.program_id(0) % tiles_per_seq
    br_b = _dot(att_ref[...] * sza_ref[...], watt_ref[...])
    u = ua_ref[...].astype(F32)
    prev_row = jnp.where(ti == 0, 0.0, uprev_ref[BF16_ROWS - 1:BF16_ROWS, :].astype(F32))
    next_row = jnp.where(ti == tiles_per_seq - 1, 0.0, unext_ref[0:1, :].astype(F32))
    rid = lax.broadcasted_iota(jnp.int32, (tm, 1), 0)
    u_prev = jnp.where(rid == 0, prev_row, pltpu.roll(u, 1, 0))
    u_next = jnp.where(rid == tm - 1, next_row, pltpu.roll(u, tm - 1, 0))
    cw = convw_ref[...]
    conv = cw[0:1, :] * u_prev + cw[1:2, :] * u + cw[2:3, :] * u_next
    br_a = _dot((wa_ref[...].astype(F32) * conv).astype(BF16), wconv_ref[...])
    parts = []
    for hh in range(GLA_HEADS):
        cols = slice(hh * GLA_DVH, (hh + 1) * GLA_DVH)
        oh = of_ref[:, cols].astype(F32) + ob_ref[:, cols].astype(F32)
        parts.append((_head_norm(oh, glag_ref[...]) * szg_ref[:, cols].astype(F32)).astype(BF16))
    br_c = _dot(jnp.concatenate(parts, axis=1), wgla_ref[...])
    merged = (gate_ref[:, :D_MODEL].astype(F32) * br_a
              + gate_ref[:, D_MODEL:2 * D_MODEL].astype(F32) * br_b
              + gate_ref[:, 2 * D_MODEL:].astype(F32) * br_c)
    out = _dot(merged.astype(BF16), wout_ref[...])
    gate = mod_ref[:, 2 * D_MODEL:]
    o_ref[...] = x_ref[...] + gate * _head_norm(out, gpost_ref[...])


def _merge(ua, wa, att, sza, o_f, o_b, szg, gates, x2, mod3, mod_row0, seq_len,
           convw, glag, gpost, wconv, watt, wgla, wout):
    n = x2.shape[0]
    tm = PROJ_TM
    tps = seq_len // tm
    halo = tm // BF16_ROWS
    n_halo = n // BF16_ROWS
    const = lambda i: (0, 0)
    row = lambda i: (i, 0)
    if mod_row0 is None:
        mod_map = lambda i: (i // tps, 0, 0)
    else:
        mod_map = lambda i: (mod_row0, 0, 0)
    tok = lambda w: pl.BlockSpec((tm, w), row)
    wspec = pl.BlockSpec((D_MODEL, D_MODEL), const)
    return pl.pallas_call(
        functools.partial(_merge_kernel, tiles_per_seq=tps),
        out_shape=jax.ShapeDtypeStruct((n, D_MODEL), F32),
        grid=(n // tm,),
        in_specs=[
            tok(CONV_W),
            pl.BlockSpec((BF16_ROWS, CONV_W), lambda i: (jnp.maximum(i * halo - 1, 0), 0)),
            pl.BlockSpec((BF16_ROWS, CONV_W), lambda i: (jnp.minimum((i + 1) * halo, n_halo - 1), 0)),
            tok(CONV_W), tok(Q_W), tok(Q_W), tok(GLA_DV), tok(GLA_DV), tok(GLA_DV),
            tok(N_BRANCH * D_MODEL), tok(D_MODEL),
            pl.BlockSpec((None, 1, 3 * D_MODEL), mod_map),
            pl.BlockSpec((3, CONV_W), const),
            pl.BlockSpec((1, GLA_DVH), const),
            pl.BlockSpec((1, D_MODEL), const),
            wspec, wspec, wspec, wspec,
        ],
        out_specs=tok(D_MODEL),
        compiler_params=_params("parallel"),
        name="merge_out",
    )(ua, ua, ua, wa, att, sza, o_f, o_b, szg, gates, x2, mod3, convw, glag, gpost,
      wconv, watt, wgla, wout)


def _rope_tables(n_tokens):
    n_rows = n_tokens // GRID_W
    row = np.repeat(np.arange(n_rows, dtype=np.float32), GRID_W)
    col = np.tile(np.arange(GRID_W, dtype=np.float32), n_rows)
    freqs = (np.float32(ROPE_THETA) ** (-np.arange(ROPE_PAIRS, dtype=np.float32) * np.float32(2.0)
                                        / np.float32(ROPE_AXIS_DIM))).astype(np.float32)
    ar, ac = row[:, None] * freqs, col[:, None] * freqs
    cos_t = np.concatenate([np.cos(ar), np.cos(ar), np.cos(ac), np.cos(ac)], axis=1)
    sin_t = np.concatenate([-np.sin(ar), np.sin(ar), -np.sin(ac), np.sin(ac)], axis=1)
    return jnp.asarray(cos_t, F32), jnp.asarray(sin_t, F32)


def _split_w_in(w):
    tail = ORIG_R + 2 * GLA_RANK
    low_rank = jnp.pad(w[:, ORIG_R:tail], ((0, 0), (0, LANES - 2 * GLA_RANK)))
    return w[:, :ORIG_R].astype(BF16), w[:, tail:].astype(BF16), low_rank.astype(BF16)


def _pack_decay(w_f, b_f, w_b, b_b):
    wd = jnp.zeros((LANES, 2 * GLA_DK), F32)
    wd = wd.at[:GLA_RANK, :GLA_DK].set(w_f).at[GLA_RANK:2 * GLA_RANK, GLA_DK:].set(w_b)
    return wd.astype(BF16), jnp.concatenate([b_f, b_b])[None, :]


def kernel(x, c, ctx, c_ctx, w_ada, b_ada, g_pre, g_post, w_in, conv_w, q_norm_g, k_norm_g,
           w_decay_fwd, b_decay_fwd, w_decay_bwd, b_decay_bwd, gla_norm_g,
           w_br_conv, w_br_attn, w_br_gla, b_gate, w_out):
    batch, seq, _ = x.shape
    ctx_len = ctx.shape[1]
    assert seq % max(PROJ_TM, ATT_TQ, ATT_KV, GLA_ROWS) == 0 and seq % GRID_W == 0
    assert ctx_len % PROJ_TM == 0 and ctx_len % GLA_C == 0 and ATT_TQ % ctx_len == 0

    mod_rows = -(-(batch + 1) // SUBLANES) * SUBLANES
    cvec = jnp.zeros((mod_rows, D_MODEL), F32).at[:batch].set(c).at[batch].set(c_ctx)
    mod = _modulation(cvec, w_ada, b_ada)
    cos_t, sin_t = _rope_tables(seq)
    zero_state = jnp.zeros((batch, GLA_HEADS, GLA_DVH, GLA_DKH), F32)

    xl = x.reshape(batch * seq, D_MODEL)
    xc = ctx.reshape(batch * ctx_len, D_MODEL)
    for l in range(DEPTH):
        last = l == DEPTH - 1
        mod3 = mod[l][:, None, :]
        w_parts = _split_w_in(w_in[l])
        wdec, bdec = _pack_decay(w_decay_fwd[l], b_decay_fwd[l], w_decay_bwd[l], b_decay_bwd[l])
        shared_in = (g_pre[l][None], w_parts, wdec, bdec, q_norm_g[l][None], k_norm_g[l][None])
        bgate = b_gate[l][None]
        pc = _projection(xc, mod3, batch, ctx_len, *shared_in, cos_t, sin_t, bgate, use_rope=False)
        pl_ = _projection(xl, mod3, None, seq, *shared_in, cos_t, sin_t, bgate, use_rope=True)
        (ua_c, wa_c, q_c, k_c, vt_c, sza_c, gq_c, gk_c, gv_c, laf_c, lab_c, szg_c, gate_c) = pc
        (ua_l, wa_l, q_l, k_l, vt_l, sza_l, gq_l, gk_l, gv_l, laf_l, lab_l, szg_l, gate_l) = pl_

        att_l = _attention(q_l, k_l, vt_l, k_c, vt_c, batch, seq, seq, ctx_len)
        of_c, ob_c, s_f, s_b = _gla_scan(gq_c, gk_c, gv_c, laf_c, lab_c, zero_state, zero_state,
                                         batch, ctx_len)
        of_l, ob_l, _, _ = _gla_scan(gq_l, gk_l, gv_l, laf_l, lab_l, s_f, s_b, batch, seq)

        shared_out = (conv_w[l], gla_norm_g[l][None], g_post[l][None],
                      w_br_conv[l].astype(BF16), w_br_attn[l].astype(BF16),
                      w_br_gla[l].astype(BF16), w_out[l].astype(BF16))
        if not last:
            att_c = _attention(q_c, None, None, k_c, vt_c, batch, ctx_len, 0, ctx_len)
            xc = _merge(ua_c, wa_c, att_c, sza_c, of_c, ob_c, szg_c, gate_c, xc, mod3, batch, ctx_len,
                        *shared_out)
        xl = _merge(ua_l, wa_l, att_l, sza_l, of_l, ob_l, szg_l, gate_l, xl, mod3, None, seq,
                    *shared_out)
    return xl.reshape(batch, seq, D_MODEL)
```

```python
import functools

import jax
import jax.numpy as jnp
import numpy as np
from jax import lax
from jax.experimental import pallas as pl
from jax.experimental.pallas import tpu as pltpu

F32 = jnp.float32
BF16 = jnp.bfloat16

D_MODEL = 1024
DEPTH = 2
GRID_W = 64
CONV_W = 1024
N_HEADS = 8
N_KV_HEADS = 2
HEAD_DIM = 128
GROUP = N_HEADS // N_KV_HEADS
ROPE_THETA = 10000.0
ROPE_AXIS_DIM = HEAD_DIM // 2
ROPE_PAIRS = ROPE_AXIS_DIM // 2
ATTN_SCALE = HEAD_DIM ** -0.5
LOG2_E = 1.4426950408889634
GLA_HEADS = 4
GLA_DK = D_MODEL // 2
GLA_DV = D_MODEL
GLA_DKH = GLA_DK // GLA_HEADS
GLA_DVH = GLA_DV // GLA_HEADS
GLA_RANK = 16
GLA_TAU = 16.0
N_BRANCH = 3
EPS = 1e-6

Q_W = N_HEADS * HEAD_DIM
KV_W = N_KV_HEADS * HEAD_DIM

LANES = 128
SUBLANES = 8
BF16_ROWS = 16
VT_ROWS = HEAD_DIM + BF16_ROWS
VMEM_LIMIT = 56 * 1024 * 1024

OFF_A_B = 0
OFF_A_C = OFF_A_B + CONV_W
OFF_A_X = OFF_A_C + CONV_W
OFF_A_Z = OFF_A_X + CONV_W
OFF_Q = OFF_A_Z + CONV_W
OFF_K = OFF_Q + Q_W
OFF_V = OFF_K + KV_W
OFF_Z_ATT = OFF_V + KV_W
OFF_GQ = OFF_Z_ATT + Q_W
OFF_GK = OFF_GQ + GLA_DK
OFF_GV = OFF_GK + GLA_DK
OFF_Z_GLA = OFF_GV + GLA_DV
OFF_R = OFF_Z_GLA + GLA_DV
ORIG_R = OFF_Z_GLA
ORIG_MG = ORIG_R + 2 * GLA_RANK + GLA_DV

PROJ_TM = 512
MERGE_TM = 512
COL_BLK = 256
ATT_TQ = 512
ATT_KV = 512
GLA_C = 64
GLA_FINE = 4
GLA_ROWS = 512
GLA_LEVELS = tuple(GLA_C >> (i + 1) for i in range(int(np.log2(GLA_C // GLA_FINE))))
assert len(GLA_LEVELS) * GLA_C % LANES == 0 and 2 * GLA_C == LANES and GLA_FINE <= SUBLANES


def _sigmoid(x):
    return jax.nn.sigmoid(x)


def _silu(x):
    return x * _sigmoid(x)


def _dot(a, b):
    return jnp.dot(a, b, preferred_element_type=F32)


def _dot_nt(a, b):
    return lax.dot_general(a, b, (((1,), (1,)), ((), ())), preferred_element_type=F32)


def _dot_tn(a, b):
    return lax.dot_general(a, b, (((0,), (0,)), ((), ())), preferred_element_type=F32)


def _params(*sem):
    return pltpu.CompilerParams(dimension_semantics=sem, vmem_limit_bytes=VMEM_LIMIT)


def _mod_kernel(c_ref, w_ref, b_ref, o_ref):
    s = _silu(c_ref[...])
    o_ref[...] = _dot(s.astype(BF16), w_ref[...].astype(BF16)) + b_ref[...]


def _modulation(cvec, w_ada, b_ada):
    rows = cvec.shape[0]
    n_col = 3 * D_MODEL // D_MODEL
    return pl.pallas_call(
        _mod_kernel,
        out_shape=jax.ShapeDtypeStruct((DEPTH, rows, 3 * D_MODEL), F32),
        grid=(DEPTH, n_col),
        in_specs=[
            pl.BlockSpec((rows, D_MODEL), lambda l, j: (0, 0)),
            pl.BlockSpec((None, D_MODEL, D_MODEL), lambda l, j: (l, 0, j)),
            pl.BlockSpec((None, 1, D_MODEL), lambda l, j: (l, 0, j)),
        ],
        out_specs=pl.BlockSpec((None, rows, D_MODEL), lambda l, j: (l, 0, j)),
        compiler_params=_params("parallel", "parallel"),
        name="adaln_mod",
    )(cvec, w_ada, b_ada.reshape(DEPTH, 1, 3 * D_MODEL))


def _head_norm(xh, g):
    ms = jnp.mean(xh * xh, axis=-1, keepdims=True)
    return xh * lax.rsqrt(ms + EPS) * g


def _rope(xh, cos, sin):
    lane = lax.broadcasted_iota(jnp.int32, xh.shape, 1)
    first_half = (lane % ROPE_AXIS_DIM) < ROPE_PAIRS
    partner = jnp.where(first_half,
                        pltpu.roll(xh, HEAD_DIM - ROPE_PAIRS, 1),
                        pltpu.roll(xh, ROPE_PAIRS, 1))
    return xh * cos + partner * sin


def _log_sigmoid(x):
    return jnp.minimum(x, 0.0) - jnp.log(1.0 + jnp.exp(-jnp.abs(x)))


def _proj_kernel(x_ref, mod_ref, xn_ref, modn_ref, gpre_ref, w_lo_ref, w_hi_ref, w_rank_ref,
                 wdec_ref, bdec_ref, qg_ref, kg_ref, cos_ref, sin_ref,
                 ua_ref, wa_ref, q_ref, k_ref, vt_ref, sza_ref, gq_ref, gk_ref, gv_ref,
                 laf_ref, lab_ref, szg_ref, hout_ref, h_ref, v_ref, *, use_rope):
    step = pl.program_id(0)
    cur = step % 2

    def modulated_norm(xr, modr, slot):
        xv = xr[...]
        ms = jnp.mean(xv * xv, axis=-1, keepdims=True)
        y = xv * lax.rsqrt(ms + EPS) * gpre_ref[...]
        mod = modr[...]
        h_ref[slot] = (y * (1.0 + mod[:, D_MODEL:2 * D_MODEL]) + mod[:, :D_MODEL]).astype(BF16)

    @pl.when(step == 0)
    def _():
        modulated_norm(x_ref, mod_ref, 0)

    def proj(off, width=COL_BLK):
        for ref, base in ((w_rank_ref, OFF_R), (w_hi_ref, OFF_Z_GLA), (w_lo_ref, 0)):
            if off >= base:
                return _dot(h_ref[cur], ref[:, off - base:off - base + width])

    for o in range(0, CONV_W, COL_BLK):
        cols = slice(o, o + COL_BLK)
        ua_ref[:, cols] = (proj(OFF_A_C + o) * proj(OFF_A_X + o)).astype(BF16)
        wa_ref[:, cols] = (proj(OFF_A_B + o) * _silu(proj(OFF_A_Z + o))).astype(BF16)
        if o == 0:
            modulated_norm(xn_ref, modn_ref, 1 - cur)

    def heads(off, width, gain, out_ref, out_scale):
        for o in range(0, width, COL_BLK):
            blk = proj(off + o)
            for hh in range(COL_BLK // HEAD_DIM):
                xh = _head_norm(blk[:, hh * HEAD_DIM:(hh + 1) * HEAD_DIM], gain)
                if use_rope:
                    xh = _rope(xh, cos_ref[...], sin_ref[...])
                c0 = o + hh * HEAD_DIM
                out_ref[:, c0:c0 + HEAD_DIM] = (xh * out_scale).astype(BF16)

    heads(OFF_Q, Q_W, qg_ref[...], q_ref, ATTN_SCALE * LOG2_E)
    heads(OFF_K, KV_W, kg_ref[...], k_ref, 1.0)
    v_ref[...] = proj(OFF_V, KV_W)
    ones_tile = jnp.where(lax.broadcasted_iota(jnp.int32, (BF16_ROWS, x_ref.shape[0]), 0) == 0, 1.0, 0.0)
    for hh in range(N_KV_HEADS):
        vt_ref[hh * VT_ROWS:hh * VT_ROWS + HEAD_DIM, :] = (
            v_ref[:, hh * HEAD_DIM:(hh + 1) * HEAD_DIM].T.astype(BF16))
        vt_ref[hh * VT_ROWS + HEAD_DIM:(hh + 1) * VT_ROWS, :] = ones_tile.astype(BF16)
    for o in range(0, Q_W, COL_BLK):
        sza_ref[:, o:o + COL_BLK] = _silu(proj(OFF_Z_ATT + o)).astype(BF16)

    for o in range(0, GLA_DK, COL_BLK):
        gq_ref[:, o:o + COL_BLK] = (proj(OFF_GQ + o) * (GLA_DKH ** -0.5)).astype(BF16)
        gk_ref[:, o:o + COL_BLK] = proj(OFF_GK + o).astype(BF16)
    for o in range(0, GLA_DV, COL_BLK):
        szg_ref[:, o:o + COL_BLK] = _silu(proj(OFF_Z_GLA + o)).astype(BF16)

    r = proj(OFF_R, LANES).astype(BF16)

    def decay_piece(out_ref, o, wcol):
        out_ref[:, o:o + COL_BLK] = _log_sigmoid(
            _dot(r, wdec_ref[:, wcol:wcol + COL_BLK]) + bdec_ref[:, wcol:wcol + COL_BLK]
        ) * (LOG2_E / GLA_TAU)

    decay_pieces = ([(laf_ref, o, o) for o in range(0, GLA_DK, COL_BLK)]
                    + [(lab_ref, o, GLA_DK + o) for o in range(0, GLA_DK, COL_BLK)])
    assert len(decay_pieces) <= GLA_DV // COL_BLK
    for i, o in enumerate(range(0, GLA_DV, COL_BLK)):
        gv_ref[:, o:o + COL_BLK] = proj(OFF_GV + o).astype(BF16)
        if i < len(decay_pieces):
            decay_piece(*decay_pieces[i])
    hout_ref[...] = h_ref[cur]


def _projection(x2, mod3, mod_row0, seq_len, gpre, w_parts, wdec, bdec, qg, kg, cos_t, sin_t, use_rope):
    n = x2.shape[0]
    tm = min(PROJ_TM, seq_len)
    tps = seq_len // tm
    const = lambda i: (0, 0)
    row = lambda i: (i, 0)
    pos = lambda i: (i % tps, 0)
    nxt = lambda i: jnp.minimum(i + 1, n // tm - 1)
    if mod_row0 is None:
        mod_map = lambda i: (i // tps, 0, 0)
    else:
        mod_map = lambda i: (mod_row0, 0, 0)
    VT_OUT = 4
    widths = [(CONV_W, BF16), (CONV_W, BF16), (Q_W, BF16), (KV_W, BF16), (N_KV_HEADS * VT_ROWS, BF16), (Q_W, BF16),
              (GLA_DK, BF16), (GLA_DK, BF16), (GLA_DV, BF16), (GLA_DK, F32), (GLA_DK, F32),
              (GLA_DV, BF16), (D_MODEL, BF16)]
    return pl.pallas_call(
        functools.partial(_proj_kernel, use_rope=use_rope),
        out_shape=[jax.ShapeDtypeStruct((w, n) if i == VT_OUT else (n, w), dt)
                   for i, (w, dt) in enumerate(widths)],
        grid=(n // tm,),
        in_specs=[
            pl.BlockSpec((tm, D_MODEL), row),
            pl.BlockSpec((None, 1, 3 * D_MODEL), mod_map),
            pl.BlockSpec((tm, D_MODEL), lambda i: (nxt(i), 0)),
            pl.BlockSpec((None, 1, 3 * D_MODEL), lambda i: mod_map(nxt(i))),
            pl.BlockSpec((1, D_MODEL), const),
            pl.BlockSpec((D_MODEL, OFF_Z_GLA), const, pipeline_mode=pl.Buffered(1)),
            pl.BlockSpec((D_MODEL, OFF_R - OFF_Z_GLA), const, pipeline_mode=pl.Buffered(1)),
            pl.BlockSpec((D_MODEL, LANES), const, pipeline_mode=pl.Buffered(1)),
            pl.BlockSpec((LANES, 2 * GLA_DK), const),
            pl.BlockSpec((1, 2 * GLA_DK), const),
            pl.BlockSpec((1, HEAD_DIM), const),
            pl.BlockSpec((1, HEAD_DIM), const),
            pl.BlockSpec((tm, HEAD_DIM), pos),
            pl.BlockSpec((tm, HEAD_DIM), pos),
        ],
        out_specs=[pl.BlockSpec((w, tm), lambda i: (0, i)) if j == VT_OUT else pl.BlockSpec((tm, w), row)
                   for j, (w, _) in enumerate(widths)],
        scratch_shapes=[pltpu.VMEM((2, tm, D_MODEL), BF16), pltpu.VMEM((tm, KV_W), F32)],
        compiler_params=_params("arbitrary"),
        name="in_proj_rope" if use_rope else "in_proj",
    )(x2, mod3, x2, mod3, gpre, *w_parts, wdec, bdec, qg, kg, cos_t, sin_t)


def _attn_kernel(*refs, n_lat_blocks):
    if n_lat_blocks:
        (q_ref, kl_ref, vtl_ref, kc_ref, vtc_ref, o_ref, m_ref, alpha_ref, mc_ref, acc_ref,
         sc_ref, pc_ref, ma_ref, mb_ref, sa_ref, sb_ref, pa_ref, pb_ref) = refs
    else:
        q_ref, kc_ref, vtc_ref, o_ref, m_ref, alpha_ref, mc_ref, acc_ref, sc_ref, pc_ref = refs
    tq = q_ref.shape[0]
    m_ref[...] = jnp.full(m_ref.shape, -jnp.inf, F32)
    acc_ref[...] = jnp.zeros(acc_ref.shape, F32)

    def step(scores=None, softmax=None, values=None):
        heads = [slice(g * tq, (g + 1) * tq) for g in range(GROUP)]
        if scores is not None:
            k, s_dst, smax_dst = scores
            for g, cols in enumerate(heads):
                s = _dot_nt(k, q_ref[:, g * HEAD_DIM:(g + 1) * HEAD_DIM])
                s_dst[:, cols] = s
                smax_dst[:, cols] = jnp.max(s, axis=0, keepdims=True)
        if values is not None:
            p_src, vt = values
            for cols in heads:
                acc_ref[:, cols] = alpha_ref[:, cols] * acc_ref[:, cols] + _dot(vt, p_src[:, cols])
        if softmax is not None:
            s_src, smax_src, p_dst = softmax
            for cols in heads:
                m_old = m_ref[:, cols]
                m_new = jnp.maximum(m_old, smax_src[:, cols])
                alpha = jnp.exp2(m_old - m_new)
                p_dst[:, cols] = jnp.exp2(s_src[:, cols] - m_new).astype(BF16)
                alpha_ref[:, cols] = alpha
                m_ref[:, cols] = m_new

    def keys(j):
        return pl.ds(pl.multiple_of(j * ATT_KV, ATT_KV), ATT_KV)

    if n_lat_blocks:
        buf_a, buf_b, buf_c = (sa_ref, ma_ref), (sb_ref, mb_ref), (sc_ref, mc_ref)

        def pair(i, carry):
            t = 2 * i
            step(scores=(kl_ref[keys(t), :], *buf_a), values=(pa_ref, vtl_ref[:, keys(t - 2)]),
                 softmax=(*buf_b, pb_ref))
            step(scores=(kl_ref[keys(t + 1), :], *buf_b), values=(pb_ref, vtl_ref[:, keys(t - 1)]),
                 softmax=(*buf_a, pa_ref))
            return carry

        n = n_lat_blocks
        step(scores=(kl_ref[keys(0), :], *buf_a))
        step(scores=(kl_ref[keys(1), :], *buf_b), softmax=(*buf_a, pa_ref))
        lax.fori_loop(1, n // 2, pair, 0)
        step(scores=(kc_ref[...], *buf_c), values=(pa_ref, vtl_ref[:, keys(n - 2)]),
             softmax=(*buf_b, pb_ref))
        step(values=(pb_ref, vtl_ref[:, keys(n - 1)]), softmax=(*buf_c, pc_ref))
    else:
        step(scores=(kc_ref[...], sc_ref, mc_ref))
        step(softmax=(sc_ref, mc_ref, pc_ref))
    step(values=(pc_ref, vtc_ref[...]))
    out_t = acc_ref[:HEAD_DIM, :] / acc_ref[HEAD_DIM:HEAD_DIM + 1, :]
    for g in range(GROUP):
        o_ref[:, g * HEAD_DIM:(g + 1) * HEAD_DIM] = out_t[:, g * tq:(g + 1) * tq].T.astype(BF16)


def _attention(q2, k_lat, vt_lat, k_ctx, vt_ctx, batch, q_len, lat_len, ctx_len):
    tq = min(ATT_TQ, q_len)
    tiles = q_len // tq
    qmap = lambda b, kv, i: (b * tiles + i, kv)
    kmap = lambda b, kv, i: (b, kv)
    vtmap = lambda b, kv, i: (kv, b)
    in_specs = [pl.BlockSpec((tq, GROUP * HEAD_DIM), qmap)]
    args = [q2]
    n_lat_blocks = 0
    if k_lat is not None:
        n_lat_blocks = lat_len // ATT_KV
        in_specs += [pl.BlockSpec((lat_len, HEAD_DIM), kmap), pl.BlockSpec((VT_ROWS, lat_len), vtmap)]
        args += [k_lat, vt_lat]
    in_specs += [pl.BlockSpec((ctx_len, HEAD_DIM), kmap), pl.BlockSpec((VT_ROWS, ctx_len), vtmap)]
    args += [k_ctx, vt_ctx]
    rows = GROUP * tq
    stat = pltpu.VMEM((1, rows), F32)
    scratch = [stat, stat, stat, pltpu.VMEM((VT_ROWS, rows), F32),
               pltpu.VMEM((ctx_len, rows), F32), pltpu.VMEM((ctx_len, rows), BF16)]
    if n_lat_blocks:
        assert n_lat_blocks % 2 == 0
        scratch += [stat, stat] + [pltpu.VMEM((ATT_KV, rows), F32)] * 2 + [pltpu.VMEM((ATT_KV, rows), BF16)] * 2
    return pl.pallas_call(
        functools.partial(_attn_kernel, n_lat_blocks=n_lat_blocks),
        out_shape=jax.ShapeDtypeStruct(q2.shape, BF16),
        grid=(batch, N_KV_HEADS, tiles),
        in_specs=in_specs,
        out_specs=pl.BlockSpec((tq, GROUP * HEAD_DIM), qmap),
        scratch_shapes=scratch,
        compiler_params=_params("parallel", "parallel", "parallel"),
        name="gqa_lat" if n_lat_blocks else "gqa_ctx",
    )(*args)


def _split3(x):
    hi = x.astype(BF16)
    r1 = x - hi.astype(F32)
    mid = r1.astype(BF16)
    lo = (r1 - mid.astype(F32)).astype(BF16)
    return hi, mid, lo


def _gla_consts(rev):
    c = GLA_C
    tri_r = lax.broadcasted_iota(jnp.int32, (c, c), 0)
    tri_c = lax.broadcasted_iota(jnp.int32, (c, c), 1)
    tri = jnp.where((tri_c >= tri_r) if rev else (tri_c <= tri_r), 1.0, 0.0).astype(BF16)
    row = lax.broadcasted_iota(jnp.int32, (c, LANES), 0)
    lane = lax.broadcasted_iota(jnp.int32, (c, LANES), 1)
    level_masks = []
    for i, m in enumerate(GLA_LEVELS):
        key = lane - c * (i % 2)
        in_half = (key >= 0) & (key < c)
        same = (row // (2 * m)) == (key // (2 * m))
        q_upper = (row % (2 * m)) >= m
        k_upper = (key % (2 * m)) >= m
        pair = (~q_upper & k_upper) if rev else (q_upper & ~k_upper)
        level_masks.append(in_half & same & pair)
    sub = row % GLA_FINE
    d_of_lane = lane if rev else (LANES - lane) % LANES
    ok = (sub + d_of_lane < GLA_FINE) if rev else (sub >= d_of_lane)
    lane_code = jnp.where((d_of_lane < GLA_FINE) & ok, d_of_lane, -1)
    return tri, level_masks, lane_code


def _gla_kernel(qf_ref, kf_ref, vf_ref, laf_ref, qb_ref, kb_ref, vb_ref, lab_ref, s0f_ref, s0b_ref,
                of_ref, ob_ref, sff_ref, sfb_ref, st_ref):
    @pl.when(pl.program_id(1) == 0)
    def _():
        st_ref[0] = s0f_ref[...]
        st_ref[1] = s0b_ref[...]

    c = GLA_C
    n_chunks = qf_ref.shape[0] // c
    dirs = ((0, False, qf_ref, kf_ref, vf_ref, laf_ref, of_ref, _gla_consts(False)),
            (1, True, qb_ref, kb_ref, vb_ref, lab_ref, ob_ref, _gla_consts(True)))

    def body(ci, carry):
        chains = []
        for di, rev, q_ref, k_ref, v_ref, la_ref, o_ref, consts in dirs:
            cc = (n_chunks - 1 - ci) if rev else ci
            rows = pl.ds(pl.multiple_of(cc * c, c), c)
            cum3 = _dot(consts[0], jnp.concatenate(_split3(la_ref[rows, :]), axis=1))
            b_all = cum3[:, :GLA_DK] + cum3[:, GLA_DK:2 * GLA_DK] + cum3[:, 2 * GLA_DK:]
            for hh in range(GLA_HEADS):
                kc = slice(hh * GLA_DKH, (hh + 1) * GLA_DKH)
                vc = slice(hh * GLA_DVH, (hh + 1) * GLA_DVH)
                chains.append(dict(di=di, hh=hh, rev=rev, rows=rows, vc=vc, o_ref=o_ref,
                                   masks=consts[1], lane_code=consts[2], b=b_all[:, kc],
                                   qf=q_ref[rows, kc].astype(F32), kf=k_ref[rows, kc].astype(F32),
                                   v=v_ref[rows, vc]))

        for w in chains:
            b, qf, kf, rev = w["b"], w["qf"], w["kf"], w["rev"]
            st = st_ref[w["di"], w["hh"]]
            tot = b[0:1, :] if rev else b[c - 1:c, :]
            w["o"] = _dot_nt((qf * jnp.exp2(b)).astype(BF16), st.astype(BF16))
            kd = (kf * jnp.exp2(tot - b)).astype(BF16)
            st_ref[w["di"], w["hh"]] = st * jnp.exp2(tot) + _dot_tn(w["v"], kd)
            qs, ks = [], []
            for m in GLA_LEVELS:
                pivot = (m - 1) if rev else m
                bm = b.reshape(c // (2 * m), 2 * m, GLA_DKH)
                f = jnp.exp2(-jnp.abs(bm - bm[:, pivot:pivot + 1, :])).reshape(c, GLA_DKH)
                qs.append((qf * f).astype(BF16))
                ks.append((kf * f).astype(BF16))
            w["coarse"] = _dot_nt(jnp.concatenate(qs, axis=0), jnp.concatenate(ks, axis=0))

        grouped = (c // SUBLANES, SUBLANES, GLA_DKH)
        for w in chains:
            qf, kf, rev, lane_code = w["qf"], w["kf"], w["rev"], w["lane_code"]
            w_slots = jnp.where(lane_code == 0, jnp.sum(qf * kf, axis=-1, keepdims=True), 0.0)
            q3, k3, b3 = qf.reshape(grouped), kf.reshape(grouped), w["b"].reshape(grouped)
            for d in range(1, GLA_FINE):
                shift = (SUBLANES - d) if rev else d
                kr = pltpu.roll(k3, shift, 1)
                br = pltpu.roll(b3, shift, 1)
                wd = jnp.sum(q3 * kr * jnp.exp2(jnp.minimum(b3 - br, 0.0)), axis=-1, keepdims=True)
                w_slots = jnp.where(lane_code == d, wd.reshape(c, 1), w_slots)
            w["att"] = pltpu.roll(w_slots, 0, 1, stride=1, stride_axis=0)

        for w in chains:
            att = w["att"]
            for i, mask in enumerate(w["masks"]):
                col0 = (i * c) // LANES * LANES
                att = jnp.where(mask, w["coarse"][i * c:(i + 1) * c, col0:col0 + LANES], att)
            o = w["o"] + _dot(att.astype(BF16), jnp.concatenate([w["v"], w["v"]], axis=0))
            w["o_ref"][w["rows"], w["vc"]] = o.astype(BF16)
        return carry

    lax.fori_loop(0, n_chunks, body, 0)
    sff_ref[...] = st_ref[0]
    sfb_ref[...] = st_ref[1]


def _gla_scan(gq, gk, gv, la_f, la_b, s0_f, s0_b, batch, seq_len):
    rows = min(GLA_ROWS, seq_len)
    nblk = seq_len // rows
    fwd = lambda b, i: (b * nblk + i, 0)
    bwd = lambda b, i: (b * nblk + (nblk - 1 - i), 0)
    smap = lambda b, i: (b, 0, 0, 0)
    state_spec = pl.BlockSpec((None, GLA_HEADS, GLA_DVH, GLA_DKH), smap)
    state_shape = jax.ShapeDtypeStruct((batch, GLA_HEADS, GLA_DVH, GLA_DKH), F32)

    def views(index_map):
        return [pl.BlockSpec((rows, GLA_DK), index_map), pl.BlockSpec((rows, GLA_DK), index_map),
                pl.BlockSpec((rows, GLA_DV), index_map), pl.BlockSpec((rows, GLA_DK), index_map)]

    return pl.pallas_call(
        _gla_kernel,
        out_shape=[jax.ShapeDtypeStruct(gv.shape, BF16), jax.ShapeDtypeStruct(gv.shape, BF16),
                   state_shape, state_shape],
        grid=(batch, nblk),
        in_specs=views(fwd) + views(bwd) + [state_spec, state_spec],
        out_specs=[pl.BlockSpec((rows, GLA_DV), fwd), pl.BlockSpec((rows, GLA_DV), bwd),
                   state_spec, state_spec],
        scratch_shapes=[pltpu.VMEM((2, GLA_HEADS, GLA_DVH, GLA_DKH), F32)],
        compiler_params=_params("parallel", "arbitrary"),
        name="gla_bidir",
    )(gq, gk, gv, la_f, gq, gk, gv, la_b, s0_f, s0_b)


def _merge_kernel(ua_ref, uprev_ref, unext_ref, wa_ref, att_ref, sza_ref, of_ref, ob_ref, szg_ref,
                  h_ref, x_ref, mod_ref, convw_ref, glag_ref, gpost_ref, bgate_ref,
                  wconv_ref, watt_ref, wgla_ref, wout_ref, wgate_ref, o_ref, *, tiles_per_seq):
    tm = x_ref.shape[0]
    ti = pl.program_id(0) % tiles_per_seq
    br_b = _dot(att_ref[...] * sza_ref[...], watt_ref[...])
    u = ua_ref[...].astype(F32)
    prev_row = jnp.where(ti == 0, 0.0, uprev_ref[BF16_ROWS - 1:BF16_ROWS, :].astype(F32))
    next_row = jnp.where(ti == tiles_per_seq - 1, 0.0, unext_ref[0:1, :].astype(F32))
    rid = lax.broadcasted_iota(jnp.int32, (tm, 1), 0)
    u_prev = jnp.where(rid == 0, prev_row, pltpu.roll(u, 1, 0))
    u_next = jnp.where(rid == tm - 1, next_row, pltpu.roll(u, tm - 1, 0))
    cw = convw_ref[...]
    conv = cw[0:1, :] * u_prev + cw[1:2, :] * u + cw[2:3, :] * u_next
    br_a = _dot((wa_ref[...].astype(F32) * conv).astype(BF16), wconv_ref[...])
    parts = []
    for hh in range(GLA_HEADS):
        cols = slice(hh * GLA_DVH, (hh + 1) * GLA_DVH)
        oh = of_ref[:, cols].astype(F32) + ob_ref[:, cols].astype(F32)
        parts.append((_head_norm(oh, glag_ref[...]) * szg_ref[:, cols].astype(F32)).astype(BF16))
    br_c = _dot(jnp.concatenate(parts, axis=1), wgla_ref[...])
    def gate(i):
        cols = slice(i * D_MODEL, (i + 1) * D_MODEL)
        return _sigmoid(_dot(h_ref[...], wgate_ref[:, cols]) + bgate_ref[:, cols])

    merged = gate(0) * br_a + gate(1) * br_b + gate(2) * br_c
    out = _dot(merged.astype(BF16), wout_ref[...])
    gate = mod_ref[:, 2 * D_MODEL:]
    o_ref[...] = x_ref[...] + gate * _head_norm(out, gpost_ref[...])


def _merge(ua, wa, att, sza, o_f, o_b, szg, h, x2, mod3, mod_row0, seq_len,
           convw, glag, gpost, bgate, wconv, watt, wgla, wout, wgate):
    n = x2.shape[0]
    tm = min(MERGE_TM, seq_len)
    tps = seq_len // tm
    halo = tm // BF16_ROWS
    n_halo = n // BF16_ROWS
    const = lambda i: (0, 0)
    row = lambda i: (i, 0)
    if mod_row0 is None:
        mod_map = lambda i: (i // tps, 0, 0)
    else:
        mod_map = lambda i: (mod_row0, 0, 0)
    tok = lambda w: pl.BlockSpec((tm, w), row)
    wspec = pl.BlockSpec((D_MODEL, D_MODEL), const, pipeline_mode=pl.Buffered(1))
    return pl.pallas_call(
        functools.partial(_merge_kernel, tiles_per_seq=tps),
        out_shape=jax.ShapeDtypeStruct((n, D_MODEL), F32),
        grid=(n // tm,),
        in_specs=[
            tok(CONV_W),
            pl.BlockSpec((BF16_ROWS, CONV_W), lambda i: (jnp.maximum(i * halo - 1, 0), 0)),
            pl.BlockSpec((BF16_ROWS, CONV_W), lambda i: (jnp.minimum((i + 1) * halo, n_halo - 1), 0)),
            tok(CONV_W), tok(Q_W), tok(Q_W), tok(GLA_DV), tok(GLA_DV), tok(GLA_DV),
            tok(D_MODEL), tok(D_MODEL),
            pl.BlockSpec((None, 1, 3 * D_MODEL), mod_map),
            pl.BlockSpec((3, CONV_W), const),
            pl.BlockSpec((1, GLA_DVH), const),
            pl.BlockSpec((1, D_MODEL), const),
            pl.BlockSpec((1, N_BRANCH * D_MODEL), const),
            wspec, wspec, wspec, wspec,
            pl.BlockSpec((D_MODEL, N_BRANCH * D_MODEL), const, pipeline_mode=pl.Buffered(1)),
        ],
        out_specs=tok(D_MODEL),
        compiler_params=_params("parallel"),
        name="merge_out",
    )(ua, ua, ua, wa, att, sza, o_f, o_b, szg, h, x2, mod3, convw, glag, gpost, bgate,
      wconv, watt, wgla, wout, wgate)


def _rope_tables(n_tokens):
    n_rows = n_tokens // GRID_W
    row = np.repeat(np.arange(n_rows, dtype=np.float32), GRID_W)
    col = np.tile(np.arange(GRID_W, dtype=np.float32), n_rows)
    freqs = (np.float32(ROPE_THETA) ** (-np.arange(ROPE_PAIRS, dtype=np.float32) * np.float32(2.0)
                                        / np.float32(ROPE_AXIS_DIM))).astype(np.float32)
    ar, ac = row[:, None] * freqs, col[:, None] * freqs
    cos_t = np.concatenate([np.cos(ar), np.cos(ar), np.cos(ac), np.cos(ac)], axis=1)
    sin_t = np.concatenate([-np.sin(ar), np.sin(ar), -np.sin(ac), np.sin(ac)], axis=1)
    return jnp.asarray(cos_t, F32), jnp.asarray(sin_t, F32)


def _split_w_in(w):
    tail = ORIG_R + 2 * GLA_RANK
    low_rank = jnp.pad(w[:, ORIG_R:tail], ((0, 0), (0, LANES - 2 * GLA_RANK)))
    return (w[:, :ORIG_R].astype(BF16), w[:, tail:ORIG_MG].astype(BF16), low_rank.astype(BF16),
            w[:, ORIG_MG:].astype(BF16))


def _pack_decay(w_f, b_f, w_b, b_b):
    wd = jnp.zeros((LANES, 2 * GLA_DK), F32)
    wd = wd.at[:GLA_RANK, :GLA_DK].set(w_f).at[GLA_RANK:2 * GLA_RANK, GLA_DK:].set(w_b)
    return wd.astype(BF16), jnp.concatenate([b_f, b_b])[None, :]


def kernel(x, c, ctx, c_ctx, w_ada, b_ada, g_pre, g_post, w_in, conv_w, q_norm_g, k_norm_g,
           w_decay_fwd, b_decay_fwd, w_decay_bwd, b_decay_bwd, gla_norm_g,
           w_br_conv, w_br_attn, w_br_gla, b_gate, w_out):
    batch, seq, _ = x.shape
    ctx_len = ctx.shape[1]
    assert seq % max(PROJ_TM, MERGE_TM, ATT_TQ, ATT_KV, GLA_ROWS) == 0 and seq % GRID_W == 0
    assert ctx_len % GLA_C == 0 and ctx_len % BF16_ROWS == 0
    assert PROJ_TM % ctx_len == 0 and MERGE_TM % ctx_len == 0 and ATT_TQ % ctx_len == 0

    mod_rows = -(-(batch + 1) // SUBLANES) * SUBLANES
    cvec = jnp.zeros((mod_rows, D_MODEL), F32).at[:batch].set(c).at[batch].set(c_ctx)
    mod = _modulation(cvec, w_ada, b_ada)
    cos_t, sin_t = _rope_tables(seq)
    zero_state = jnp.zeros((batch, GLA_HEADS, GLA_DVH, GLA_DKH), F32)

    xl = x.reshape(batch * seq, D_MODEL)
    xc = ctx.reshape(batch * ctx_len, D_MODEL)
    for l in range(DEPTH):
        last = l == DEPTH - 1
        mod3 = mod[l][:, None, :]
        *w_parts, w_gate = _split_w_in(w_in[l])
        wdec, bdec = _pack_decay(w_decay_fwd[l], b_decay_fwd[l], w_decay_bwd[l], b_decay_bwd[l])
        shared_in = (g_pre[l][None], w_parts, wdec, bdec, q_norm_g[l][None], k_norm_g[l][None])
        pc = _projection(xc, mod3, batch, ctx_len, *shared_in, cos_t, sin_t, use_rope=False)
        pl_ = _projection(xl, mod3, None, seq, *shared_in, cos_t, sin_t, use_rope=True)
        (ua_c, wa_c, q_c, k_c, vt_c, sza_c, gq_c, gk_c, gv_c, laf_c, lab_c, szg_c, h_c) = pc
        (ua_l, wa_l, q_l, k_l, vt_l, sza_l, gq_l, gk_l, gv_l, laf_l, lab_l, szg_l, h_l) = pl_

        att_l = _attention(q_l, k_l, vt_l, k_c, vt_c, batch, seq, seq, ctx_len)
        of_c, ob_c, s_f, s_b = _gla_scan(gq_c, gk_c, gv_c, laf_c, lab_c, zero_state, zero_state,
                                         batch, ctx_len)
        of_l, ob_l, _, _ = _gla_scan(gq_l, gk_l, gv_l, laf_l, lab_l, s_f, s_b, batch, seq)

        shared_out = (conv_w[l], gla_norm_g[l][None], g_post[l][None], b_gate[l][None],
                      w_br_conv[l].astype(BF16), w_br_attn[l].astype(BF16),
                      w_br_gla[l].astype(BF16), w_out[l].astype(BF16), w_gate)
        if not last:
            att_c = _attention(q_c, None, None, k_c, vt_c, batch, ctx_len, 0, ctx_len)
            xc = _merge(ua_c, wa_c, att_c, sza_c, of_c, ob_c, szg_c, h_c, xc, mod3, batch, ctx_len,
                        *shared_out)
        xl = _merge(ua_l, wa_l, att_l, sza_l, of_l, ob_l, szg_l, h_l, xl, mod3, None, seq,
                    *shared_out)
    return xl.reshape(batch, seq, D_MODEL)
```

```python
import functools

import jax
import jax.numpy as jnp
import numpy as np
from jax import lax
from jax.experimental import pallas as pl
from jax.experimental.pallas import tpu as pltpu

F32 = jnp.float32
BF16 = jnp.bfloat16

D_MODEL = 1024
DEPTH = 2
GRID_W = 64
CONV_W = 1024
N_HEADS = 8
N_KV_HEADS = 2
HEAD_DIM = 128
GROUP = N_HEADS // N_KV_HEADS
ROPE_THETA = 10000.0
ROPE_AXIS_DIM = HEAD_DIM // 2
ROPE_PAIRS = ROPE_AXIS_DIM // 2
ATTN_SCALE = HEAD_DIM ** -0.5
LOG2_E = 1.4426950408889634
GLA_HEADS = 4
GLA_DK = D_MODEL // 2
GLA_DV = D_MODEL
GLA_DKH = GLA_DK // GLA_HEADS
GLA_DVH = GLA_DV // GLA_HEADS
GLA_RANK = 16
GLA_TAU = 16.0
N_BRANCH = 3
EPS = 1e-6

Q_W = N_HEADS * HEAD_DIM
KV_W = N_KV_HEADS * HEAD_DIM

LANES = 128
SUBLANES = 8
BF16_ROWS = 16
VT_ROWS = HEAD_DIM + BF16_ROWS
VMEM_LIMIT = 56 * 1024 * 1024

OFF_A_B = 0
OFF_A_C = OFF_A_B + CONV_W
OFF_A_X = OFF_A_C + CONV_W
OFF_A_Z = OFF_A_X + CONV_W
OFF_Q = OFF_A_Z + CONV_W
OFF_K = OFF_Q + Q_W
OFF_V = OFF_K + KV_W
OFF_Z_ATT = OFF_V + KV_W
OFF_GQ = OFF_Z_ATT + Q_W
OFF_GK = OFF_GQ + GLA_DK
OFF_GV = OFF_GK + GLA_DK
OFF_Z_GLA = OFF_GV + GLA_DV
OFF_R = OFF_Z_GLA + GLA_DV
ORIG_R = OFF_Z_GLA
ORIG_MG = ORIG_R + 2 * GLA_RANK + GLA_DV

PROJ_TM = 512
MERGE_TM = 512
COL_BLK = 256
ATT_TQ = 512
ATT_KV = 512
GLA_C = 64
GLA_FINE = 4
GLA_ROWS = 512
GLA_LEVELS = tuple(GLA_C >> (i + 1) for i in range(int(np.log2(GLA_C // GLA_FINE))))
assert len(GLA_LEVELS) * GLA_C % LANES == 0 and 2 * GLA_C == LANES and GLA_FINE <= SUBLANES


def _sigmoid(x):
    return jax.nn.sigmoid(x)


def _silu(x):
    return x * _sigmoid(x)


def _dot(a, b):
    return jnp.dot(a, b, preferred_element_type=F32)


def _dot_nt(a, b):
    return lax.dot_general(a, b, (((1,), (1,)), ((), ())), preferred_element_type=F32)


def _dot_tn(a, b):
    return lax.dot_general(a, b, (((0,), (0,)), ((), ())), preferred_element_type=F32)


def _params(*sem):
    return pltpu.CompilerParams(dimension_semantics=sem, vmem_limit_bytes=VMEM_LIMIT)


def _mod_kernel(c_ref, w_ref, b_ref, o_ref):
    s = _silu(c_ref[...])
    o_ref[...] = _dot(s.astype(BF16), w_ref[...].astype(BF16)) + b_ref[...]


def _modulation(cvec, w_ada, b_ada):
    rows = cvec.shape[0]
    n_col = 3 * D_MODEL // D_MODEL
    return pl.pallas_call(
        _mod_kernel,
        out_shape=jax.ShapeDtypeStruct((DEPTH, rows, 3 * D_MODEL), F32),
        grid=(DEPTH, n_col),
        in_specs=[
            pl.BlockSpec((rows, D_MODEL), lambda l, j: (0, 0)),
            pl.BlockSpec((None, D_MODEL, D_MODEL), lambda l, j: (l, 0, j)),
            pl.BlockSpec((None, 1, D_MODEL), lambda l, j: (l, 0, j)),
        ],
        out_specs=pl.BlockSpec((None, rows, D_MODEL), lambda l, j: (l, 0, j)),
        compiler_params=_params("parallel", "parallel"),
        name="adaln_mod",
    )(cvec, w_ada, b_ada.reshape(DEPTH, 1, 3 * D_MODEL))


def _head_norm(xh, g):
    ms = jnp.mean(xh * xh, axis=-1, keepdims=True)
    return xh * lax.rsqrt(ms + EPS) * g


def _rope(xh, cos, sin):
    lane = lax.broadcasted_iota(jnp.int32, xh.shape, 1)
    first_half = (lane % ROPE_AXIS_DIM) < ROPE_PAIRS
    partner = jnp.where(first_half,
                        pltpu.roll(xh, HEAD_DIM - ROPE_PAIRS, 1),
                        pltpu.roll(xh, ROPE_PAIRS, 1))
    return xh * cos + partner * sin


def _log_sigmoid(x):
    return jnp.minimum(x, 0.0) - jnp.log(1.0 + jnp.exp(-jnp.abs(x)))


def _proj_kernel(x_ref, mod_ref, xn_ref, modn_ref, gpre_ref, w_lo_ref, w_hi_ref, w_rank_ref,
                 wdec_ref, bdec_ref, qg_ref, kg_ref, cos_ref, sin_ref,
                 ua_ref, wa_ref, q_ref, k_ref, vt_ref, sza_ref, gq_ref, gk_ref, gv_ref,
                 laf_ref, lab_ref, szg_ref, hout_ref, h_ref, v_ref, *, use_rope):
    step = pl.program_id(0)
    cur = step % 2

    def modulated_norm(xr, modr, slot):
        xv = xr[...]
        ms = jnp.mean(xv * xv, axis=-1, keepdims=True)
        y = xv * lax.rsqrt(ms + EPS) * gpre_ref[...]
        mod = modr[...]
        h_ref[slot] = (y * (1.0 + mod[:, D_MODEL:2 * D_MODEL]) + mod[:, :D_MODEL]).astype(BF16)

    @pl.when(step == 0)
    def _():
        modulated_norm(x_ref, mod_ref, 0)

    def proj(off, width=COL_BLK):
        for ref, base in ((w_rank_ref, OFF_R), (w_hi_ref, OFF_Z_GLA), (w_lo_ref, 0)):
            if off >= base:
                return _dot(h_ref[cur], ref[:, off - base:off - base + width])

    for o in range(0, CONV_W, COL_BLK):
        cols = slice(o, o + COL_BLK)
        ua_ref[:, cols] = (proj(OFF_A_C + o) * proj(OFF_A_X + o)).astype(BF16)
        wa_ref[:, cols] = (proj(OFF_A_B + o) * _silu(proj(OFF_A_Z + o))).astype(BF16)
        if o == 0:
            modulated_norm(xn_ref, modn_ref, 1 - cur)

    def heads(off, width, gain, out_ref, out_scale):
        for o in range(0, width, COL_BLK):
            blk = proj(off + o)
            for hh in range(COL_BLK // HEAD_DIM):
                xh = _head_norm(blk[:, hh * HEAD_DIM:(hh + 1) * HEAD_DIM], gain)
                if use_rope:
                    xh = _rope(xh, cos_ref[...], sin_ref[...])
                c0 = o + hh * HEAD_DIM
                out_ref[:, c0:c0 + HEAD_DIM] = (xh * out_scale).astype(BF16)

    heads(OFF_Q, Q_W, qg_ref[...], q_ref, ATTN_SCALE * LOG2_E)
    heads(OFF_K, KV_W, kg_ref[...], k_ref, 1.0)
    v_ref[...] = proj(OFF_V, KV_W)
    ones_tile = jnp.where(lax.broadcasted_iota(jnp.int32, (BF16_ROWS, x_ref.shape[0]), 0) == 0, 1.0, 0.0)
    for hh in range(N_KV_HEADS):
        vt_ref[hh * VT_ROWS:hh * VT_ROWS + HEAD_DIM, :] = (
            v_ref[:, hh * HEAD_DIM:(hh + 1) * HEAD_DIM].T.astype(BF16))
        vt_ref[hh * VT_ROWS + HEAD_DIM:(hh + 1) * VT_ROWS, :] = ones_tile.astype(BF16)
    for o in range(0, Q_W, COL_BLK):
        sza_ref[:, o:o + COL_BLK] = _silu(proj(OFF_Z_ATT + o)).astype(BF16)

    for o in range(0, GLA_DK, COL_BLK):
        gq_ref[:, o:o + COL_BLK] = (proj(OFF_GQ + o) * (GLA_DKH ** -0.5)).astype(BF16)
        gk_ref[:, o:o + COL_BLK] = proj(OFF_GK + o).astype(BF16)
    for o in range(0, GLA_DV, COL_BLK):
        szg_ref[:, o:o + COL_BLK] = _silu(proj(OFF_Z_GLA + o)).astype(BF16)

    r = proj(OFF_R, LANES).astype(BF16)

    def decay_piece(out_ref, o, wcol):
        out_ref[:, o:o + COL_BLK] = _log_sigmoid(
            _dot(r, wdec_ref[:, wcol:wcol + COL_BLK]) + bdec_ref[:, wcol:wcol + COL_BLK]
        ) * (LOG2_E / GLA_TAU)

    decay_pieces = ([(laf_ref, o, o) for o in range(0, GLA_DK, COL_BLK)]
                    + [(lab_ref, o, GLA_DK + o) for o in range(0, GLA_DK, COL_BLK)])
    assert len(decay_pieces) <= GLA_DV // COL_BLK
    for i, o in enumerate(range(0, GLA_DV, COL_BLK)):
        gv_ref[:, o:o + COL_BLK] = proj(OFF_GV + o).astype(BF16)
        if i < len(decay_pieces):
            decay_piece(*decay_pieces[i])
    hout_ref[...] = h_ref[cur]


def _projection(x2, mod3, mod_row0, seq_len, gpre, w_parts, wdec, bdec, qg, kg, cos_t, sin_t, use_rope):
    n = x2.shape[0]
    tm = min(PROJ_TM, seq_len)
    tps = seq_len // tm
    const = lambda i: (0, 0)
    row = lambda i: (i, 0)
    pos = lambda i: (i % tps, 0)
    nxt = lambda i: jnp.minimum(i + 1, n // tm - 1)
    if mod_row0 is None:
        mod_map = lambda i: (i // tps, 0, 0)
    else:
        mod_map = lambda i: (mod_row0, 0, 0)
    VT_OUT = 4
    widths = [(CONV_W, BF16), (CONV_W, BF16), (Q_W, BF16), (KV_W, BF16), (N_KV_HEADS * VT_ROWS, BF16), (Q_W, BF16),
              (GLA_DK, BF16), (GLA_DK, BF16), (GLA_DV, BF16), (GLA_DK, F32), (GLA_DK, F32),
              (GLA_DV, BF16), (D_MODEL, BF16)]
    return pl.pallas_call(
        functools.partial(_proj_kernel, use_rope=use_rope),
        out_shape=[jax.ShapeDtypeStruct((w, n) if i == VT_OUT else (n, w), dt)
                   for i, (w, dt) in enumerate(widths)],
        grid=(n // tm,),
        in_specs=[
            pl.BlockSpec((tm, D_MODEL), row),
            pl.BlockSpec((None, 1, 3 * D_MODEL), mod_map),
            pl.BlockSpec((tm, D_MODEL), lambda i: (nxt(i), 0)),
            pl.BlockSpec((None, 1, 3 * D_MODEL), lambda i: mod_map(nxt(i))),
            pl.BlockSpec((1, D_MODEL), const),
            pl.BlockSpec((D_MODEL, OFF_Z_GLA), const, pipeline_mode=pl.Buffered(1)),
            pl.BlockSpec((D_MODEL, OFF_R - OFF_Z_GLA), const, pipeline_mode=pl.Buffered(1)),
            pl.BlockSpec((D_MODEL, LANES), const, pipeline_mode=pl.Buffered(1)),
            pl.BlockSpec((LANES, 2 * GLA_DK), const),
            pl.BlockSpec((1, 2 * GLA_DK), const),
            pl.BlockSpec((1, HEAD_DIM), const),
            pl.BlockSpec((1, HEAD_DIM), const),
            pl.BlockSpec((tm, HEAD_DIM), pos),
            pl.BlockSpec((tm, HEAD_DIM), pos),
        ],
        out_specs=[pl.BlockSpec((w, tm), lambda i: (0, i)) if j == VT_OUT else pl.BlockSpec((tm, w), row)
                   for j, (w, _) in enumerate(widths)],
        scratch_shapes=[pltpu.VMEM((2, tm, D_MODEL), BF16), pltpu.VMEM((tm, KV_W), F32)],
        compiler_params=_params("arbitrary"),
        name="in_proj_rope" if use_rope else "in_proj",
    )(x2, mod3, x2, mod3, gpre, *w_parts, wdec, bdec, qg, kg, cos_t, sin_t)


def _attn_stages(tq, m_ref, alpha_ref, acc_ref):
    heads = [slice(g * tq, (g + 1) * tq) for g in range(GROUP)]

    def step(scores=(), values=(), softmax=()):
        for q_src, k, s_dst, smax_dst in scores:
            for g, cols in enumerate(heads):
                s = _dot_nt(k, q_src[:, g * HEAD_DIM:(g + 1) * HEAD_DIM])
                s_dst[:, cols] = s
                smax_dst[:, cols] = jnp.max(s, axis=0, keepdims=True)
        for p_src, vt, slot in values:
            for cols in heads:
                acc_ref[slot, :, cols] = (alpha_ref[slot, :, cols] * acc_ref[slot, :, cols]
                                          + _dot(vt, p_src[:, cols]))
        for s_src, smax_src, p_dst, slot in softmax:
            for cols in heads:
                m_old = m_ref[slot, :, cols]
                m_new = jnp.maximum(m_old, smax_src[:, cols])
                p_dst[:, cols] = jnp.exp2(s_src[:, cols] - m_new).astype(BF16)
                alpha_ref[slot, :, cols] = jnp.exp2(m_old - m_new)
                m_ref[slot, :, cols] = m_new

    def init(slot):
        m_ref[slot] = jnp.full(m_ref.shape[1:], -jnp.inf, F32)
        acc_ref[slot] = jnp.zeros(acc_ref.shape[1:], F32)

    def finish(slot, o_ref):
        out_t = acc_ref[slot, :HEAD_DIM, :] / acc_ref[slot, HEAD_DIM:HEAD_DIM + 1, :]
        for g in range(GROUP):
            o_ref[:, g * HEAD_DIM:(g + 1) * HEAD_DIM] = out_t[:, g * tq:(g + 1) * tq].T.astype(BF16)

    return step, init, finish


def _attn_ctx_kernel(q_ref, kc_ref, vtc_ref, o_ref, m_ref, alpha_ref, acc_ref, mc_ref, sc_ref, pc_ref):
    step, init, finish = _attn_stages(q_ref.shape[0], m_ref, alpha_ref, acc_ref)
    init(0)
    step(scores=[(q_ref, kc_ref[...], sc_ref, mc_ref)])
    step(softmax=[(sc_ref, mc_ref, pc_ref, 0)])
    step(values=[(pc_ref, vtc_ref[...], 0)])
    finish(0, o_ref)


def _attn_kernel(q_ref, qn_ref, kl_ref, vtl_ref, kc_ref, vtc_ref, o_ref, m_ref, alpha_ref, acc_ref,
                 mc_ref, sc_ref, pc_ref, ma_ref, mb_ref, sa_ref, sb_ref, pa_ref, pb_ref, *, n_lat_blocks):
    tile = pl.program_id(2)
    cur = tile % 2
    nxt = 1 - cur
    n = n_lat_blocks
    step, init, finish = _attn_stages(q_ref.shape[0], m_ref, alpha_ref, acc_ref)
    buf_a, buf_b, buf_c = (sa_ref, ma_ref), (sb_ref, mb_ref), (sc_ref, mc_ref)

    def keys(j):
        return pl.ds(pl.multiple_of(j * ATT_KV, ATT_KV), ATT_KV)

    @pl.when(tile == 0)
    def _():
        init(cur)
        step(scores=[(q_ref, kl_ref[keys(0), :], *buf_a)])
        step(scores=[(q_ref, kl_ref[keys(1), :], *buf_b)], softmax=[(*buf_a, pa_ref, cur)])

    def pair(i, carry):
        t = 2 * i
        step(scores=[(q_ref, kl_ref[keys(t), :], *buf_a)], values=[(pa_ref, vtl_ref[:, keys(t - 2)], cur)],
             softmax=[(*buf_b, pb_ref, cur)])
        step(scores=[(q_ref, kl_ref[keys(t + 1), :], *buf_b)], values=[(pb_ref, vtl_ref[:, keys(t - 1)], cur)],
             softmax=[(*buf_a, pa_ref, cur)])
        return carry

    lax.fori_loop(1, n // 2, pair, 0)
    init(nxt)
    step(scores=[(q_ref, kc_ref[...], *buf_c), (qn_ref, kl_ref[keys(0), :], *buf_a)],
         values=[(pa_ref, vtl_ref[:, keys(n - 2)], cur)], softmax=[(*buf_b, pb_ref, cur)])
    step(scores=[(qn_ref, kl_ref[keys(1), :], *buf_b)],
         values=[(pb_ref, vtl_ref[:, keys(n - 1)], cur)],
         softmax=[(*buf_c, pc_ref, cur), (*buf_a, pa_ref, nxt)])
    step(values=[(pc_ref, vtc_ref[...], cur)])
    finish(cur, o_ref)


def _attention(q2, k_lat, vt_lat, k_ctx, vt_ctx, batch, q_len, lat_len, ctx_len):
    tq = min(ATT_TQ, q_len)
    tiles = q_len // tq
    rows = GROUP * tq
    qmap = lambda b, kv, i: (b * tiles + i, kv)
    qnext = lambda b, kv, i: (b * tiles + jnp.minimum(i + 1, tiles - 1), kv)
    kmap = lambda b, kv, i: (b, kv)
    vtmap = lambda b, kv, i: (kv, b)
    q_spec = pl.BlockSpec((tq, GROUP * HEAD_DIM), qmap)
    ctx_specs = [pl.BlockSpec((ctx_len, HEAD_DIM), kmap), pl.BlockSpec((VT_ROWS, ctx_len), vtmap)]
    stat = pltpu.VMEM((1, rows), F32)
    if k_lat is None:
        slots = 1
        body, name = _attn_ctx_kernel, "gqa_ctx"
        in_specs, args = [q_spec] + ctx_specs, [q2, k_ctx, vt_ctx]
        buffers = []
        semantics = ("parallel", "parallel", "parallel")
    else:
        slots = 2
        n_lat_blocks = lat_len // ATT_KV
        assert n_lat_blocks % 2 == 0 and n_lat_blocks >= 2
        body, name = functools.partial(_attn_kernel, n_lat_blocks=n_lat_blocks), "gqa_lat"
        in_specs = ([q_spec, pl.BlockSpec((tq, GROUP * HEAD_DIM), qnext),
                     pl.BlockSpec((lat_len, HEAD_DIM), kmap), pl.BlockSpec((VT_ROWS, lat_len), vtmap)]
                    + ctx_specs)
        args = [q2, q2, k_lat, vt_lat, k_ctx, vt_ctx]
        buffers = [stat, stat] + [pltpu.VMEM((ATT_KV, rows), F32)] * 2 + [pltpu.VMEM((ATT_KV, rows), BF16)] * 2
        semantics = ("parallel", "parallel", "arbitrary")
    scratch = [pltpu.VMEM((slots, 1, rows), F32), pltpu.VMEM((slots, 1, rows), F32),
               pltpu.VMEM((slots, VT_ROWS, rows), F32), stat,
               pltpu.VMEM((ctx_len, rows), F32), pltpu.VMEM((ctx_len, rows), BF16)] + buffers
    return pl.pallas_call(
        body,
        out_shape=jax.ShapeDtypeStruct(q2.shape, BF16),
        grid=(batch, N_KV_HEADS, tiles),
        in_specs=in_specs,
        out_specs=q_spec,
        scratch_shapes=scratch,
        compiler_params=_params(*semantics),
        name=name,
    )(*args)


def _split3(x):
    hi = x.astype(BF16)
    r1 = x - hi.astype(F32)
    mid = r1.astype(BF16)
    lo = (r1 - mid.astype(F32)).astype(BF16)
    return hi, mid, lo


def _gla_consts(rev):
    c = GLA_C
    tri_r = lax.broadcasted_iota(jnp.int32, (c, c), 0)
    tri_c = lax.broadcasted_iota(jnp.int32, (c, c), 1)
    tri = jnp.where((tri_c >= tri_r) if rev else (tri_c <= tri_r), 1.0, 0.0).astype(BF16)
    row = lax.broadcasted_iota(jnp.int32, (c, LANES), 0)
    lane = lax.broadcasted_iota(jnp.int32, (c, LANES), 1)
    level_masks = []
    for i, m in enumerate(GLA_LEVELS):
        key = lane - c * (i % 2)
        in_half = (key >= 0) & (key < c)
        same = (row // (2 * m)) == (key // (2 * m))
        q_upper = (row % (2 * m)) >= m
        k_upper = (key % (2 * m)) >= m
        pair = (~q_upper & k_upper) if rev else (q_upper & ~k_upper)
        level_masks.append(in_half & same & pair)
    sub = row % GLA_FINE
    d_of_lane = lane if rev else (LANES - lane) % LANES
    ok = (sub + d_of_lane < GLA_FINE) if rev else (sub >= d_of_lane)
    lane_code = jnp.where((d_of_lane < GLA_FINE) & ok, d_of_lane, -1)
    return tri, level_masks, lane_code


def _gla_kernel(qf_ref, kf_ref, vf_ref, laf_ref, qb_ref, kb_ref, vb_ref, lab_ref, s0f_ref, s0b_ref,
                of_ref, ob_ref, sff_ref, sfb_ref, st_ref):
    @pl.when(pl.program_id(1) == 0)
    def _():
        st_ref[0] = s0f_ref[...]
        st_ref[1] = s0b_ref[...]

    c = GLA_C
    n_chunks = qf_ref.shape[0] // c
    dirs = ((0, False, qf_ref, kf_ref, vf_ref, laf_ref, of_ref, _gla_consts(False)),
            (1, True, qb_ref, kb_ref, vb_ref, lab_ref, ob_ref, _gla_consts(True)))

    def body(ci, carry):
        chains = []
        for di, rev, q_ref, k_ref, v_ref, la_ref, o_ref, consts in dirs:
            cc = (n_chunks - 1 - ci) if rev else ci
            rows = pl.ds(pl.multiple_of(cc * c, c), c)
            cum3 = _dot(consts[0], jnp.concatenate(_split3(la_ref[rows, :]), axis=1))
            b_all = cum3[:, :GLA_DK] + cum3[:, GLA_DK:2 * GLA_DK] + cum3[:, 2 * GLA_DK:]
            for hh in range(GLA_HEADS):
                kc = slice(hh * GLA_DKH, (hh + 1) * GLA_DKH)
                vc = slice(hh * GLA_DVH, (hh + 1) * GLA_DVH)
                chains.append(dict(di=di, hh=hh, rev=rev, rows=rows, vc=vc, o_ref=o_ref,
                                   masks=consts[1], lane_code=consts[2], b=b_all[:, kc],
                                   qf=q_ref[rows, kc].astype(F32), kf=k_ref[rows, kc].astype(F32),
                                   v=v_ref[rows, vc]))

        for w in chains:
            b, qf, kf, rev = w["b"], w["qf"], w["kf"], w["rev"]
            st = st_ref[w["di"], w["hh"]]
            tot = b[0:1, :] if rev else b[c - 1:c, :]
            w["o"] = _dot_nt((qf * jnp.exp2(b)).astype(BF16), st.astype(BF16))
            kd = (kf * jnp.exp2(tot - b)).astype(BF16)
            st_ref[w["di"], w["hh"]] = st * jnp.exp2(tot) + _dot_tn(w["v"], kd)
            qs, ks = [], []
            for m in GLA_LEVELS:
                pivot = (m - 1) if rev else m
                bm = b.reshape(c // (2 * m), 2 * m, GLA_DKH)
                f = jnp.exp2(-jnp.abs(bm - bm[:, pivot:pivot + 1, :])).reshape(c, GLA_DKH)
                qs.append((qf * f).astype(BF16))
                ks.append((kf * f).astype(BF16))
            w["coarse"] = _dot_nt(jnp.concatenate(qs, axis=0), jnp.concatenate(ks, axis=0))

        grouped = (c // SUBLANES, SUBLANES, GLA_DKH)
        for w in chains:
            qf, kf, rev, lane_code = w["qf"], w["kf"], w["rev"], w["lane_code"]
            w_slots = jnp.where(lane_code == 0, jnp.sum(qf * kf, axis=-1, keepdims=True), 0.0)
            q3, k3, b3 = qf.reshape(grouped), kf.reshape(grouped), w["b"].reshape(grouped)
            for d in range(1, GLA_FINE):
                shift = (SUBLANES - d) if rev else d
                kr = pltpu.roll(k3, shift, 1)
                br = pltpu.roll(b3, shift, 1)
                wd = jnp.sum(q3 * kr * jnp.exp2(jnp.minimum(b3 - br, 0.0)), axis=-1, keepdims=True)
                w_slots = jnp.where(lane_code == d, wd.reshape(c, 1), w_slots)
            w["att"] = pltpu.roll(w_slots, 0, 1, stride=1, stride_axis=0)

        for w in chains:
            att = w["att"]
            for i, mask in enumerate(w["masks"]):
                col0 = (i * c) // LANES * LANES
                att = jnp.where(mask, w["coarse"][i * c:(i + 1) * c, col0:col0 + LANES], att)
            o = w["o"] + _dot(att.astype(BF16), jnp.concatenate([w["v"], w["v"]], axis=0))
            w["o_ref"][w["rows"], w["vc"]] = o.astype(BF16)
        return carry

    lax.fori_loop(0, n_chunks, body, 0)
    sff_ref[...] = st_ref[0]
    sfb_ref[...] = st_ref[1]


def _gla_scan(gq, gk, gv, la_f, la_b, s0_f, s0_b, batch, seq_len):
    rows = min(GLA_ROWS, seq_len)
    nblk = seq_len // rows
    fwd = lambda b, i: (b * nblk + i, 0)
    bwd = lambda b, i: (b * nblk + (nblk - 1 - i), 0)
    smap = lambda b, i: (b, 0, 0, 0)
    state_spec = pl.BlockSpec((None, GLA_HEADS, GLA_DVH, GLA_DKH), smap)
    state_shape = jax.ShapeDtypeStruct((batch, GLA_HEADS, GLA_DVH, GLA_DKH), F32)

    def views(index_map):
        return [pl.BlockSpec((rows, GLA_DK), index_map), pl.BlockSpec((rows, GLA_DK), index_map),
                pl.BlockSpec((rows, GLA_DV), index_map), pl.BlockSpec((rows, GLA_DK), index_map)]

    return pl.pallas_call(
        _gla_kernel,
        out_shape=[jax.ShapeDtypeStruct(gv.shape, BF16), jax.ShapeDtypeStruct(gv.shape, BF16),
                   state_shape, state_shape],
        grid=(batch, nblk),
        in_specs=views(fwd) + views(bwd) + [state_spec, state_spec],
        out_specs=[pl.BlockSpec((rows, GLA_DV), fwd), pl.BlockSpec((rows, GLA_DV), bwd),
                   state_spec, state_spec],
        scratch_shapes=[pltpu.VMEM((2, GLA_HEADS, GLA_DVH, GLA_DKH), F32)],
        compiler_params=_params("parallel", "arbitrary"),
        name="gla_bidir",
    )(gq, gk, gv, la_f, gq, gk, gv, la_b, s0_f, s0_b)


def _merge_kernel(ua_ref, uprev_ref, unext_ref, wa_ref, att_ref, sza_ref, of_ref, ob_ref, szg_ref,
                  h_ref, x_ref, mod_ref, convw_ref, glag_ref, gpost_ref, bgate_ref,
                  wconv_ref, watt_ref, wgla_ref, wout_ref, wgate_ref, o_ref, *, tiles_per_seq):
    tm = x_ref.shape[0]
    ti = pl.program_id(0) % tiles_per_seq
    br_b = _dot(att_ref[...] * sza_ref[...], watt_ref[...])
    u = ua_ref[...].astype(F32)
    prev_row = jnp.where(ti == 0, 0.0, uprev_ref[BF16_ROWS - 1:BF16_ROWS, :].astype(F32))
    next_row = jnp.where(ti == tiles_per_seq - 1, 0.0, unext_ref[0:1, :].astype(F32))
    rid = lax.broadcasted_iota(jnp.int32, (tm, 1), 0)
    u_prev = jnp.where(rid == 0, prev_row, pltpu.roll(u, 1, 0))
    u_next = jnp.where(rid == tm - 1, next_row, pltpu.roll(u, tm - 1, 0))
    cw = convw_ref[...]
    conv = cw[0:1, :] * u_prev + cw[1:2, :] * u + cw[2:3, :] * u_next
    br_a = _dot((wa_ref[...].astype(F32) * conv).astype(BF16), wconv_ref[...])
    parts = []
    for hh in range(GLA_HEADS):
        cols = slice(hh * GLA_DVH, (hh + 1) * GLA_DVH)
        oh = of_ref[:, cols].astype(F32) + ob_ref[:, cols].astype(F32)
        parts.append((_head_norm(oh, glag_ref[...]) * szg_ref[:, cols].astype(F32)).astype(BF16))
    br_c = _dot(jnp.concatenate(parts, axis=1), wgla_ref[...])
    def gate(i):
        cols = slice(i * D_MODEL, (i + 1) * D_MODEL)
        return _sigmoid(_dot(h_ref[...], wgate_ref[:, cols]) + bgate_ref[:, cols])

    merged = gate(0) * br_a + gate(1) * br_b + gate(2) * br_c
    out = _dot(merged.astype(BF16), wout_ref[...])
    gate = mod_ref[:, 2 * D_MODEL:]
    o_ref[...] = x_ref[...] + gate * _head_norm(out, gpost_ref[...])


def _merge(ua, wa, att, sza, o_f, o_b, szg, h, x2, mod3, mod_row0, seq_len,
           convw, glag, gpost, bgate, wconv, watt, wgla, wout, wgate):
    n = x2.shape[0]
    tm = min(MERGE_TM, seq_len)
    tps = seq_len // tm
    halo = tm // BF16_ROWS
    n_halo = n // BF16_ROWS
    const = lambda i: (0, 0)
    row = lambda i: (i, 0)
    if mod_row0 is None:
        mod_map = lambda i: (i // tps, 0, 0)
    else:
        mod_map = lambda i: (mod_row0, 0, 0)
    tok = lambda w: pl.BlockSpec((tm, w), row)
    wspec = pl.BlockSpec((D_MODEL, D_MODEL), const, pipeline_mode=pl.Buffered(1))
    return pl.pallas_call(
        functools.partial(_merge_kernel, tiles_per_seq=tps),
        out_shape=jax.ShapeDtypeStruct((n, D_MODEL), F32),
        grid=(n // tm,),
        in_specs=[
            tok(CONV_W),
            pl.BlockSpec((BF16_ROWS, CONV_W), lambda i: (jnp.maximum(i * halo - 1, 0), 0)),
            pl.BlockSpec((BF16_ROWS, CONV_W), lambda i: (jnp.minimum((i + 1) * halo, n_halo - 1), 0)),
            tok(CONV_W), tok(Q_W), tok(Q_W), tok(GLA_DV), tok(GLA_DV), tok(GLA_DV),
            tok(D_MODEL), tok(D_MODEL),
            pl.BlockSpec((None, 1, 3 * D_MODEL), mod_map),
            pl.BlockSpec((3, CONV_W), const),
            pl.BlockSpec((1, GLA_DVH), const),
            pl.BlockSpec((1, D_MODEL), const),
            pl.BlockSpec((1, N_BRANCH * D_MODEL), const),
            wspec, wspec, wspec, wspec,
            pl.BlockSpec((D_MODEL, N_BRANCH * D_MODEL), const, pipeline_mode=pl.Buffered(1)),
        ],
        out_specs=tok(D_MODEL),
        compiler_params=_params("parallel"),
        name="merge_out",
    )(ua, ua, ua, wa, att, sza, o_f, o_b, szg, h, x2, mod3, convw, glag, gpost, bgate,
      wconv, watt, wgla, wout, wgate)


def _rope_tables(n_tokens):
    n_rows = n_tokens // GRID_W
    row = np.repeat(np.arange(n_rows, dtype=np.float32), GRID_W)
    col = np.tile(np.arange(GRID_W, dtype=np.float32), n_rows)
    freqs = (np.float32(ROPE_THETA) ** (-np.arange(ROPE_PAIRS, dtype=np.float32) * np.float32(2.0)
                                        / np.float32(ROPE_AXIS_DIM))).astype(np.float32)
    ar, ac = row[:, None] * freqs, col[:, None] * freqs
    cos_t = np.concatenate([np.cos(ar), np.cos(ar), np.cos(ac), np.cos(ac)], axis=1)
    sin_t = np.concatenate([-np.sin(ar), np.sin(ar), -np.sin(ac), np.sin(ac)], axis=1)
    return jnp.asarray(cos_t, F32), jnp.asarray(sin_t, F32)


def _split_w_in(w):
    tail = ORIG_R + 2 * GLA_RANK
    low_rank = jnp.pad(w[:, ORIG_R:tail], ((0, 0), (0, LANES - 2 * GLA_RANK)))
    return (w[:, :ORIG_R].astype(BF16), w[:, tail:ORIG_MG].astype(BF16), low_rank.astype(BF16),
            w[:, ORIG_MG:].astype(BF16))


def _pack_decay(w_f, b_f, w_b, b_b):
    wd = jnp.zeros((LANES, 2 * GLA_DK), F32)
    wd = wd.at[:GLA_RANK, :GLA_DK].set(w_f).at[GLA_RANK:2 * GLA_RANK, GLA_DK:].set(w_b)
    return wd.astype(BF16), jnp.concatenate([b_f, b_b])[None, :]


def kernel(x, c, ctx, c_ctx, w_ada, b_ada, g_pre, g_post, w_in, conv_w, q_norm_g, k_norm_g,
           w_decay_fwd, b_decay_fwd, w_decay_bwd, b_decay_bwd, gla_norm_g,
           w_br_conv, w_br_attn, w_br_gla, b_gate, w_out):
    batch, seq, _ = x.shape
    ctx_len = ctx.shape[1]
    assert seq % max(PROJ_TM, MERGE_TM, ATT_TQ, ATT_KV, GLA_ROWS) == 0 and seq % GRID_W == 0
    assert ctx_len % GLA_C == 0 and ctx_len % BF16_ROWS == 0
    assert PROJ_TM % ctx_len == 0 and MERGE_TM % ctx_len == 0 and ATT_TQ % ctx_len == 0

    mod_rows = -(-(batch + 1) // SUBLANES) * SUBLANES
    cvec = jnp.zeros((mod_rows, D_MODEL), F32).at[:batch].set(c).at[batch].set(c_ctx)
    mod = _modulation(cvec, w_ada, b_ada)
    cos_t, sin_t = _rope_tables(seq)
    zero_state = jnp.zeros((batch, GLA_HEADS, GLA_DVH, GLA_DKH), F32)

    xl = x.reshape(batch * seq, D_MODEL)
    xc = ctx.reshape(batch * ctx_len, D_MODEL)
    for l in range(DEPTH):
        last = l == DEPTH - 1
        mod3 = mod[l][:, None, :]
        *w_parts, w_gate = _split_w_in(w_in[l])
        wdec, bdec = _pack_decay(w_decay_fwd[l], b_decay_fwd[l], w_decay_bwd[l], b_decay_bwd[l])
        shared_in = (g_pre[l][None], w_parts, wdec, bdec, q_norm_g[l][None], k_norm_g[l][None])
        pc = _projection(xc, mod3, batch, ctx_len, *shared_in, cos_t, sin_t, use_rope=False)
        pl_ = _projection(xl, mod3, None, seq, *shared_in, cos_t, sin_t, use_rope=True)
        (ua_c, wa_c, q_c, k_c, vt_c, sza_c, gq_c, gk_c, gv_c, laf_c, lab_c, szg_c, h_c) = pc
        (ua_l, wa_l, q_l, k_l, vt_l, sza_l, gq_l, gk_l, gv_l, laf_l, lab_l, szg_l, h_l) = pl_

        att_l = _attention(q_l, k_l, vt_l, k_c, vt_c, batch, seq, seq, ctx_len)
        of_c, ob_c, s_f, s_b = _gla_scan(gq_c, gk_c, gv_c, laf_c, lab_c, zero_state, zero_state,
                                         batch, ctx_len)
        of_l, ob_l, _, _ = _gla_scan(gq_l, gk_l, gv_l, laf_l, lab_l, s_f, s_b, batch, seq)

        shared_out = (conv_w[l], gla_norm_g[l][None], g_post[l][None], b_gate[l][None],
                      w_br_conv[l].astype(BF16), w_br_attn[l].astype(BF16),
                      w_br_gla[l].astype(BF16), w_out[l].astype(BF16), w_gate)
        if not last:
            att_c = _attention(q_c, None, None, k_c, vt_c, batch, ctx_len, 0, ctx_len)
            xc = _merge(ua_c, wa_c, att_c, sza_c, of_c, ob_c, szg_c, h_c, xc, mod3, batch, ctx_len,
                        *shared_out)
        xl = _merge(ua_l, wa_l, att_l, sza_l, of_l, ob_l, szg_l, h_l, xl, mod3, None, seq,
                    *shared_out)
    return xl.reshape(batch, seq, D_MODEL)
```

```python
import functools

import jax
import jax.numpy as jnp
import numpy as np
from jax import lax
from jax.experimental import pallas as pl
from jax.experimental.pallas import tpu as pltpu

F32 = jnp.float32
BF16 = jnp.bfloat16

D_MODEL = 1024
DEPTH = 2
GRID_W = 64
CONV_W = 1024
N_HEADS = 8
N_KV_HEADS = 2
HEAD_DIM = 128
GROUP = N_HEADS // N_KV_HEADS
ROPE_THETA = 10000.0
ROPE_AXIS_DIM = HEAD_DIM // 2
ROPE_PAIRS = ROPE_AXIS_DIM // 2
ATTN_SCALE = HEAD_DIM ** -0.5
LOG2_E = 1.4426950408889634
GLA_HEADS = 4
GLA_DK = D_MODEL // 2
GLA_DV = D_MODEL
GLA_DKH = GLA_DK // GLA_HEADS
GLA_DVH = GLA_DV // GLA_HEADS
GLA_RANK = 16
GLA_TAU = 16.0
N_BRANCH = 3
EPS = 1e-6

Q_W = N_HEADS * HEAD_DIM
KV_W = N_KV_HEADS * HEAD_DIM

LANES = 128
SUBLANES = 8
BF16_ROWS = 16
VT_ROWS = HEAD_DIM + BF16_ROWS
VMEM_LIMIT = 56 * 1024 * 1024

OFF_A_B = 0
OFF_A_C = OFF_A_B + CONV_W
OFF_A_X = OFF_A_C + CONV_W
OFF_A_Z = OFF_A_X + CONV_W
OFF_Q = OFF_A_Z + CONV_W
OFF_K = OFF_Q + Q_W
OFF_V = OFF_K + KV_W
OFF_Z_ATT = OFF_V + KV_W
OFF_GQ = OFF_Z_ATT + Q_W
OFF_GK = OFF_GQ + GLA_DK
OFF_GV = OFF_GK + GLA_DK
OFF_Z_GLA = OFF_GV + GLA_DV
OFF_R = OFF_Z_GLA + GLA_DV
ORIG_R = OFF_Z_GLA
ORIG_MG = ORIG_R + 2 * GLA_RANK + GLA_DV

PROJ_TM = 512
MERGE_TM = 512
COL_BLK = 256
ATT_TQ = 512
ATT_KV = 512
GLA_C = 64
GLA_FINE = 4
GLA_ROWS = 1024
GLA_LEVELS = tuple(GLA_C >> (i + 1) for i in range(int(np.log2(GLA_C // GLA_FINE))))
assert len(GLA_LEVELS) * GLA_C % LANES == 0 and 2 * GLA_C == LANES and GLA_FINE <= SUBLANES


def _sigmoid(x):
    return jax.nn.sigmoid(x)


def _silu(x):
    return x * _sigmoid(x)


def _dot(a, b):
    return jnp.dot(a, b, preferred_element_type=F32)


def _dot_nt(a, b):
    return lax.dot_general(a, b, (((1,), (1,)), ((), ())), preferred_element_type=F32)


def _dot_tn(a, b):
    return lax.dot_general(a, b, (((0,), (0,)), ((), ())), preferred_element_type=F32)


def _params(*sem):
    return pltpu.CompilerParams(dimension_semantics=sem, vmem_limit_bytes=VMEM_LIMIT)


def _mod_kernel(c_ref, w_ref, b_ref, o_ref):
    s = _silu(c_ref[...])
    o_ref[...] = _dot(s.astype(BF16), w_ref[...].astype(BF16)) + b_ref[...]


def _modulation(cvec, w_ada, b_ada):
    rows = cvec.shape[0]
    n_col = 3 * D_MODEL // D_MODEL
    return pl.pallas_call(
        _mod_kernel,
        out_shape=jax.ShapeDtypeStruct((DEPTH, rows, 3 * D_MODEL), F32),
        grid=(DEPTH, n_col),
        in_specs=[
            pl.BlockSpec((rows, D_MODEL), lambda l, j: (0, 0)),
            pl.BlockSpec((None, D_MODEL, D_MODEL), lambda l, j: (l, 0, j)),
            pl.BlockSpec((None, 1, D_MODEL), lambda l, j: (l, 0, j)),
        ],
        out_specs=pl.BlockSpec((None, rows, D_MODEL), lambda l, j: (l, 0, j)),
        compiler_params=_params("parallel", "parallel"),
        name="adaln_mod",
    )(cvec, w_ada, b_ada.reshape(DEPTH, 1, 3 * D_MODEL))


def _head_norm(xh, g):
    ms = jnp.mean(xh * xh, axis=-1, keepdims=True)
    return xh * lax.rsqrt(ms + EPS) * g


def _rope(xh, cos, sin):
    lane = lax.broadcasted_iota(jnp.int32, xh.shape, 1)
    first_half = (lane % ROPE_AXIS_DIM) < ROPE_PAIRS
    partner = jnp.where(first_half,
                        pltpu.roll(xh, HEAD_DIM - ROPE_PAIRS, 1),
                        pltpu.roll(xh, ROPE_PAIRS, 1))
    return xh * cos + partner * sin


def _log_sigmoid(x):
    return jnp.minimum(x, 0.0) - jnp.log(1.0 + jnp.exp(-jnp.abs(x)))


PROJ_OUTS = (("ua", CONV_W, BF16), ("wa", CONV_W, BF16), ("q", Q_W, BF16), ("k", KV_W, BF16),
             ("vt", N_KV_HEADS * VT_ROWS, BF16), ("sza", Q_W, BF16), ("gq", GLA_DK, BF16),
             ("gk", GLA_DK, BF16), ("gv", GLA_DV, BF16), ("laf", GLA_DK, F32), ("lab", GLA_DK, F32),
             ("szg", GLA_DV, BF16), ("h", D_MODEL, BF16))
PROJ_STATE_OUTS = tuple(o for o in PROJ_OUTS if o[0] in ("k", "vt", "gk", "gv", "laf", "lab"))


def _proj_kernel(x_ref, mod_ref, xn_ref, modn_ref, gpre_ref, w_lo_ref, w_hi_ref, w_rank_ref,
                 wdec_ref, bdec_ref, qg_ref, kg_ref, cos_ref, sin_ref, *rest, use_rope, outs):
    out = dict(zip((name for name, _, _ in outs), rest))
    h_ref, v_ref = rest[len(outs):]
    step = pl.program_id(0)
    cur = step % 2

    def modulated_norm(xr, modr, slot):
        xv = xr[...]
        ms = jnp.mean(xv * xv, axis=-1, keepdims=True)
        y = xv * lax.rsqrt(ms + EPS) * gpre_ref[...]
        mod = modr[...]
        h_ref[slot] = (y * (1.0 + mod[:, D_MODEL:2 * D_MODEL]) + mod[:, :D_MODEL]).astype(BF16)

    @pl.when(step == 0)
    def _():
        modulated_norm(x_ref, mod_ref, 0)

    def proj(off, width=COL_BLK):
        for ref, base in ((w_rank_ref, OFF_R), (w_hi_ref, OFF_Z_GLA), (w_lo_ref, 0)):
            if off >= base:
                return _dot(h_ref[cur], ref[:, off - base:off - base + width])

    for o in range(0, CONV_W if "ua" in out else 0, COL_BLK):
        cols = slice(o, o + COL_BLK)
        out["ua"][:, cols] = (proj(OFF_A_C + o) * proj(OFF_A_X + o)).astype(BF16)
        out["wa"][:, cols] = (proj(OFF_A_B + o) * _silu(proj(OFF_A_Z + o))).astype(BF16)
        if o == 0:
            modulated_norm(xn_ref, modn_ref, 1 - cur)

    def heads(off, width, gain, out_ref, out_scale):
        for o in range(0, width, COL_BLK):
            blk = proj(off + o)
            for hh in range(COL_BLK // HEAD_DIM):
                xh = _head_norm(blk[:, hh * HEAD_DIM:(hh + 1) * HEAD_DIM], gain)
                if use_rope:
                    xh = _rope(xh, cos_ref[...], sin_ref[...])
                c0 = o + hh * HEAD_DIM
                out_ref[:, c0:c0 + HEAD_DIM] = (xh * out_scale).astype(BF16)

    if "q" in out:
        heads(OFF_Q, Q_W, qg_ref[...], out["q"], ATTN_SCALE * LOG2_E)
    heads(OFF_K, KV_W, kg_ref[...], out["k"], 1.0)
    if "ua" not in out:
        modulated_norm(xn_ref, modn_ref, 1 - cur)
    v_ref[...] = proj(OFF_V, KV_W)
    ones_tile = jnp.where(lax.broadcasted_iota(jnp.int32, (BF16_ROWS, x_ref.shape[0]), 0) == 0, 1.0, 0.0)
    for hh in range(N_KV_HEADS):
        out["vt"][hh * VT_ROWS:hh * VT_ROWS + HEAD_DIM, :] = (
            v_ref[:, hh * HEAD_DIM:(hh + 1) * HEAD_DIM].T.astype(BF16))
        out["vt"][hh * VT_ROWS + HEAD_DIM:(hh + 1) * VT_ROWS, :] = ones_tile.astype(BF16)
    for o in range(0, Q_W if "sza" in out else 0, COL_BLK):
        out["sza"][:, o:o + COL_BLK] = _silu(proj(OFF_Z_ATT + o)).astype(BF16)

    for o in range(0, GLA_DK, COL_BLK):
        if "gq" in out:
            out["gq"][:, o:o + COL_BLK] = (proj(OFF_GQ + o) * (GLA_DKH ** -0.5)).astype(BF16)
        out["gk"][:, o:o + COL_BLK] = proj(OFF_GK + o).astype(BF16)
    for o in range(0, GLA_DV if "szg" in out else 0, COL_BLK):
        out["szg"][:, o:o + COL_BLK] = _silu(proj(OFF_Z_GLA + o)).astype(BF16)

    r = proj(OFF_R, LANES).astype(BF16)

    def decay_piece(out_ref, o, wcol):
        out_ref[:, o:o + COL_BLK] = _log_sigmoid(
            _dot(r, wdec_ref[:, wcol:wcol + COL_BLK]) + bdec_ref[:, wcol:wcol + COL_BLK]
        ) * (LOG2_E / GLA_TAU)

    decay_pieces = ([(out["laf"], o, o) for o in range(0, GLA_DK, COL_BLK)]
                    + [(out["lab"], o, GLA_DK + o) for o in range(0, GLA_DK, COL_BLK)])
    assert len(decay_pieces) <= GLA_DV // COL_BLK
    for i, o in enumerate(range(0, GLA_DV, COL_BLK)):
        out["gv"][:, o:o + COL_BLK] = proj(OFF_GV + o).astype(BF16)
        if i < len(decay_pieces):
            decay_piece(*decay_pieces[i])
    if "h" in out:
        out["h"][...] = h_ref[cur]


def _projection(x2, mod3, mod_row0, seq_len, gpre, w_parts, wdec, bdec, qg, kg, cos_t, sin_t, use_rope,
                outs=PROJ_OUTS):
    n = x2.shape[0]
    tm = min(PROJ_TM, seq_len)
    tps = seq_len // tm
    const = lambda i: (0, 0)
    row = lambda i: (i, 0)
    pos = lambda i: (i % tps, 0)
    nxt = lambda i: jnp.minimum(i + 1, n // tm - 1)
    if mod_row0 is None:
        mod_map = lambda i: (i // tps, 0, 0)
    else:
        mod_map = lambda i: (mod_row0, 0, 0)
    transposed = ("vt",)
    return pl.pallas_call(
        functools.partial(_proj_kernel, use_rope=use_rope, outs=outs),
        out_shape=[jax.ShapeDtypeStruct((w, n) if name in transposed else (n, w), dt)
                   for name, w, dt in outs],
        grid=(n // tm,),
        in_specs=[
            pl.BlockSpec((tm, D_MODEL), row),
            pl.BlockSpec((None, 1, 3 * D_MODEL), mod_map),
            pl.BlockSpec((tm, D_MODEL), lambda i: (nxt(i), 0)),
            pl.BlockSpec((None, 1, 3 * D_MODEL), lambda i: mod_map(nxt(i))),
            pl.BlockSpec((1, D_MODEL), const),
            pl.BlockSpec((D_MODEL, OFF_Z_GLA), const, pipeline_mode=pl.Buffered(1)),
            pl.BlockSpec((D_MODEL, OFF_R - OFF_Z_GLA), const, pipeline_mode=pl.Buffered(1)),
            pl.BlockSpec((D_MODEL, LANES), const, pipeline_mode=pl.Buffered(1)),
            pl.BlockSpec((LANES, 2 * GLA_DK), const),
            pl.BlockSpec((1, 2 * GLA_DK), const),
            pl.BlockSpec((1, HEAD_DIM), const),
            pl.BlockSpec((1, HEAD_DIM), const),
            pl.BlockSpec((tm, HEAD_DIM), pos),
            pl.BlockSpec((tm, HEAD_DIM), pos),
        ],
        out_specs=[pl.BlockSpec((w, tm), lambda i: (0, i)) if name in transposed
                   else pl.BlockSpec((tm, w), row) for name, w, _ in outs],
        scratch_shapes=[pltpu.VMEM((2, tm, D_MODEL), BF16), pltpu.VMEM((tm, KV_W), F32)],
        compiler_params=_params("arbitrary"),
        name="in_proj_rope" if use_rope else "in_proj",
    )(x2, mod3, x2, mod3, gpre, *w_parts, wdec, bdec, qg, kg, cos_t, sin_t)


def _attn_stages(tq, m_ref, alpha_ref, acc_ref):
    heads = [slice(g * tq, (g + 1) * tq) for g in range(GROUP)]

    def step(scores=(), values=(), softmax=()):
        for q_src, k, s_dst, smax_dst in scores:
            for g, cols in enumerate(heads):
                s = _dot_nt(k, q_src[:, g * HEAD_DIM:(g + 1) * HEAD_DIM])
                s_dst[:, cols] = s
                smax_dst[:, cols] = jnp.max(s, axis=0, keepdims=True)
        for p_src, vt, slot in values:
            for cols in heads:
                acc_ref[slot, :, cols] = (alpha_ref[slot, :, cols] * acc_ref[slot, :, cols]
                                          + _dot(vt, p_src[:, cols]))
        for s_src, smax_src, p_dst, slot in softmax:
            for cols in heads:
                m_old = m_ref[slot, :, cols]
                m_new = jnp.maximum(m_old, smax_src[:, cols])
                p_dst[:, cols] = jnp.exp2(s_src[:, cols] - m_new).astype(BF16)
                alpha_ref[slot, :, cols] = jnp.exp2(m_old - m_new)
                m_ref[slot, :, cols] = m_new

    def init(slot):
        m_ref[slot] = jnp.full(m_ref.shape[1:], -jnp.inf, F32)
        acc_ref[slot] = jnp.zeros(acc_ref.shape[1:], F32)

    def finish(slot, o_ref):
        out_t = acc_ref[slot, :HEAD_DIM, :] / acc_ref[slot, HEAD_DIM:HEAD_DIM + 1, :]
        for g in range(GROUP):
            o_ref[:, g * HEAD_DIM:(g + 1) * HEAD_DIM] = out_t[:, g * tq:(g + 1) * tq].T.astype(BF16)

    return step, init, finish


def _attn_ctx_kernel(q_ref, kc_ref, vtc_ref, o_ref, m_ref, alpha_ref, acc_ref, mc_ref, sc_ref, pc_ref):
    step, init, finish = _attn_stages(q_ref.shape[0], m_ref, alpha_ref, acc_ref)
    init(0)
    step(scores=[(q_ref, kc_ref[...], sc_ref, mc_ref)])
    step(softmax=[(sc_ref, mc_ref, pc_ref, 0)])
    step(values=[(pc_ref, vtc_ref[...], 0)])
    finish(0, o_ref)


def _attn_kernel(q_ref, qn_ref, kl_ref, vtl_ref, kc_ref, vtc_ref, o_ref, m_ref, alpha_ref, acc_ref,
                 mc_ref, sc_ref, pc_ref, ma_ref, mb_ref, sa_ref, sb_ref, pa_ref, pb_ref, *, n_lat_blocks):
    tile = pl.program_id(2)
    cur = tile % 2
    nxt = 1 - cur
    n = n_lat_blocks
    step, init, finish = _attn_stages(q_ref.shape[0], m_ref, alpha_ref, acc_ref)
    buf_a, buf_b, buf_c = (sa_ref, ma_ref), (sb_ref, mb_ref), (sc_ref, mc_ref)

    def keys(j):
        return pl.ds(pl.multiple_of(j * ATT_KV, ATT_KV), ATT_KV)

    @pl.when(tile == 0)
    def _():
        init(cur)
        step(scores=[(q_ref, kl_ref[keys(0), :], *buf_a)])
        step(scores=[(q_ref, kl_ref[keys(1), :], *buf_b)], softmax=[(*buf_a, pa_ref, cur)])

    def pair(i, carry):
        t = 2 * i
        step(scores=[(q_ref, kl_ref[keys(t), :], *buf_a)], values=[(pa_ref, vtl_ref[:, keys(t - 2)], cur)],
             softmax=[(*buf_b, pb_ref, cur)])
        step(scores=[(q_ref, kl_ref[keys(t + 1), :], *buf_b)], values=[(pb_ref, vtl_ref[:, keys(t - 1)], cur)],
             softmax=[(*buf_a, pa_ref, cur)])
        return carry

    lax.fori_loop(1, n // 2, pair, 0)
    init(nxt)
    step(scores=[(q_ref, kc_ref[...], *buf_c), (qn_ref, kl_ref[keys(0), :], *buf_a)],
         values=[(pa_ref, vtl_ref[:, keys(n - 2)], cur)], softmax=[(*buf_b, pb_ref, cur)])
    step(scores=[(qn_ref, kl_ref[keys(1), :], *buf_b)],
         values=[(pb_ref, vtl_ref[:, keys(n - 1)], cur)],
         softmax=[(*buf_c, pc_ref, cur), (*buf_a, pa_ref, nxt)])
    step(values=[(pc_ref, vtc_ref[...], cur)])
    finish(cur, o_ref)


def _attention(q2, k_lat, vt_lat, k_ctx, vt_ctx, batch, q_len, lat_len, ctx_len):
    tq = min(ATT_TQ, q_len)
    tiles = q_len // tq
    rows = GROUP * tq
    qmap = lambda b, kv, i: (b * tiles + i, kv)
    qnext = lambda b, kv, i: (b * tiles + jnp.minimum(i + 1, tiles - 1), kv)
    kmap = lambda b, kv, i: (b, kv)
    vtmap = lambda b, kv, i: (kv, b)
    q_spec = pl.BlockSpec((tq, GROUP * HEAD_DIM), qmap)
    ctx_specs = [pl.BlockSpec((ctx_len, HEAD_DIM), kmap), pl.BlockSpec((VT_ROWS, ctx_len), vtmap)]
    stat = pltpu.VMEM((1, rows), F32)
    if k_lat is None:
        slots = 1
        body, name = _attn_ctx_kernel, "gqa_ctx"
        in_specs, args = [q_spec] + ctx_specs, [q2, k_ctx, vt_ctx]
        buffers = []
        semantics = ("parallel", "parallel", "parallel")
    else:
        slots = 2
        n_lat_blocks = lat_len // ATT_KV
        assert n_lat_blocks % 2 == 0 and n_lat_blocks >= 2
        body, name = functools.partial(_attn_kernel, n_lat_blocks=n_lat_blocks), "gqa_lat"
        in_specs = ([q_spec, pl.BlockSpec((tq, GROUP * HEAD_DIM), qnext),
                     pl.BlockSpec((lat_len, HEAD_DIM), kmap), pl.BlockSpec((VT_ROWS, lat_len), vtmap)]
                    + ctx_specs)
        args = [q2, q2, k_lat, vt_lat, k_ctx, vt_ctx]
        buffers = [stat, stat] + [pltpu.VMEM((ATT_KV, rows), F32)] * 2 + [pltpu.VMEM((ATT_KV, rows), BF16)] * 2
        semantics = ("parallel", "parallel", "arbitrary")
    scratch = [pltpu.VMEM((slots, 1, rows), F32), pltpu.VMEM((slots, 1, rows), F32),
               pltpu.VMEM((slots, VT_ROWS, rows), F32), stat,
               pltpu.VMEM((ctx_len, rows), F32), pltpu.VMEM((ctx_len, rows), BF16)] + buffers
    return pl.pallas_call(
        body,
        out_shape=jax.ShapeDtypeStruct(q2.shape, BF16),
        grid=(batch, N_KV_HEADS, tiles),
        in_specs=in_specs,
        out_specs=q_spec,
        scratch_shapes=scratch,
        compiler_params=_params(*semantics),
        name=name,
    )(*args)


def _split2(x):
    hi = x.astype(BF16)
    lo = (x - hi.astype(F32)).astype(BF16)
    return hi, lo


def _gla_consts(rev):
    c = GLA_C
    tri_r = lax.broadcasted_iota(jnp.int32, (c, c), 0)
    tri_c = lax.broadcasted_iota(jnp.int32, (c, c), 1)
    tri = jnp.where((tri_c >= tri_r) if rev else (tri_c <= tri_r), 1.0, 0.0).astype(BF16)
    row = lax.broadcasted_iota(jnp.int32, (c, LANES), 0)
    lane = lax.broadcasted_iota(jnp.int32, (c, LANES), 1)
    level_masks = []
    for i, m in enumerate(GLA_LEVELS):
        key = lane - c * (i % 2)
        in_half = (key >= 0) & (key < c)
        same = (row // (2 * m)) == (key // (2 * m))
        q_upper = (row % (2 * m)) >= m
        k_upper = (key % (2 * m)) >= m
        pair = (~q_upper & k_upper) if rev else (q_upper & ~k_upper)
        level_masks.append(in_half & same & pair)
    sub = row % GLA_FINE
    d_of_lane = lane if rev else (LANES - lane) % LANES
    ok = (sub + d_of_lane < GLA_FINE) if rev else (sub >= d_of_lane)
    lane_code = jnp.where((d_of_lane < GLA_FINE) & ok, d_of_lane, -1)
    return tri, level_masks, lane_code


def _gla_kernel(*refs, with_outputs):
    if with_outputs:
        (qf_ref, kf_ref, vf_ref, laf_ref, qb_ref, kb_ref, vb_ref, lab_ref, s0f_ref, s0b_ref,
         of_ref, ob_ref, sff_ref, sfb_ref, st_ref) = refs
    else:
        kf_ref, vf_ref, laf_ref, kb_ref, vb_ref, lab_ref, s0f_ref, s0b_ref, sff_ref, sfb_ref, st_ref = refs
        qf_ref = qb_ref = of_ref = ob_ref = None

    @pl.when(pl.program_id(1) == 0)
    def _():
        st_ref[0] = s0f_ref[...]
        st_ref[1] = s0b_ref[...]

    c = GLA_C
    n_chunks = kf_ref.shape[0] // c
    dirs = ((0, False, qf_ref, kf_ref, vf_ref, laf_ref, of_ref, _gla_consts(False)),
            (1, True, qb_ref, kb_ref, vb_ref, lab_ref, ob_ref, _gla_consts(True)))

    def body(ci, carry):
        chains = []
        for di, rev, q_ref, k_ref, v_ref, la_ref, o_ref, consts in dirs:
            cc = (n_chunks - 1 - ci) if rev else ci
            rows = pl.ds(pl.multiple_of(cc * c, c), c)
            cum2 = _dot(consts[0], jnp.concatenate(_split2(la_ref[rows, :]), axis=1))
            b_all = cum2[:, :GLA_DK] + cum2[:, GLA_DK:]
            for hh in range(GLA_HEADS):
                kc = slice(hh * GLA_DKH, (hh + 1) * GLA_DKH)
                vc = slice(hh * GLA_DVH, (hh + 1) * GLA_DVH)
                chains.append(dict(di=di, hh=hh, rev=rev, rows=rows, vc=vc, o_ref=o_ref,
                                   masks=consts[1], lane_code=consts[2], b=b_all[:, kc],
                                   qf=q_ref[rows, kc].astype(F32) if with_outputs else None,
                                   kf=k_ref[rows, kc].astype(F32), v=v_ref[rows, vc]))

        for w in chains:
            b, qf, kf, rev = w["b"], w["qf"], w["kf"], w["rev"]
            st = st_ref[w["di"], w["hh"]]
            tot = b[0:1, :] if rev else b[c - 1:c, :]
            kd = (kf * jnp.exp2(tot - b)).astype(BF16)
            st_ref[w["di"], w["hh"]] = st * jnp.exp2(tot) + _dot_tn(w["v"], kd)
            if not with_outputs:
                continue
            w["o"] = _dot_nt((qf * jnp.exp2(b)).astype(BF16), st.astype(BF16))
            qs, ks = [], []
            for m in GLA_LEVELS:
                pivot = (m - 1) if rev else m
                bm = b.reshape(c // (2 * m), 2 * m, GLA_DKH)
                f = jnp.exp2(-jnp.abs(bm - bm[:, pivot:pivot + 1, :])).reshape(c, GLA_DKH)
                qs.append((qf * f).astype(BF16))
                ks.append((kf * f).astype(BF16))
            w["coarse"] = _dot_nt(jnp.concatenate(qs, axis=0), jnp.concatenate(ks, axis=0))

        grouped = (c // SUBLANES, SUBLANES, GLA_DKH)
        for w in chains if with_outputs else ():
            qf, kf, rev, lane_code = w["qf"], w["kf"], w["rev"], w["lane_code"]
            w_slots = jnp.where(lane_code == 0, jnp.sum(qf * kf, axis=-1, keepdims=True), 0.0)
            q3, k3, b3 = qf.reshape(grouped), kf.reshape(grouped), w["b"].reshape(grouped)
            for d in range(1, GLA_FINE):
                shift = (SUBLANES - d) if rev else d
                kr = pltpu.roll(k3, shift, 1)
                br = pltpu.roll(b3, shift, 1)
                wd = jnp.sum(q3 * kr * jnp.exp2(jnp.minimum(b3 - br, 0.0)), axis=-1, keepdims=True)
                w_slots = jnp.where(lane_code == d, wd.reshape(c, 1), w_slots)
            w["att"] = pltpu.roll(w_slots, 0, 1, stride=1, stride_axis=0)

        for w in chains if with_outputs else ():
            att = w["att"]
            for i, mask in enumerate(w["masks"]):
                col0 = (i * c) // LANES * LANES
                att = jnp.where(mask, w["coarse"][i * c:(i + 1) * c, col0:col0 + LANES], att)
            o = w["o"] + _dot(att.astype(BF16), jnp.concatenate([w["v"], w["v"]], axis=0))
            w["o_ref"][w["rows"], w["vc"]] = o.astype(BF16)
        return carry

    lax.fori_loop(0, n_chunks, body, 0)
    sff_ref[...] = st_ref[0]
    sfb_ref[...] = st_ref[1]


def _gla_scan(gq, gk, gv, la_f, la_b, s0_f, s0_b, batch, seq_len):
    rows = min(GLA_ROWS, seq_len)
    nblk = seq_len // rows
    fwd = lambda b, i: (b * nblk + i, 0)
    bwd = lambda b, i: (b * nblk + (nblk - 1 - i), 0)
    smap = lambda b, i: (b, 0, 0, 0)
    state_spec = pl.BlockSpec((None, GLA_HEADS, GLA_DVH, GLA_DKH), smap)
    state_shape = jax.ShapeDtypeStruct((batch, GLA_HEADS, GLA_DVH, GLA_DKH), F32)
    with_outputs = gq is not None

    def view(index_map, la):
        arrays = [(gq, GLA_DK)] * with_outputs + [(gk, GLA_DK), (gv, GLA_DV), (la, GLA_DK)]
        return [a for a, _ in arrays], [pl.BlockSpec((rows, w), index_map) for _, w in arrays]

    (args_f, specs_f), (args_b, specs_b) = view(fwd, la_f), view(bwd, la_b)
    o_shape = [jax.ShapeDtypeStruct(gv.shape, BF16)] * 2 if with_outputs else []
    o_specs = [pl.BlockSpec((rows, GLA_DV), fwd), pl.BlockSpec((rows, GLA_DV), bwd)] if with_outputs else []
    return pl.pallas_call(
        functools.partial(_gla_kernel, with_outputs=with_outputs),
        out_shape=o_shape + [state_shape, state_shape],
        grid=(batch, nblk),
        in_specs=specs_f + specs_b + [state_spec, state_spec],
        out_specs=o_specs + [state_spec, state_spec],
        scratch_shapes=[pltpu.VMEM((2, GLA_HEADS, GLA_DVH, GLA_DKH), F32)],
        compiler_params=_params("parallel", "arbitrary"),
        name="gla_bidir" if with_outputs else "gla_states",
    )(*args_f, *args_b, s0_f, s0_b)


def _merge_kernel(ua_ref, uprev_ref, unext_ref, wa_ref, att_ref, sza_ref, of_ref, ob_ref, szg_ref,
                  h_ref, x_ref, mod_ref, convw_ref, glag_ref, gpost_ref, bgate_ref,
                  wconv_ref, watt_ref, wgla_ref, wout_ref, wgate_ref, o_ref, *, tiles_per_seq):
    tm = x_ref.shape[0]
    ti = pl.program_id(0) % tiles_per_seq
    br_b = _dot(att_ref[...] * sza_ref[...], watt_ref[...])
    u = ua_ref[...].astype(F32)
    prev_row = jnp.where(ti == 0, 0.0, uprev_ref[BF16_ROWS - 1:BF16_ROWS, :].astype(F32))
    next_row = jnp.where(ti == tiles_per_seq - 1, 0.0, unext_ref[0:1, :].astype(F32))
    rid = lax.broadcasted_iota(jnp.int32, (tm, 1), 0)
    u_prev = jnp.where(rid == 0, prev_row, pltpu.roll(u, 1, 0))
    u_next = jnp.where(rid == tm - 1, next_row, pltpu.roll(u, tm - 1, 0))
    cw = convw_ref[...]
    conv = cw[0:1, :] * u_prev + cw[1:2, :] * u + cw[2:3, :] * u_next
    br_a = _dot((wa_ref[...].astype(F32) * conv).astype(BF16), wconv_ref[...])
    parts = []
    for hh in range(GLA_HEADS):
        cols = slice(hh * GLA_DVH, (hh + 1) * GLA_DVH)
        oh = of_ref[:, cols].astype(F32) + ob_ref[:, cols].astype(F32)
        parts.append((_head_norm(oh, glag_ref[...]) * szg_ref[:, cols].astype(F32)).astype(BF16))
    br_c = _dot(jnp.concatenate(parts, axis=1), wgla_ref[...])
    def gate(i):
        cols = slice(i * D_MODEL, (i + 1) * D_MODEL)
        return _sigmoid(_dot(h_ref[...], wgate_ref[:, cols]) + bgate_ref[:, cols])

    merged = gate(0) * br_a + gate(1) * br_b + gate(2) * br_c
    out = _dot(merged.astype(BF16), wout_ref[...])
    gate = mod_ref[:, 2 * D_MODEL:]
    o_ref[...] = x_ref[...] + gate * _head_norm(out, gpost_ref[...])


def _merge(ua, wa, att, sza, o_f, o_b, szg, h, x2, mod3, mod_row0, seq_len,
           convw, glag, gpost, bgate, wconv, watt, wgla, wout, wgate):
    n = x2.shape[0]
    tm = min(MERGE_TM, seq_len)
    tps = seq_len // tm
    halo = tm // BF16_ROWS
    n_halo = n // BF16_ROWS
    const = lambda i: (0, 0)
    row = lambda i: (i, 0)
    if mod_row0 is None:
        mod_map = lambda i: (i // tps, 0, 0)
    else:
        mod_map = lambda i: (mod_row0, 0, 0)
    tok = lambda w: pl.BlockSpec((tm, w), row)
    wspec = pl.BlockSpec((D_MODEL, D_MODEL), const, pipeline_mode=pl.Buffered(1))
    return pl.pallas_call(
        functools.partial(_merge_kernel, tiles_per_seq=tps),
        out_shape=jax.ShapeDtypeStruct((n, D_MODEL), F32),
        grid=(n // tm,),
        in_specs=[
            tok(CONV_W),
            pl.BlockSpec((BF16_ROWS, CONV_W), lambda i: (jnp.maximum(i * halo - 1, 0), 0)),
            pl.BlockSpec((BF16_ROWS, CONV_W), lambda i: (jnp.minimum((i + 1) * halo, n_halo - 1), 0)),
            tok(CONV_W), tok(Q_W), tok(Q_W), tok(GLA_DV), tok(GLA_DV), tok(GLA_DV),
            tok(D_MODEL), tok(D_MODEL),
            pl.BlockSpec((None, 1, 3 * D_MODEL), mod_map),
            pl.BlockSpec((3, CONV_W), const),
            pl.BlockSpec((1, GLA_DVH), const),
            pl.BlockSpec((1, D_MODEL), const),
            pl.BlockSpec((1, N_BRANCH * D_MODEL), const),
            wspec, wspec, wspec, wspec,
            pl.BlockSpec((D_MODEL, N_BRANCH * D_MODEL), const, pipeline_mode=pl.Buffered(1)),
        ],
        out_specs=tok(D_MODEL),
        compiler_params=_params("parallel"),
        name="merge_out",
    )(ua, ua, ua, wa, att, sza, o_f, o_b, szg, h, x2, mod3, convw, glag, gpost, bgate,
      wconv, watt, wgla, wout, wgate)


def _rope_tables(n_tokens):
    n_rows = n_tokens // GRID_W
    row = np.repeat(np.arange(n_rows, dtype=np.float32), GRID_W)
    col = np.tile(np.arange(GRID_W, dtype=np.float32), n_rows)
    freqs = (np.float32(ROPE_THETA) ** (-np.arange(ROPE_PAIRS, dtype=np.float32) * np.float32(2.0)
                                        / np.float32(ROPE_AXIS_DIM))).astype(np.float32)
    ar, ac = row[:, None] * freqs, col[:, None] * freqs
    cos_t = np.concatenate([np.cos(ar), np.cos(ar), np.cos(ac), np.cos(ac)], axis=1)
    sin_t = np.concatenate([-np.sin(ar), np.sin(ar), -np.sin(ac), np.sin(ac)], axis=1)
    return jnp.asarray(cos_t, F32), jnp.asarray(sin_t, F32)


def _split_w_in(w):
    tail = ORIG_R + 2 * GLA_RANK
    low_rank = jnp.pad(w[:, ORIG_R:tail], ((0, 0), (0, LANES - 2 * GLA_RANK)))
    return (w[:, :ORIG_R].astype(BF16), w[:, tail:ORIG_MG].astype(BF16), low_rank.astype(BF16),
            w[:, ORIG_MG:].astype(BF16))


def _pack_decay(w_f, b_f, w_b, b_b):
    wd = jnp.zeros((LANES, 2 * GLA_DK), F32)
    wd = wd.at[:GLA_RANK, :GLA_DK].set(w_f).at[GLA_RANK:2 * GLA_RANK, GLA_DK:].set(w_b)
    return wd.astype(BF16), jnp.concatenate([b_f, b_b])[None, :]


def kernel(x, c, ctx, c_ctx, w_ada, b_ada, g_pre, g_post, w_in, conv_w, q_norm_g, k_norm_g,
           w_decay_fwd, b_decay_fwd, w_decay_bwd, b_decay_bwd, gla_norm_g,
           w_br_conv, w_br_attn, w_br_gla, b_gate, w_out):
    batch, seq, _ = x.shape
    ctx_len = ctx.shape[1]
    assert seq % max(PROJ_TM, MERGE_TM, ATT_TQ, ATT_KV, GLA_ROWS) == 0 and seq % GRID_W == 0
    assert ctx_len % GLA_C == 0 and ctx_len % BF16_ROWS == 0
    assert PROJ_TM % ctx_len == 0 and MERGE_TM % ctx_len == 0 and ATT_TQ % ctx_len == 0

    mod_rows = -(-(batch + 1) // SUBLANES) * SUBLANES
    cvec = jnp.zeros((mod_rows, D_MODEL), F32).at[:batch].set(c).at[batch].set(c_ctx)
    mod = _modulation(cvec, w_ada, b_ada)
    cos_t, sin_t = _rope_tables(seq)
    zero_state = jnp.zeros((batch, GLA_HEADS, GLA_DVH, GLA_DKH), F32)

    xl = x.reshape(batch * seq, D_MODEL)
    xc = ctx.reshape(batch * ctx_len, D_MODEL)
    for l in range(DEPTH):
        last = l == DEPTH - 1
        mod3 = mod[l][:, None, :]
        *w_parts, w_gate = _split_w_in(w_in[l])
        wdec, bdec = _pack_decay(w_decay_fwd[l], b_decay_fwd[l], w_decay_bwd[l], b_decay_bwd[l])
        shared_in = (g_pre[l][None], w_parts, wdec, bdec, q_norm_g[l][None], k_norm_g[l][None])
        pl_ = _projection(xl, mod3, None, seq, *shared_in, cos_t, sin_t, use_rope=True)
        (ua_l, wa_l, q_l, k_l, vt_l, sza_l, gq_l, gk_l, gv_l, laf_l, lab_l, szg_l, h_l) = pl_
        if last:
            k_c, vt_c, gk_c, gv_c, laf_c, lab_c = _projection(
                xc, mod3, batch, ctx_len, *shared_in, cos_t, sin_t, use_rope=False, outs=PROJ_STATE_OUTS)
            s_f, s_b = _gla_scan(None, gk_c, gv_c, laf_c, lab_c, zero_state, zero_state, batch, ctx_len)
        else:
            pc = _projection(xc, mod3, batch, ctx_len, *shared_in, cos_t, sin_t, use_rope=False)
            (ua_c, wa_c, q_c, k_c, vt_c, sza_c, gq_c, gk_c, gv_c, laf_c, lab_c, szg_c, h_c) = pc
            of_c, ob_c, s_f, s_b = _gla_scan(gq_c, gk_c, gv_c, laf_c, lab_c, zero_state, zero_state,
                                             batch, ctx_len)
        att_l = _attention(q_l, k_l, vt_l, k_c, vt_c, batch, seq, seq, ctx_len)
        of_l, ob_l, _, _ = _gla_scan(gq_l, gk_l, gv_l, laf_l, lab_l, s_f, s_b, batch, seq)

        shared_out = (conv_w[l], gla_norm_g[l][None], g_post[l][None], b_gate[l][None],
                      w_br_conv[l].astype(BF16), w_br_attn[l].astype(BF16),
                      w_br_gla[l].astype(BF16), w_out[l].astype(BF16), w_gate)
        if not last:
            att_c = _attention(q_c, None, None, k_c, vt_c, batch, ctx_len, 0, ctx_len)
            xc = _merge(ua_c, wa_c, att_c, sza_c, of_c, ob_c, szg_c, h_c, xc, mod3, batch, ctx_len,
                        *shared_out)
        xl = _merge(ua_l, wa_l, att_l, sza_l, of_l, ob_l, szg_l, h_l, xl, mod3, None, seq,
                    *shared_out)
    return xl.reshape(batch, seq, D_MODEL)
```

```python
import functools

import jax
import jax.numpy as jnp
import numpy as np
from jax import lax
from jax.experimental import pallas as pl
from jax.experimental.pallas import tpu as pltpu

F32 = jnp.float32
BF16 = jnp.bfloat16

D_MODEL = 1024
DEPTH = 2
GRID_W = 64
CONV_W = 1024
N_HEADS = 8
N_KV_HEADS = 2
HEAD_DIM = 128
GROUP = N_HEADS // N_KV_HEADS
ROPE_THETA = 10000.0
ROPE_AXIS_DIM = HEAD_DIM // 2
ROPE_PAIRS = ROPE_AXIS_DIM // 2
ATTN_SCALE = HEAD_DIM ** -0.5
LOG2_E = 1.4426950408889634
GLA_HEADS = 4
GLA_DK = D_MODEL // 2
GLA_DV = D_MODEL
GLA_DKH = GLA_DK // GLA_HEADS
GLA_DVH = GLA_DV // GLA_HEADS
GLA_RANK = 16
GLA_TAU = 16.0
N_BRANCH = 3
EPS = 1e-6

Q_W = N_HEADS * HEAD_DIM
KV_W = N_KV_HEADS * HEAD_DIM

LANES = 128
SUBLANES = 8
BF16_ROWS = 16
VT_ROWS = HEAD_DIM + BF16_ROWS
VMEM_LIMIT = 56 * 1024 * 1024

OFF_A_B = 0
OFF_A_C = OFF_A_B + CONV_W
OFF_A_X = OFF_A_C + CONV_W
OFF_A_Z = OFF_A_X + CONV_W
OFF_Q = OFF_A_Z + CONV_W
OFF_K = OFF_Q + Q_W
OFF_V = OFF_K + KV_W
OFF_Z_ATT = OFF_V + KV_W
OFF_GQ = OFF_Z_ATT + Q_W
OFF_GK = OFF_GQ + GLA_DK
OFF_GV = OFF_GK + GLA_DK
OFF_Z_GLA = OFF_GV + GLA_DV
OFF_R = OFF_Z_GLA + GLA_DV
ORIG_R = OFF_Z_GLA
ORIG_MG = ORIG_R + 2 * GLA_RANK + GLA_DV

PROJ_TM = 512
MERGE_TM = 512
COL_BLK = 256
ATT_TQ = 512
ATT_KV = 512
GLA_C = 64
GLA_FINE = 4
GLA_ROWS = 1024
GLA_LEVELS = tuple(GLA_C >> (i + 1) for i in range(int(np.log2(GLA_C // GLA_FINE))))
assert len(GLA_LEVELS) * GLA_C % LANES == 0 and 2 * GLA_C == LANES and GLA_FINE <= SUBLANES


def _sigmoid(x):
    return jax.nn.sigmoid(x)


def _silu(x):
    return x * _sigmoid(x)


def _dot(a, b):
    return jnp.dot(a, b, preferred_element_type=F32)


def _dot_nt(a, b):
    return lax.dot_general(a, b, (((1,), (1,)), ((), ())), preferred_element_type=F32)


def _dot_tn(a, b):
    return lax.dot_general(a, b, (((0,), (0,)), ((), ())), preferred_element_type=F32)


def _params(*sem):
    return pltpu.CompilerParams(dimension_semantics=sem, vmem_limit_bytes=VMEM_LIMIT)


def _mod_kernel(c_ref, w_ref, b_ref, o_ref):
    s = _silu(c_ref[...])
    o_ref[...] = _dot(s.astype(BF16), w_ref[...].astype(BF16)) + b_ref[...]


def _modulation(cvec, w_ada, b_ada):
    rows = cvec.shape[0]
    n_col = 3 * D_MODEL // D_MODEL
    return pl.pallas_call(
        _mod_kernel,
        out_shape=jax.ShapeDtypeStruct((DEPTH, rows, 3 * D_MODEL), F32),
        grid=(DEPTH, n_col),
        in_specs=[
            pl.BlockSpec((rows, D_MODEL), lambda l, j: (0, 0)),
            pl.BlockSpec((None, D_MODEL, D_MODEL), lambda l, j: (l, 0, j)),
            pl.BlockSpec((None, 1, D_MODEL), lambda l, j: (l, 0, j)),
        ],
        out_specs=pl.BlockSpec((None, rows, D_MODEL), lambda l, j: (l, 0, j)),
        compiler_params=_params("parallel", "parallel"),
        name="adaln_mod",
    )(cvec, w_ada, b_ada.reshape(DEPTH, 1, 3 * D_MODEL))


def _head_norm(xh, g):
    ms = jnp.mean(xh * xh, axis=-1, keepdims=True)
    return xh * lax.rsqrt(ms + EPS) * g


def _rope(xh, cos, sin):
    lane = lax.broadcasted_iota(jnp.int32, xh.shape, 1)
    first_half = (lane % ROPE_AXIS_DIM) < ROPE_PAIRS
    partner = jnp.where(first_half,
                        pltpu.roll(xh, HEAD_DIM - ROPE_PAIRS, 1),
                        pltpu.roll(xh, ROPE_PAIRS, 1))
    return xh * cos + partner * sin


def _log_sigmoid(x):
    return jnp.minimum(x, 0.0) - jnp.log(1.0 + jnp.exp(-jnp.abs(x)))


PROJ_OUTS = (("ua", CONV_W, BF16), ("wa", CONV_W, BF16), ("q", Q_W, BF16), ("k", KV_W, BF16),
             ("vt", N_KV_HEADS * VT_ROWS, BF16), ("sza", Q_W, BF16), ("gq", GLA_DK, BF16),
             ("gk", GLA_DK, BF16), ("gv", GLA_DV, BF16), ("laf", GLA_DK, F32), ("lab", GLA_DK, F32),
             ("szg", GLA_DV, BF16), ("h", D_MODEL, BF16))
PROJ_STATE_OUTS = tuple(o for o in PROJ_OUTS if o[0] in ("k", "vt", "gk", "gv", "laf", "lab"))


def _proj_kernel(x_ref, mod_ref, xn_ref, modn_ref, gpre_ref, w_lo_ref, w_hi_ref, w_rank_ref,
                 wdec_ref, bdec_ref, qg_ref, kg_ref, cos_ref, sin_ref, *rest, use_rope, outs):
    out = dict(zip((name for name, _, _ in outs), rest))
    h_ref, v_ref = rest[len(outs):]
    step = pl.program_id(0)
    cur = step % 2

    def modulated_norm(xr, modr, slot):
        xv = xr[...]
        ms = jnp.mean(xv * xv, axis=-1, keepdims=True)
        y = xv * lax.rsqrt(ms + EPS) * gpre_ref[...]
        mod = modr[...]
        h_ref[slot] = (y * (1.0 + mod[:, D_MODEL:2 * D_MODEL]) + mod[:, :D_MODEL]).astype(BF16)

    @pl.when(step == 0)
    def _():
        modulated_norm(x_ref, mod_ref, 0)

    def proj(off, width=COL_BLK):
        for ref, base in ((w_rank_ref, OFF_R), (w_hi_ref, OFF_Z_GLA), (w_lo_ref, 0)):
            if off >= base:
                return _dot(h_ref[cur], ref[:, off - base:off - base + width])

    for o in range(0, CONV_W if "ua" in out else 0, COL_BLK):
        cols = slice(o, o + COL_BLK)
        out["ua"][:, cols] = (proj(OFF_A_C + o) * proj(OFF_A_X + o)).astype(BF16)
        out["wa"][:, cols] = (proj(OFF_A_B + o) * _silu(proj(OFF_A_Z + o))).astype(BF16)
        if o == 0:
            modulated_norm(xn_ref, modn_ref, 1 - cur)

    def heads(off, width, gain, out_ref, out_scale):
        for o in range(0, width, COL_BLK):
            blk = proj(off + o)
            for hh in range(COL_BLK // HEAD_DIM):
                xh = _head_norm(blk[:, hh * HEAD_DIM:(hh + 1) * HEAD_DIM], gain)
                if use_rope:
                    xh = _rope(xh, cos_ref[...], sin_ref[...])
                c0 = o + hh * HEAD_DIM
                out_ref[:, c0:c0 + HEAD_DIM] = (xh * out_scale).astype(BF16)

    if "q" in out:
        heads(OFF_Q, Q_W, qg_ref[...], out["q"], ATTN_SCALE * LOG2_E)
    heads(OFF_K, KV_W, kg_ref[...], out["k"], 1.0)
    if "ua" not in out:
        modulated_norm(xn_ref, modn_ref, 1 - cur)
    v_ref[...] = proj(OFF_V, KV_W)
    ones_tile = jnp.where(lax.broadcasted_iota(jnp.int32, (BF16_ROWS, x_ref.shape[0]), 0) == 0, 1.0, 0.0)
    for hh in range(N_KV_HEADS):
        out["vt"][hh * VT_ROWS:hh * VT_ROWS + HEAD_DIM, :] = (
            v_ref[:, hh * HEAD_DIM:(hh + 1) * HEAD_DIM].T.astype(BF16))
        out["vt"][hh * VT_ROWS + HEAD_DIM:(hh + 1) * VT_ROWS, :] = ones_tile.astype(BF16)
    for o in range(0, Q_W if "sza" in out else 0, COL_BLK):
        out["sza"][:, o:o + COL_BLK] = _silu(proj(OFF_Z_ATT + o)).astype(BF16)

    for o in range(0, GLA_DK, COL_BLK):
        if "gq" in out:
            out["gq"][:, o:o + COL_BLK] = (proj(OFF_GQ + o) * (GLA_DKH ** -0.5)).astype(BF16)
        out["gk"][:, o:o + COL_BLK] = proj(OFF_GK + o).astype(BF16)
    for o in range(0, GLA_DV if "szg" in out else 0, COL_BLK):
        out["szg"][:, o:o + COL_BLK] = _silu(proj(OFF_Z_GLA + o)).astype(BF16)

    r = proj(OFF_R, LANES).astype(BF16)

    def decay_piece(out_ref, o, wcol):
        out_ref[:, o:o + COL_BLK] = _log_sigmoid(
            _dot(r, wdec_ref[:, wcol:wcol + COL_BLK]) + bdec_ref[:, wcol:wcol + COL_BLK]
        ) * (LOG2_E / GLA_TAU)

    decay_pieces = ([(out["laf"], o, o) for o in range(0, GLA_DK, COL_BLK)]
                    + [(out["lab"], o, GLA_DK + o) for o in range(0, GLA_DK, COL_BLK)])
    assert len(decay_pieces) <= GLA_DV // COL_BLK
    for i, o in enumerate(range(0, GLA_DV, COL_BLK)):
        out["gv"][:, o:o + COL_BLK] = proj(OFF_GV + o).astype(BF16)
        if i < len(decay_pieces):
            decay_piece(*decay_pieces[i])
    if "h" in out:
        out["h"][...] = h_ref[cur]


def _projection(x2, mod3, mod_row0, seq_len, gpre, w_parts, wdec, bdec, qg, kg, cos_t, sin_t, use_rope,
                outs=PROJ_OUTS):
    n = x2.shape[0]
    tm = min(PROJ_TM, seq_len)
    tps = seq_len // tm
    const = lambda i: (0, 0)
    row = lambda i: (i, 0)
    pos = lambda i: (i % tps, 0)
    nxt = lambda i: jnp.minimum(i + 1, n // tm - 1)
    if mod_row0 is None:
        mod_map = lambda i: (i // tps, 0, 0)
    else:
        mod_map = lambda i: (mod_row0, 0, 0)
    transposed = ("vt",)
    return pl.pallas_call(
        functools.partial(_proj_kernel, use_rope=use_rope, outs=outs),
        out_shape=[jax.ShapeDtypeStruct((w, n) if name in transposed else (n, w), dt)
                   for name, w, dt in outs],
        grid=(n // tm,),
        in_specs=[
            pl.BlockSpec((tm, D_MODEL), row),
            pl.BlockSpec((None, 1, 3 * D_MODEL), mod_map),
            pl.BlockSpec((tm, D_MODEL), lambda i: (nxt(i), 0)),
            pl.BlockSpec((None, 1, 3 * D_MODEL), lambda i: mod_map(nxt(i))),
            pl.BlockSpec((1, D_MODEL), const),
            pl.BlockSpec((D_MODEL, OFF_Z_GLA), const, pipeline_mode=pl.Buffered(1)),
            pl.BlockSpec((D_MODEL, OFF_R - OFF_Z_GLA), const, pipeline_mode=pl.Buffered(1)),
            pl.BlockSpec((D_MODEL, LANES), const, pipeline_mode=pl.Buffered(1)),
            pl.BlockSpec((LANES, 2 * GLA_DK), const),
            pl.BlockSpec((1, 2 * GLA_DK), const),
            pl.BlockSpec((1, HEAD_DIM), const),
            pl.BlockSpec((1, HEAD_DIM), const),
            pl.BlockSpec((tm, HEAD_DIM), pos),
            pl.BlockSpec((tm, HEAD_DIM), pos),
        ],
        out_specs=[pl.BlockSpec((w, tm), lambda i: (0, i)) if name in transposed
                   else pl.BlockSpec((tm, w), row) for name, w, _ in outs],
        scratch_shapes=[pltpu.VMEM((2, tm, D_MODEL), BF16), pltpu.VMEM((tm, KV_W), F32)],
        compiler_params=_params("arbitrary"),
        name="in_proj_rope" if use_rope else "in_proj",
    )(x2, mod3, x2, mod3, gpre, *w_parts, wdec, bdec, qg, kg, cos_t, sin_t)


def _attn_stages(tq, m_ref, alpha_ref, acc_ref):
    heads = [slice(g * tq, (g + 1) * tq) for g in range(GROUP)]

    def step(scores=(), values=(), softmax=()):
        for q_src, k, s_dst, smax_dst in scores:
            for g, cols in enumerate(heads):
                s = _dot_nt(k, q_src[:, g * HEAD_DIM:(g + 1) * HEAD_DIM])
                s_dst[:, cols] = s
                smax_dst[:, cols] = jnp.max(s, axis=0, keepdims=True)
        for p_src, vt, slot in values:
            for cols in heads:
                acc_ref[slot, :, cols] = (alpha_ref[slot, :, cols] * acc_ref[slot, :, cols]
                                          + _dot(vt, p_src[:, cols]))
        for s_src, smax_src, p_dst, slot in softmax:
            for cols in heads:
                m_old = m_ref[slot, :, cols]
                m_new = jnp.maximum(m_old, smax_src[:, cols])
                p_dst[:, cols] = jnp.exp2(s_src[:, cols] - m_new).astype(BF16)
                alpha_ref[slot, :, cols] = jnp.exp2(m_old - m_new)
                m_ref[slot, :, cols] = m_new

    def init(slot):
        m_ref[slot] = jnp.full(m_ref.shape[1:], -jnp.inf, F32)
        acc_ref[slot] = jnp.zeros(acc_ref.shape[1:], F32)

    def finish(slot, o_ref):
        out_t = acc_ref[slot, :HEAD_DIM, :] / acc_ref[slot, HEAD_DIM:HEAD_DIM + 1, :]
        for g in range(GROUP):
            o_ref[:, g * HEAD_DIM:(g + 1) * HEAD_DIM] = out_t[:, g * tq:(g + 1) * tq].T.astype(BF16)

    return step, init, finish


def _attn_ctx_kernel(q_ref, kc_ref, vtc_ref, o_ref, m_ref, alpha_ref, acc_ref, mc_ref, sc_ref, pc_ref):
    step, init, finish = _attn_stages(q_ref.shape[0], m_ref, alpha_ref, acc_ref)
    init(0)
    step(scores=[(q_ref, kc_ref[...], sc_ref, mc_ref)])
    step(softmax=[(sc_ref, mc_ref, pc_ref, 0)])
    step(values=[(pc_ref, vtc_ref[...], 0)])
    finish(0, o_ref)


def _attn_kernel(q_ref, qn_ref, kl_ref, vtl_ref, kc_ref, vtc_ref, o_ref, m_ref, alpha_ref, acc_ref,
                 mc_ref, sc_ref, pc_ref, ma_ref, mb_ref, sa_ref, sb_ref, pa_ref, pb_ref, *, n_lat_blocks):
    tile = pl.program_id(2)
    cur = tile % 2
    nxt = 1 - cur
    n = n_lat_blocks
    step, init, finish = _attn_stages(q_ref.shape[0], m_ref, alpha_ref, acc_ref)
    buf_a, buf_b, buf_c = (sa_ref, ma_ref), (sb_ref, mb_ref), (sc_ref, mc_ref)

    def keys(j):
        return pl.ds(pl.multiple_of(j * ATT_KV, ATT_KV), ATT_KV)

    @pl.when(tile == 0)
    def _():
        init(cur)
        step(scores=[(q_ref, kl_ref[keys(0), :], *buf_a)])
        step(scores=[(q_ref, kl_ref[keys(1), :], *buf_b)], softmax=[(*buf_a, pa_ref, cur)])

    def pair(i, carry):
        t = 2 * i
        step(scores=[(q_ref, kl_ref[keys(t), :], *buf_a)], values=[(pa_ref, vtl_ref[:, keys(t - 2)], cur)],
             softmax=[(*buf_b, pb_ref, cur)])
        step(scores=[(q_ref, kl_ref[keys(t + 1), :], *buf_b)], values=[(pb_ref, vtl_ref[:, keys(t - 1)], cur)],
             softmax=[(*buf_a, pa_ref, cur)])
        return carry

    lax.fori_loop(1, n // 2, pair, 0)
    init(nxt)
    step(scores=[(q_ref, kc_ref[...], *buf_c), (qn_ref, kl_ref[keys(0), :], *buf_a)],
         values=[(pa_ref, vtl_ref[:, keys(n - 2)], cur)], softmax=[(*buf_b, pb_ref, cur)])
    step(scores=[(qn_ref, kl_ref[keys(1), :], *buf_b)],
         values=[(pb_ref, vtl_ref[:, keys(n - 1)], cur)],
         softmax=[(*buf_c, pc_ref, cur), (*buf_a, pa_ref, nxt)])
    step(values=[(pc_ref, vtc_ref[...], cur)])
    finish(cur, o_ref)


def _attention(q2, k_lat, vt_lat, k_ctx, vt_ctx, batch, q_len, lat_len, ctx_len):
    tq = min(ATT_TQ, q_len)
    tiles = q_len // tq
    rows = GROUP * tq
    qmap = lambda b, kv, i: (b * tiles + i, kv)
    qnext = lambda b, kv, i: (b * tiles + jnp.minimum(i + 1, tiles - 1), kv)
    kmap = lambda b, kv, i: (b, kv)
    vtmap = lambda b, kv, i: (kv, b)
    q_spec = pl.BlockSpec((tq, GROUP * HEAD_DIM), qmap)
    ctx_specs = [pl.BlockSpec((ctx_len, HEAD_DIM), kmap), pl.BlockSpec((VT_ROWS, ctx_len), vtmap)]
    stat = pltpu.VMEM((1, rows), F32)
    if k_lat is None:
        slots = 1
        body, name = _attn_ctx_kernel, "gqa_ctx"
        in_specs, args = [q_spec] + ctx_specs, [q2, k_ctx, vt_ctx]
        buffers = []
        semantics = ("parallel", "parallel", "parallel")
    else:
        slots = 2
        n_lat_blocks = lat_len // ATT_KV
        assert n_lat_blocks % 2 == 0 and n_lat_blocks >= 2
        body, name = functools.partial(_attn_kernel, n_lat_blocks=n_lat_blocks), "gqa_lat"
        in_specs = ([q_spec, pl.BlockSpec((tq, GROUP * HEAD_DIM), qnext),
                     pl.BlockSpec((lat_len, HEAD_DIM), kmap), pl.BlockSpec((VT_ROWS, lat_len), vtmap)]
                    + ctx_specs)
        args = [q2, q2, k_lat, vt_lat, k_ctx, vt_ctx]
        buffers = [stat, stat] + [pltpu.VMEM((ATT_KV, rows), F32)] * 2 + [pltpu.VMEM((ATT_KV, rows), BF16)] * 2
        semantics = ("parallel", "parallel", "arbitrary")
    scratch = [pltpu.VMEM((slots, 1, rows), F32), pltpu.VMEM((slots, 1, rows), F32),
               pltpu.VMEM((slots, VT_ROWS, rows), F32), stat,
               pltpu.VMEM((ctx_len, rows), F32), pltpu.VMEM((ctx_len, rows), BF16)] + buffers
    return pl.pallas_call(
        body,
        out_shape=jax.ShapeDtypeStruct(q2.shape, BF16),
        grid=(batch, N_KV_HEADS, tiles),
        in_specs=in_specs,
        out_specs=q_spec,
        scratch_shapes=scratch,
        compiler_params=_params(*semantics),
        name=name,
    )(*args)


def _split2(x):
    hi = x.astype(BF16)
    lo = (x - hi.astype(F32)).astype(BF16)
    return hi, lo


def _gla_consts(rev):
    c = GLA_C
    tri_r = lax.broadcasted_iota(jnp.int32, (c, c), 0)
    tri_c = lax.broadcasted_iota(jnp.int32, (c, c), 1)
    tri = jnp.where((tri_c >= tri_r) if rev else (tri_c <= tri_r), 1.0, 0.0).astype(BF16)
    row = lax.broadcasted_iota(jnp.int32, (c, LANES), 0)
    lane = lax.broadcasted_iota(jnp.int32, (c, LANES), 1)
    level_masks = []
    for i, m in enumerate(GLA_LEVELS):
        key = lane - c * (i % 2)
        in_half = (key >= 0) & (key < c)
        same = (row // (2 * m)) == (key // (2 * m))
        q_upper = (row % (2 * m)) >= m
        k_upper = (key % (2 * m)) >= m
        pair = (~q_upper & k_upper) if rev else (q_upper & ~k_upper)
        level_masks.append(in_half & same & pair)
    sub = row % GLA_FINE
    d_of_lane = lane if rev else (LANES - lane) % LANES
    ok = (sub + d_of_lane < GLA_FINE) if rev else (sub >= d_of_lane)
    lane_code = jnp.where((d_of_lane < GLA_FINE) & ok, d_of_lane, -1)
    return tri, level_masks, lane_code


def _gla_kernel(*refs, with_outputs):
    if with_outputs:
        (qf_ref, kf_ref, vf_ref, laf_ref, qb_ref, kb_ref, vb_ref, lab_ref, s0f_ref, s0b_ref,
         of_ref, ob_ref, sff_ref, sfb_ref, st_ref) = refs
    else:
        kf_ref, vf_ref, laf_ref, kb_ref, vb_ref, lab_ref, s0f_ref, s0b_ref, sff_ref, sfb_ref, st_ref = refs
        qf_ref = qb_ref = of_ref = ob_ref = None

    @pl.when(pl.program_id(1) == 0)
    def _():
        st_ref[0] = s0f_ref[...]
        st_ref[1] = s0b_ref[...]

    c = GLA_C
    n_chunks = kf_ref.shape[0] // c
    dirs = ((0, False, qf_ref, kf_ref, vf_ref, laf_ref, of_ref, _gla_consts(False)),
            (1, True, qb_ref, kb_ref, vb_ref, lab_ref, ob_ref, _gla_consts(True)))

    def body(ci, carry):
        chains = []
        for di, rev, q_ref, k_ref, v_ref, la_ref, o_ref, consts in dirs:
            cc = (n_chunks - 1 - ci) if rev else ci
            rows = pl.ds(pl.multiple_of(cc * c, c), c)
            cum2 = _dot(consts[0], jnp.concatenate(_split2(la_ref[rows, :]), axis=1))
            b_all = cum2[:, :GLA_DK] + cum2[:, GLA_DK:]
            for hh in range(GLA_HEADS):
                kc = slice(hh * GLA_DKH, (hh + 1) * GLA_DKH)
                vc = slice(hh * GLA_DVH, (hh + 1) * GLA_DVH)
                chains.append(dict(di=di, hh=hh, rev=rev, rows=rows, vc=vc, o_ref=o_ref,
                                   masks=consts[1], lane_code=consts[2], b=b_all[:, kc],
                                   qf=q_ref[rows, kc].astype(F32) if with_outputs else None,
                                   kf=k_ref[rows, kc].astype(F32), v=v_ref[rows, vc]))

        for w in chains:
            b, qf, kf, rev = w["b"], w["qf"], w["kf"], w["rev"]
            st = st_ref[w["di"], w["hh"]]
            tot = b[0:1, :] if rev else b[c - 1:c, :]
            kd = (kf * jnp.exp2(tot - b)).astype(BF16)
            st_ref[w["di"], w["hh"]] = st * jnp.exp2(tot) + _dot_tn(w["v"], kd)
            if not with_outputs:
                continue
            w["o"] = _dot_nt((qf * jnp.exp2(b)).astype(BF16), st.astype(BF16))
            qs, ks = [], []
            for m in GLA_LEVELS:
                pivot = (m - 1) if rev else m
                bm = b.reshape(c // (2 * m), 2 * m, GLA_DKH)
                f = jnp.exp2(-jnp.abs(bm - bm[:, pivot:pivot + 1, :])).reshape(c, GLA_DKH)
                qs.append((qf * f).astype(BF16))
                ks.append((kf * f).astype(BF16))
            w["coarse"] = _dot_nt(jnp.concatenate(qs, axis=0), jnp.concatenate(ks, axis=0))

        grouped = (c // SUBLANES, SUBLANES, GLA_DKH)
        for w in chains if with_outputs else ():
            qf, kf, rev, lane_code = w["qf"], w["kf"], w["rev"], w["lane_code"]
            w_slots = jnp.where(lane_code == 0, jnp.sum(qf * kf, axis=-1, keepdims=True), 0.0)
            q3, k3, b3 = qf.reshape(grouped), kf.reshape(grouped), w["b"].reshape(grouped)
            for d in range(1, GLA_FINE):
                shift = (SUBLANES - d) if rev else d
                kr = pltpu.roll(k3, shift, 1)
                br = pltpu.roll(b3, shift, 1)
                wd = jnp.sum(q3 * kr * jnp.exp2(jnp.minimum(b3 - br, 0.0)), axis=-1, keepdims=True)
                w_slots = jnp.where(lane_code == d, wd.reshape(c, 1), w_slots)
            w["att"] = pltpu.roll(w_slots, 0, 1, stride=1, stride_axis=0)

        for w in chains if with_outputs else ():
            att = w["att"]
            for i, mask in enumerate(w["masks"]):
                col0 = (i * c) // LANES * LANES
                att = jnp.where(mask, w["coarse"][i * c:(i + 1) * c, col0:col0 + LANES], att)
            o = w["o"] + _dot(att.astype(BF16), jnp.concatenate([w["v"], w["v"]], axis=0))
            w["o_ref"][w["rows"], w["vc"]] = o.astype(BF16)
        return carry

    lax.fori_loop(0, n_chunks, body, 0)
    sff_ref[...] = st_ref[0]
    sfb_ref[...] = st_ref[1]


def _gla_scan(gq, gk, gv, la_f, la_b, s0_f, s0_b, batch, seq_len):
    rows = min(GLA_ROWS, seq_len)
    nblk = seq_len // rows
    fwd = lambda b, i: (b * nblk + i, 0)
    bwd = lambda b, i: (b * nblk + (nblk - 1 - i), 0)
    smap = lambda b, i: (b, 0, 0, 0)
    state_spec = pl.BlockSpec((None, GLA_HEADS, GLA_DVH, GLA_DKH), smap)
    state_shape = jax.ShapeDtypeStruct((batch, GLA_HEADS, GLA_DVH, GLA_DKH), F32)
    with_outputs = gq is not None

    def view(index_map, la):
        arrays = [(gq, GLA_DK)] * with_outputs + [(gk, GLA_DK), (gv, GLA_DV), (la, GLA_DK)]
        return [a for a, _ in arrays], [pl.BlockSpec((rows, w), index_map) for _, w in arrays]

    (args_f, specs_f), (args_b, specs_b) = view(fwd, la_f), view(bwd, la_b)
    o_shape = [jax.ShapeDtypeStruct(gv.shape, BF16)] * 2 if with_outputs else []
    o_specs = [pl.BlockSpec((rows, GLA_DV), fwd), pl.BlockSpec((rows, GLA_DV), bwd)] if with_outputs else []
    return pl.pallas_call(
        functools.partial(_gla_kernel, with_outputs=with_outputs),
        out_shape=o_shape + [state_shape, state_shape],
        grid=(batch, nblk),
        in_specs=specs_f + specs_b + [state_spec, state_spec],
        out_specs=o_specs + [state_spec, state_spec],
        scratch_shapes=[pltpu.VMEM((2, GLA_HEADS, GLA_DVH, GLA_DKH), F32)],
        compiler_params=_params("parallel", "arbitrary"),
        name="gla_bidir" if with_outputs else "gla_states",
    )(*args_f, *args_b, s0_f, s0_b)


def _merge_kernel(ua_ref, uprev_ref, unext_ref, wa_ref, att_ref, sza_ref, of_ref, ob_ref, szg_ref,
                  h_ref, x_ref, mod_ref, convw_ref, glag_ref, gpost_ref, bgate_ref,
                  wconv_ref, watt_ref, wgla_ref, wout_ref, wgate_ref, o_ref, *, tiles_per_seq):
    tm = x_ref.shape[0]
    ti = pl.program_id(0) % tiles_per_seq
    br_b = _dot(att_ref[...] * sza_ref[...], watt_ref[...])
    u = ua_ref[...].astype(F32)
    prev_row = jnp.where(ti == 0, 0.0, uprev_ref[BF16_ROWS - 1:BF16_ROWS, :].astype(F32))
    next_row = jnp.where(ti == tiles_per_seq - 1, 0.0, unext_ref[0:1, :].astype(F32))
    rid = lax.broadcasted_iota(jnp.int32, (tm, 1), 0)
    u_prev = jnp.where(rid == 0, prev_row, pltpu.roll(u, 1, 0))
    u_next = jnp.where(rid == tm - 1, next_row, pltpu.roll(u, tm - 1, 0))
    cw = convw_ref[...]
    conv = cw[0:1, :] * u_prev + cw[1:2, :] * u + cw[2:3, :] * u_next
    br_a = _dot((wa_ref[...].astype(F32) * conv).astype(BF16), wconv_ref[...])
    parts = []
    for hh in range(GLA_HEADS):
        cols = slice(hh * GLA_DVH, (hh + 1) * GLA_DVH)
        oh = of_ref[:, cols].astype(F32) + ob_ref[:, cols].astype(F32)
        parts.append((_head_norm(oh, glag_ref[...]) * szg_ref[:, cols].astype(F32)).astype(BF16))
    br_c = _dot(jnp.concatenate(parts, axis=1), wgla_ref[...])
    def gate(i):
        cols = slice(i * D_MODEL, (i + 1) * D_MODEL)
        return _sigmoid(_dot(h_ref[...], wgate_ref[:, cols]) + bgate_ref[:, cols])

    merged = gate(0) * br_a + gate(1) * br_b + gate(2) * br_c
    out = _dot(merged.astype(BF16), wout_ref[...])
    gate = mod_ref[:, 2 * D_MODEL:]
    o_ref[...] = x_ref[...] + gate * _head_norm(out, gpost_ref[...])


def _merge(ua, wa, att, sza, o_f, o_b, szg, h, x2, mod3, mod_row0, seq_len,
           convw, glag, gpost, bgate, wconv, watt, wgla, wout, wgate):
    n = x2.shape[0]
    tm = min(MERGE_TM, seq_len)
    tps = seq_len // tm
    halo = tm // BF16_ROWS
    n_halo = n // BF16_ROWS
    const = lambda i: (0, 0)
    row = lambda i: (i, 0)
    if mod_row0 is None:
        mod_map = lambda i: (i // tps, 0, 0)
    else:
        mod_map = lambda i: (mod_row0, 0, 0)
    tok = lambda w: pl.BlockSpec((tm, w), row)
    wspec = pl.BlockSpec((D_MODEL, D_MODEL), const, pipeline_mode=pl.Buffered(1))
    return pl.pallas_call(
        functools.partial(_merge_kernel, tiles_per_seq=tps),
        out_shape=jax.ShapeDtypeStruct((n, D_MODEL), F32),
        grid=(n // tm,),
        in_specs=[
            tok(CONV_W),
            pl.BlockSpec((BF16_ROWS, CONV_W), lambda i: (jnp.maximum(i * halo - 1, 0), 0)),
            pl.BlockSpec((BF16_ROWS, CONV_W), lambda i: (jnp.minimum((i + 1) * halo, n_halo - 1), 0)),
            tok(CONV_W), tok(Q_W), tok(Q_W), tok(GLA_DV), tok(GLA_DV), tok(GLA_DV),
            tok(D_MODEL), tok(D_MODEL),
            pl.BlockSpec((None, 1, 3 * D_MODEL), mod_map),
            pl.BlockSpec((3, CONV_W), const),
            pl.BlockSpec((1, GLA_DVH), const),
            pl.BlockSpec((1, D_MODEL), const),
            pl.BlockSpec((1, N_BRANCH * D_MODEL), const),
            wspec, wspec, wspec, wspec,
            pl.BlockSpec((D_MODEL, N_BRANCH * D_MODEL), const, pipeline_mode=pl.Buffered(1)),
        ],
        out_specs=tok(D_MODEL),
        compiler_params=_params("parallel"),
        name="merge_out",
    )(ua, ua, ua, wa, att, sza, o_f, o_b, szg, h, x2, mod3, convw, glag, gpost, bgate,
      wconv, watt, wgla, wout, wgate)


def _rope_tables(n_tokens):
    n_rows = n_tokens // GRID_W
    row = np.repeat(np.arange(n_rows, dtype=np.float32), GRID_W)
    col = np.tile(np.arange(GRID_W, dtype=np.float32), n_rows)
    freqs = (np.float32(ROPE_THETA) ** (-np.arange(ROPE_PAIRS, dtype=np.float32) * np.float32(2.0)
                                        / np.float32(ROPE_AXIS_DIM))).astype(np.float32)
    ar, ac = row[:, None] * freqs, col[:, None] * freqs
    cos_t = np.concatenate([np.cos(ar), np.cos(ar), np.cos(ac), np.cos(ac)], axis=1)
    sin_t = np.concatenate([-np.sin(ar), np.sin(ar), -np.sin(ac), np.sin(ac)], axis=1)
    return jnp.asarray(cos_t, F32), jnp.asarray(sin_t, F32)


def _split_w_in(w):
    wb = w.astype(BF16)
    tail = ORIG_R + 2 * GLA_RANK
    low_rank = jnp.pad(wb[:, ORIG_R:tail], ((0, 0), (0, LANES - 2 * GLA_RANK)))
    return wb, wb[:, tail:ORIG_MG], low_rank, wb[:, ORIG_MG:]


def _pack_decay(w_f, b_f, w_b, b_b):
    wd = jnp.zeros((LANES, 2 * GLA_DK), F32)
    wd = wd.at[:GLA_RANK, :GLA_DK].set(w_f).at[GLA_RANK:2 * GLA_RANK, GLA_DK:].set(w_b)
    return wd.astype(BF16), jnp.concatenate([b_f, b_b])[None, :]


def kernel(x, c, ctx, c_ctx, w_ada, b_ada, g_pre, g_post, w_in, conv_w, q_norm_g, k_norm_g,
           w_decay_fwd, b_decay_fwd, w_decay_bwd, b_decay_bwd, gla_norm_g,
           w_br_conv, w_br_attn, w_br_gla, b_gate, w_out):
    batch, seq, _ = x.shape
    ctx_len = ctx.shape[1]
    assert seq % max(PROJ_TM, MERGE_TM, ATT_TQ, ATT_KV, GLA_ROWS) == 0 and seq % GRID_W == 0
    assert ctx_len % GLA_C == 0 and ctx_len % BF16_ROWS == 0
    assert PROJ_TM % ctx_len == 0 and MERGE_TM % ctx_len == 0 and ATT_TQ % ctx_len == 0

    mod_rows = -(-(batch + 1) // SUBLANES) * SUBLANES
    cvec = jnp.zeros((mod_rows, D_MODEL), F32).at[:batch].set(c).at[batch].set(c_ctx)
    mod = _modulation(cvec, w_ada, b_ada)
    cos_t, sin_t = _rope_tables(seq)
    zero_state = jnp.zeros((batch, GLA_HEADS, GLA_DVH, GLA_DKH), F32)

    xl = x.reshape(batch * seq, D_MODEL)
    xc = ctx.reshape(batch * ctx_len, D_MODEL)
    for l in range(DEPTH):
        last = l == DEPTH - 1
        mod3 = mod[l][:, None, :]
        *w_parts, w_gate = _split_w_in(w_in[l])
        wdec, bdec = _pack_decay(w_decay_fwd[l], b_decay_fwd[l], w_decay_bwd[l], b_decay_bwd[l])
        shared_in = (g_pre[l][None], w_parts, wdec, bdec, q_norm_g[l][None], k_norm_g[l][None])
        pl_ = _projection(xl, mod3, None, seq, *shared_in, cos_t, sin_t, use_rope=True)
        (ua_l, wa_l, q_l, k_l, vt_l, sza_l, gq_l, gk_l, gv_l, laf_l, lab_l, szg_l, h_l) = pl_
        if last:
            k_c, vt_c, gk_c, gv_c, laf_c, lab_c = _projection(
                xc, mod3, batch, ctx_len, *shared_in, cos_t, sin_t, use_rope=False, outs=PROJ_STATE_OUTS)
            s_f, s_b = _gla_scan(None, gk_c, gv_c, laf_c, lab_c, zero_state, zero_state, batch, ctx_len)
        else:
            pc = _projection(xc, mod3, batch, ctx_len, *shared_in, cos_t, sin_t, use_rope=False)
            (ua_c, wa_c, q_c, k_c, vt_c, sza_c, gq_c, gk_c, gv_c, laf_c, lab_c, szg_c, h_c) = pc
            of_c, ob_c, s_f, s_b = _gla_scan(gq_c, gk_c, gv_c, laf_c, lab_c, zero_state, zero_state,
                                             batch, ctx_len)
        att_l = _attention(q_l, k_l, vt_l, k_c, vt_c, batch, seq, seq, ctx_len)
        of_l, ob_l, _, _ = _gla_scan(gq_l, gk_l, gv_l, laf_l, lab_l, s_f, s_b, batch, seq)

        shared_out = (conv_w[l], gla_norm_g[l][None], g_post[l][None], b_gate[l][None],
                      w_br_conv[l].astype(BF16), w_br_attn[l].astype(BF16),
                      w_br_gla[l].astype(BF16), w_out[l].astype(BF16), w_gate)
        if not last:
            att_c = _attention(q_c, None, None, k_c, vt_c, batch, ctx_len, 0, ctx_len)
            xc = _merge(ua_c, wa_c, att_c, sza_c, of_c, ob_c, szg_c, h_c, xc, mod3, batch, ctx_len,
                        *shared_out)
        xl = _merge(ua_l, wa_l, att_l, sza_l, of_l, ob_l, szg_l, h_l, xl, mod3, None, seq,
                    *shared_out)
    return xl.reshape(batch, seq, D_MODEL)
```

```python
import functools

import jax
import jax.numpy as jnp
import numpy as np
from jax import lax
from jax.experimental import pallas as pl
from jax.experimental.pallas import tpu as pltpu

F32 = jnp.float32
BF16 = jnp.bfloat16

D_MODEL = 1024
DEPTH = 2
GRID_W = 64
CONV_W = 1024
N_HEADS = 8
N_KV_HEADS = 2
HEAD_DIM = 128
GROUP = N_HEADS // N_KV_HEADS
ROPE_THETA = 10000.0
ROPE_AXIS_DIM = HEAD_DIM // 2
ROPE_PAIRS = ROPE_AXIS_DIM // 2
ATTN_SCALE = HEAD_DIM ** -0.5
LOG2_E = 1.4426950408889634
GLA_HEADS = 4
GLA_DK = D_MODEL // 2
GLA_DV = D_MODEL
GLA_DKH = GLA_DK // GLA_HEADS
GLA_DVH = GLA_DV // GLA_HEADS
GLA_RANK = 16
GLA_TAU = 16.0
N_BRANCH = 3
EPS = 1e-6

Q_W = N_HEADS * HEAD_DIM
KV_W = N_KV_HEADS * HEAD_DIM

LANES = 128
SUBLANES = 8
BF16_ROWS = 16
VT_ROWS = HEAD_DIM + BF16_ROWS
VMEM_LIMIT = 56 * 1024 * 1024

OFF_A_B = 0
OFF_A_C = OFF_A_B + CONV_W
OFF_A_X = OFF_A_C + CONV_W
OFF_A_Z = OFF_A_X + CONV_W
OFF_Q = OFF_A_Z + CONV_W
OFF_K = OFF_Q + Q_W
OFF_V = OFF_K + KV_W
OFF_Z_ATT = OFF_V + KV_W
OFF_GQ = OFF_Z_ATT + Q_W
OFF_GK = OFF_GQ + GLA_DK
OFF_GV = OFF_GK + GLA_DK
OFF_Z_GLA = OFF_GV + GLA_DV
OFF_R = OFF_Z_GLA + GLA_DV
ORIG_R = OFF_Z_GLA
ORIG_MG = ORIG_R + 2 * GLA_RANK + GLA_DV

PROJ_TM = 512
MERGE_TM = 512
MERGE_STREAMS = 2
COL_BLK = 256
ATT_TQ = 512
ATT_KV = 512
GLA_C = 64
GLA_FINE = 4
GLA_ROWS = 1024
GLA_LEVELS = tuple(GLA_C >> (i + 1) for i in range(int(np.log2(GLA_C // GLA_FINE))))
assert len(GLA_LEVELS) * GLA_C % LANES == 0 and 2 * GLA_C == LANES and GLA_FINE <= SUBLANES


def _sigmoid(x):
    return jax.nn.sigmoid(x)


def _silu(x):
    return x * _sigmoid(x)


def _dot(a, b):
    return jnp.dot(a, b, preferred_element_type=F32)


def _dot_nt(a, b):
    return lax.dot_general(a, b, (((1,), (1,)), ((), ())), preferred_element_type=F32)


def _dot_tn(a, b):
    return lax.dot_general(a, b, (((0,), (0,)), ((), ())), preferred_element_type=F32)


def _params(*sem):
    return pltpu.CompilerParams(dimension_semantics=sem, vmem_limit_bytes=VMEM_LIMIT)


def _mod_kernel(c_ref, w_ref, b_ref, o_ref):
    s = _silu(c_ref[...])
    o_ref[...] = _dot(s.astype(BF16), w_ref[...].astype(BF16)) + b_ref[...]


def _modulation(cvec, w_ada, b_ada):
    rows = cvec.shape[0]
    n_col = 3 * D_MODEL // D_MODEL
    return pl.pallas_call(
        _mod_kernel,
        out_shape=jax.ShapeDtypeStruct((DEPTH, rows, 3 * D_MODEL), F32),
        grid=(DEPTH, n_col),
        in_specs=[
            pl.BlockSpec((rows, D_MODEL), lambda l, j: (0, 0)),
            pl.BlockSpec((None, D_MODEL, D_MODEL), lambda l, j: (l, 0, j)),
            pl.BlockSpec((None, 1, D_MODEL), lambda l, j: (l, 0, j)),
        ],
        out_specs=pl.BlockSpec((None, rows, D_MODEL), lambda l, j: (l, 0, j)),
        compiler_params=_params("parallel", "parallel"),
        name="adaln_mod",
    )(cvec, w_ada, b_ada.reshape(DEPTH, 1, 3 * D_MODEL))


def _head_norm(xh, g):
    ms = jnp.mean(xh * xh, axis=-1, keepdims=True)
    return xh * lax.rsqrt(ms + EPS) * g


def _rope(xh, cos, sin):
    lane = lax.broadcasted_iota(jnp.int32, xh.shape, 1)
    first_half = (lane % ROPE_AXIS_DIM) < ROPE_PAIRS
    partner = jnp.where(first_half,
                        pltpu.roll(xh, HEAD_DIM - ROPE_PAIRS, 1),
                        pltpu.roll(xh, ROPE_PAIRS, 1))
    return xh * cos + partner * sin


def _log_sigmoid(x):
    return jnp.minimum(x, 0.0) - jnp.log(1.0 + jnp.exp(-jnp.abs(x)))


PROJ_OUTS = (("ua", CONV_W, BF16), ("wa", CONV_W, BF16), ("q", Q_W, BF16), ("k", KV_W, BF16),
             ("vt", N_KV_HEADS * VT_ROWS, BF16), ("sza", Q_W, BF16), ("gq", GLA_DK, BF16),
             ("gk", GLA_DK, BF16), ("gv", GLA_DV, BF16), ("laf", GLA_DK, F32), ("lab", GLA_DK, F32),
             ("szg", GLA_DV, BF16), ("h", D_MODEL, BF16))
PROJ_STATE_OUTS = tuple(o for o in PROJ_OUTS if o[0] in ("k", "vt", "gk", "gv", "laf", "lab"))


def _proj_kernel(x_ref, mod_ref, xn_ref, modn_ref, gpre_ref, w_lo_ref, w_hi_ref, w_rank_ref,
                 wdec_ref, bdec_ref, qg_ref, kg_ref, cos_ref, sin_ref, *rest, use_rope, outs):
    out = dict(zip((name for name, _, _ in outs), rest))
    h_ref, v_ref = rest[len(outs):]
    step = pl.program_id(0)
    cur = step % 2

    def modulated_norm(xr, modr, slot):
        xv = xr[...]
        ms = jnp.mean(xv * xv, axis=-1, keepdims=True)
        y = xv * lax.rsqrt(ms + EPS) * gpre_ref[...]
        mod = modr[...]
        h_ref[slot] = (y * (1.0 + mod[:, D_MODEL:2 * D_MODEL]) + mod[:, :D_MODEL]).astype(BF16)

    @pl.when(step == 0)
    def _():
        modulated_norm(x_ref, mod_ref, 0)

    def proj(off, width=COL_BLK):
        for ref, base in ((w_rank_ref, OFF_R), (w_hi_ref, OFF_Z_GLA), (w_lo_ref, 0)):
            if off >= base:
                return _dot(h_ref[cur], ref[:, off - base:off - base + width])

    for o in range(0, CONV_W if "ua" in out else 0, COL_BLK):
        cols = slice(o, o + COL_BLK)
        out["ua"][:, cols] = (proj(OFF_A_C + o) * proj(OFF_A_X + o)).astype(BF16)
        out["wa"][:, cols] = (proj(OFF_A_B + o) * _silu(proj(OFF_A_Z + o))).astype(BF16)
        if o == 0:
            modulated_norm(xn_ref, modn_ref, 1 - cur)

    def heads(off, width, gain, out_ref, out_scale):
        for o in range(0, width, COL_BLK):
            blk = proj(off + o)
            for hh in range(COL_BLK // HEAD_DIM):
                xh = _head_norm(blk[:, hh * HEAD_DIM:(hh + 1) * HEAD_DIM], gain)
                if use_rope:
                    xh = _rope(xh, cos_ref[...], sin_ref[...])
                c0 = o + hh * HEAD_DIM
                out_ref[:, c0:c0 + HEAD_DIM] = (xh * out_scale).astype(BF16)

    if "q" in out:
        heads(OFF_Q, Q_W, qg_ref[...], out["q"], ATTN_SCALE * LOG2_E)
    heads(OFF_K, KV_W, kg_ref[...], out["k"], 1.0)
    if "ua" not in out:
        modulated_norm(xn_ref, modn_ref, 1 - cur)
    v_ref[...] = proj(OFF_V, KV_W)
    ones_tile = jnp.where(lax.broadcasted_iota(jnp.int32, (BF16_ROWS, x_ref.shape[0]), 0) == 0, 1.0, 0.0)
    for hh in range(N_KV_HEADS):
        out["vt"][hh * VT_ROWS:hh * VT_ROWS + HEAD_DIM, :] = (
            v_ref[:, hh * HEAD_DIM:(hh + 1) * HEAD_DIM].T.astype(BF16))
        out["vt"][hh * VT_ROWS + HEAD_DIM:(hh + 1) * VT_ROWS, :] = ones_tile.astype(BF16)
    for o in range(0, Q_W if "sza" in out else 0, COL_BLK):
        out["sza"][:, o:o + COL_BLK] = _silu(proj(OFF_Z_ATT + o)).astype(BF16)

    for o in range(0, GLA_DK, COL_BLK):
        if "gq" in out:
            out["gq"][:, o:o + COL_BLK] = (proj(OFF_GQ + o) * (GLA_DKH ** -0.5)).astype(BF16)
        out["gk"][:, o:o + COL_BLK] = proj(OFF_GK + o).astype(BF16)
    for o in range(0, GLA_DV if "szg" in out else 0, COL_BLK):
        out["szg"][:, o:o + COL_BLK] = _silu(proj(OFF_Z_GLA + o)).astype(BF16)

    r = proj(OFF_R, LANES).astype(BF16)

    def decay_piece(out_ref, o, wcol):
        out_ref[:, o:o + COL_BLK] = _log_sigmoid(
            _dot(r, wdec_ref[:, wcol:wcol + COL_BLK]) + bdec_ref[:, wcol:wcol + COL_BLK]
        ) * (LOG2_E / GLA_TAU)

    decay_pieces = ([(out["laf"], o, o) for o in range(0, GLA_DK, COL_BLK)]
                    + [(out["lab"], o, GLA_DK + o) for o in range(0, GLA_DK, COL_BLK)])
    assert len(decay_pieces) <= GLA_DV // COL_BLK
    for i, o in enumerate(range(0, GLA_DV, COL_BLK)):
        out["gv"][:, o:o + COL_BLK] = proj(OFF_GV + o).astype(BF16)
        if i < len(decay_pieces):
            decay_piece(*decay_pieces[i])
    if "h" in out:
        out["h"][...] = h_ref[cur]


def _projection(x2, mod3, mod_row0, seq_len, gpre, w_parts, wdec, bdec, qg, kg, cos_t, sin_t, use_rope,
                outs=PROJ_OUTS):
    n = x2.shape[0]
    tm = min(PROJ_TM, seq_len)
    tps = seq_len // tm
    const = lambda i: (0, 0)
    row = lambda i: (i, 0)
    pos = lambda i: (i % tps, 0)
    nxt = lambda i: jnp.minimum(i + 1, n // tm - 1)
    if mod_row0 is None:
        mod_map = lambda i: (i // tps, 0, 0)
    else:
        mod_map = lambda i: (mod_row0, 0, 0)
    transposed = ("vt",)
    return pl.pallas_call(
        functools.partial(_proj_kernel, use_rope=use_rope, outs=outs),
        out_shape=[jax.ShapeDtypeStruct((w, n) if name in transposed else (n, w), dt)
                   for name, w, dt in outs],
        grid=(n // tm,),
        in_specs=[
            pl.BlockSpec((tm, D_MODEL), row),
            pl.BlockSpec((None, 1, 3 * D_MODEL), mod_map),
            pl.BlockSpec((tm, D_MODEL), lambda i: (nxt(i), 0)),
            pl.BlockSpec((None, 1, 3 * D_MODEL), lambda i: mod_map(nxt(i))),
            pl.BlockSpec((1, D_MODEL), const),
            pl.BlockSpec((D_MODEL, OFF_Z_GLA), const, pipeline_mode=pl.Buffered(1)),
            pl.BlockSpec((D_MODEL, OFF_R - OFF_Z_GLA), const, pipeline_mode=pl.Buffered(1)),
            pl.BlockSpec((D_MODEL, LANES), const, pipeline_mode=pl.Buffered(1)),
            pl.BlockSpec((LANES, 2 * GLA_DK), const),
            pl.BlockSpec((1, 2 * GLA_DK), const),
            pl.BlockSpec((1, HEAD_DIM), const),
            pl.BlockSpec((1, HEAD_DIM), const),
            pl.BlockSpec((tm, HEAD_DIM), pos),
            pl.BlockSpec((tm, HEAD_DIM), pos),
        ],
        out_specs=[pl.BlockSpec((w, tm), lambda i: (0, i)) if name in transposed
                   else pl.BlockSpec((tm, w), row) for name, w, _ in outs],
        scratch_shapes=[pltpu.VMEM((2, tm, D_MODEL), BF16), pltpu.VMEM((tm, KV_W), F32)],
        compiler_params=_params("arbitrary"),
        name="in_proj_rope" if use_rope else "in_proj",
    )(x2, mod3, x2, mod3, gpre, *w_parts, wdec, bdec, qg, kg, cos_t, sin_t)


def _attn_stages(tq, m_ref, alpha_ref, acc_ref):
    heads = [slice(g * tq, (g + 1) * tq) for g in range(GROUP)]

    def step(scores=(), values=(), softmax=()):
        for q_src, k, s_dst, smax_dst in scores:
            for g, cols in enumerate(heads):
                s = _dot_nt(k, q_src[:, g * HEAD_DIM:(g + 1) * HEAD_DIM])
                s_dst[:, cols] = s
                smax_dst[:, cols] = jnp.max(s, axis=0, keepdims=True)
        for p_src, vt, slot in values:
            for cols in heads:
                acc_ref[slot, :, cols] = (alpha_ref[slot, :, cols] * acc_ref[slot, :, cols]
                                          + _dot(vt, p_src[:, cols]))
        for s_src, smax_src, p_dst, slot in softmax:
            for cols in heads:
                m_old = m_ref[slot, :, cols]
                m_new = jnp.maximum(m_old, smax_src[:, cols])
                p_dst[:, cols] = jnp.exp2(s_src[:, cols] - m_new).astype(BF16)
                alpha_ref[slot, :, cols] = jnp.exp2(m_old - m_new)
                m_ref[slot, :, cols] = m_new

    def init(slot):
        m_ref[slot] = jnp.full(m_ref.shape[1:], -jnp.inf, F32)
        acc_ref[slot] = jnp.zeros(acc_ref.shape[1:], F32)

    def finish(slot, o_ref):
        out_t = acc_ref[slot, :HEAD_DIM, :] / acc_ref[slot, HEAD_DIM:HEAD_DIM + 1, :]
        for g in range(GROUP):
            o_ref[:, g * HEAD_DIM:(g + 1) * HEAD_DIM] = out_t[:, g * tq:(g + 1) * tq].T.astype(BF16)

    return step, init, finish


def _attn_ctx_kernel(q_ref, kc_ref, vtc_ref, o_ref, m_ref, alpha_ref, acc_ref, mc_ref, sc_ref, pc_ref):
    step, init, finish = _attn_stages(q_ref.shape[0], m_ref, alpha_ref, acc_ref)
    init(0)
    step(scores=[(q_ref, kc_ref[...], sc_ref, mc_ref)])
    step(softmax=[(sc_ref, mc_ref, pc_ref, 0)])
    step(values=[(pc_ref, vtc_ref[...], 0)])
    finish(0, o_ref)


def _attn_kernel(q_ref, qn_ref, kl_ref, vtl_ref, kc_ref, vtc_ref, o_ref, m_ref, alpha_ref, acc_ref,
                 mc_ref, sc_ref, pc_ref, ma_ref, mb_ref, sa_ref, sb_ref, pa_ref, pb_ref, *, n_lat_blocks):
    tile = pl.program_id(2)
    cur = tile % 2
    nxt = 1 - cur
    n = n_lat_blocks
    step, init, finish = _attn_stages(q_ref.shape[0], m_ref, alpha_ref, acc_ref)
    buf_a, buf_b, buf_c = (sa_ref, ma_ref), (sb_ref, mb_ref), (sc_ref, mc_ref)

    def keys(j):
        return pl.ds(pl.multiple_of(j * ATT_KV, ATT_KV), ATT_KV)

    @pl.when(tile == 0)
    def _():
        init(cur)
        step(scores=[(q_ref, kl_ref[keys(0), :], *buf_a)])
        step(scores=[(q_ref, kl_ref[keys(1), :], *buf_b)], softmax=[(*buf_a, pa_ref, cur)])

    def pair(i, carry):
        t = 2 * i
        step(scores=[(q_ref, kl_ref[keys(t), :], *buf_a)], values=[(pa_ref, vtl_ref[:, keys(t - 2)], cur)],
             softmax=[(*buf_b, pb_ref, cur)])
        step(scores=[(q_ref, kl_ref[keys(t + 1), :], *buf_b)], values=[(pb_ref, vtl_ref[:, keys(t - 1)], cur)],
             softmax=[(*buf_a, pa_ref, cur)])
        return carry

    lax.fori_loop(1, n // 2, pair, 0)
    init(nxt)
    step(scores=[(q_ref, kc_ref[...], *buf_c), (qn_ref, kl_ref[keys(0), :], *buf_a)],
         values=[(pa_ref, vtl_ref[:, keys(n - 2)], cur)], softmax=[(*buf_b, pb_ref, cur)])
    step(scores=[(qn_ref, kl_ref[keys(1), :], *buf_b)],
         values=[(pb_ref, vtl_ref[:, keys(n - 1)], cur)],
         softmax=[(*buf_c, pc_ref, cur), (*buf_a, pa_ref, nxt)])
    step(values=[(pc_ref, vtc_ref[...], cur)])
    finish(cur, o_ref)


def _attention(q2, k_lat, vt_lat, k_ctx, vt_ctx, batch, q_len, lat_len, ctx_len):
    tq = min(ATT_TQ, q_len)
    tiles = q_len // tq
    rows = GROUP * tq
    qmap = lambda b, kv, i: (b * tiles + i, kv)
    qnext = lambda b, kv, i: (b * tiles + jnp.minimum(i + 1, tiles - 1), kv)
    kmap = lambda b, kv, i: (b, kv)
    vtmap = lambda b, kv, i: (kv, b)
    q_spec = pl.BlockSpec((tq, GROUP * HEAD_DIM), qmap)
    ctx_specs = [pl.BlockSpec((ctx_len, HEAD_DIM), kmap), pl.BlockSpec((VT_ROWS, ctx_len), vtmap)]
    stat = pltpu.VMEM((1, rows), F32)
    if k_lat is None:
        slots = 1
        body, name = _attn_ctx_kernel, "gqa_ctx"
        in_specs, args = [q_spec] + ctx_specs, [q2, k_ctx, vt_ctx]
        buffers = []
        semantics = ("parallel", "parallel", "parallel")
    else:
        slots = 2
        n_lat_blocks = lat_len // ATT_KV
        assert n_lat_blocks % 2 == 0 and n_lat_blocks >= 2
        body, name = functools.partial(_attn_kernel, n_lat_blocks=n_lat_blocks), "gqa_lat"
        in_specs = ([q_spec, pl.BlockSpec((tq, GROUP * HEAD_DIM), qnext),
                     pl.BlockSpec((lat_len, HEAD_DIM), kmap), pl.BlockSpec((VT_ROWS, lat_len), vtmap)]
                    + ctx_specs)
        args = [q2, q2, k_lat, vt_lat, k_ctx, vt_ctx]
        buffers = [stat, stat] + [pltpu.VMEM((ATT_KV, rows), F32)] * 2 + [pltpu.VMEM((ATT_KV, rows), BF16)] * 2
        semantics = ("parallel", "parallel", "arbitrary")
    scratch = [pltpu.VMEM((slots, 1, rows), F32), pltpu.VMEM((slots, 1, rows), F32),
               pltpu.VMEM((slots, VT_ROWS, rows), F32), stat,
               pltpu.VMEM((ctx_len, rows), F32), pltpu.VMEM((ctx_len, rows), BF16)] + buffers
    return pl.pallas_call(
        body,
        out_shape=jax.ShapeDtypeStruct(q2.shape, BF16),
        grid=(batch, N_KV_HEADS, tiles),
        in_specs=in_specs,
        out_specs=q_spec,
        scratch_shapes=scratch,
        compiler_params=_params(*semantics),
        name=name,
    )(*args)


def _split2(x):
    hi = x.astype(BF16)
    lo = (x - hi.astype(F32)).astype(BF16)
    return hi, lo


def _gla_consts(rev):
    c = GLA_C
    tri_r = lax.broadcasted_iota(jnp.int32, (c, c), 0)
    tri_c = lax.broadcasted_iota(jnp.int32, (c, c), 1)
    tri = jnp.where((tri_c >= tri_r) if rev else (tri_c <= tri_r), 1.0, 0.0).astype(BF16)
    row = lax.broadcasted_iota(jnp.int32, (c, LANES), 0)
    lane = lax.broadcasted_iota(jnp.int32, (c, LANES), 1)
    level_masks = []
    for i, m in enumerate(GLA_LEVELS):
        key = lane - c * (i % 2)
        in_half = (key >= 0) & (key < c)
        same = (row // (2 * m)) == (key // (2 * m))
        q_upper = (row % (2 * m)) >= m
        k_upper = (key % (2 * m)) >= m
        pair = (~q_upper & k_upper) if rev else (q_upper & ~k_upper)
        level_masks.append(in_half & same & pair)
    sub = row % GLA_FINE
    d_of_lane = lane if rev else (LANES - lane) % LANES
    ok = (sub + d_of_lane < GLA_FINE) if rev else (sub >= d_of_lane)
    lane_code = jnp.where((d_of_lane < GLA_FINE) & ok, d_of_lane, -1)
    return tri, level_masks, lane_code


def _gla_kernel(*refs, with_outputs):
    if with_outputs:
        (qf_ref, kf_ref, vf_ref, laf_ref, qb_ref, kb_ref, vb_ref, lab_ref, s0f_ref, s0b_ref,
         of_ref, ob_ref, sff_ref, sfb_ref, st_ref) = refs
    else:
        kf_ref, vf_ref, laf_ref, kb_ref, vb_ref, lab_ref, s0f_ref, s0b_ref, sff_ref, sfb_ref, st_ref = refs
        qf_ref = qb_ref = of_ref = ob_ref = None

    @pl.when(pl.program_id(1) == 0)
    def _():
        st_ref[0] = s0f_ref[...]
        st_ref[1] = s0b_ref[...]

    c = GLA_C
    n_chunks = kf_ref.shape[0] // c
    dirs = ((0, False, qf_ref, kf_ref, vf_ref, laf_ref, of_ref, _gla_consts(False)),
            (1, True, qb_ref, kb_ref, vb_ref, lab_ref, ob_ref, _gla_consts(True)))

    def body(ci, carry):
        chains = []
        for di, rev, q_ref, k_ref, v_ref, la_ref, o_ref, consts in dirs:
            cc = (n_chunks - 1 - ci) if rev else ci
            rows = pl.ds(pl.multiple_of(cc * c, c), c)
            cum2 = _dot(consts[0], jnp.concatenate(_split2(la_ref[rows, :]), axis=1))
            b_all = cum2[:, :GLA_DK] + cum2[:, GLA_DK:]
            for hh in range(GLA_HEADS):
                kc = slice(hh * GLA_DKH, (hh + 1) * GLA_DKH)
                vc = slice(hh * GLA_DVH, (hh + 1) * GLA_DVH)
                chains.append(dict(di=di, hh=hh, rev=rev, rows=rows, vc=vc, o_ref=o_ref,
                                   masks=consts[1], lane_code=consts[2], b=b_all[:, kc],
                                   qf=q_ref[rows, kc].astype(F32) if with_outputs else None,
                                   kf=k_ref[rows, kc].astype(F32), v=v_ref[rows, vc]))

        for w in chains:
            b, qf, kf, rev = w["b"], w["qf"], w["kf"], w["rev"]
            st = st_ref[w["di"], w["hh"]]
            tot = b[0:1, :] if rev else b[c - 1:c, :]
            kd = (kf * jnp.exp2(tot - b)).astype(BF16)
            st_ref[w["di"], w["hh"]] = st * jnp.exp2(tot) + _dot_tn(w["v"], kd)
            if not with_outputs:
                continue
            w["o"] = _dot_nt((qf * jnp.exp2(b)).astype(BF16), st.astype(BF16))
            qs, ks = [], []
            for m in GLA_LEVELS:
                pivot = (m - 1) if rev else m
                bm = b.reshape(c // (2 * m), 2 * m, GLA_DKH)
                f = jnp.exp2(-jnp.abs(bm - bm[:, pivot:pivot + 1, :])).reshape(c, GLA_DKH)
                qs.append((qf * f).astype(BF16))
                ks.append((kf * f).astype(BF16))
            w["coarse"] = _dot_nt(jnp.concatenate(qs, axis=0), jnp.concatenate(ks, axis=0))

        grouped = (c // SUBLANES, SUBLANES, GLA_DKH)
        for w in chains if with_outputs else ():
            qf, kf, rev, lane_code = w["qf"], w["kf"], w["rev"], w["lane_code"]
            w_slots = jnp.where(lane_code == 0, jnp.sum(qf * kf, axis=-1, keepdims=True), 0.0)
            q3, k3, b3 = qf.reshape(grouped), kf.reshape(grouped), w["b"].reshape(grouped)
            for d in range(1, GLA_FINE):
                shift = (SUBLANES - d) if rev else d
                kr = pltpu.roll(k3, shift, 1)
                br = pltpu.roll(b3, shift, 1)
                wd = jnp.sum(q3 * kr * jnp.exp2(jnp.minimum(b3 - br, 0.0)), axis=-1, keepdims=True)
                w_slots = jnp.where(lane_code == d, wd.reshape(c, 1), w_slots)
            w["att"] = pltpu.roll(w_slots, 0, 1, stride=1, stride_axis=0)

        for w in chains if with_outputs else ():
            att = w["att"]
            for i, mask in enumerate(w["masks"]):
                col0 = (i * c) // LANES * LANES
                att = jnp.where(mask, w["coarse"][i * c:(i + 1) * c, col0:col0 + LANES], att)
            o = w["o"] + _dot(att.astype(BF16), jnp.concatenate([w["v"], w["v"]], axis=0))
            w["o_ref"][w["rows"], w["vc"]] = o.astype(BF16)
        return carry

    lax.fori_loop(0, n_chunks, body, 0)
    sff_ref[...] = st_ref[0]
    sfb_ref[...] = st_ref[1]


def _gla_scan(gq, gk, gv, la_f, la_b, s0_f, s0_b, batch, seq_len):
    rows = min(GLA_ROWS, seq_len)
    nblk = seq_len // rows
    fwd = lambda b, i: (b * nblk + i, 0)
    bwd = lambda b, i: (b * nblk + (nblk - 1 - i), 0)
    smap = lambda b, i: (b, 0, 0, 0)
    state_spec = pl.BlockSpec((None, GLA_HEADS, GLA_DVH, GLA_DKH), smap)
    state_shape = jax.ShapeDtypeStruct((batch, GLA_HEADS, GLA_DVH, GLA_DKH), F32)
    with_outputs = gq is not None

    def view(index_map, la):
        arrays = [(gq, GLA_DK)] * with_outputs + [(gk, GLA_DK), (gv, GLA_DV), (la, GLA_DK)]
        return [a for a, _ in arrays], [pl.BlockSpec((rows, w), index_map) for _, w in arrays]

    (args_f, specs_f), (args_b, specs_b) = view(fwd, la_f), view(bwd, la_b)
    o_shape = [jax.ShapeDtypeStruct(gv.shape, BF16)] * 2 if with_outputs else []
    o_specs = [pl.BlockSpec((rows, GLA_DV), fwd), pl.BlockSpec((rows, GLA_DV), bwd)] if with_outputs else []
    return pl.pallas_call(
        functools.partial(_gla_kernel, with_outputs=with_outputs),
        out_shape=o_shape + [state_shape, state_shape],
        grid=(batch, nblk),
        in_specs=specs_f + specs_b + [state_spec, state_spec],
        out_specs=o_specs + [state_spec, state_spec],
        scratch_shapes=[pltpu.VMEM((2, GLA_HEADS, GLA_DVH, GLA_DKH), F32)],
        compiler_params=_params("parallel", "arbitrary"),
        name="gla_bidir" if with_outputs else "gla_states",
    )(*args_f, *args_b, s0_f, s0_b)


def _merge_kernel(ua_ref, uprev_ref, unext_ref, wa_ref, att_ref, sza_ref, of_ref, ob_ref, szg_ref,
                  h_ref, x_ref, mod_ref, convw_ref, glag_ref, gpost_ref, bgate_ref,
                  wconv_ref, watt_ref, wgla_ref, wout_ref, wgate_ref, o_ref, *, tiles_per_seq):
    tm = x_ref.shape[0]
    ti = pl.program_id(0) % tiles_per_seq
    rows_per = tm // MERGE_STREAMS
    groups = [slice(g * rows_per, (g + 1) * rows_per) for g in range(MERGE_STREAMS)]
    rid = lax.broadcasted_iota(jnp.int32, (rows_per, 1), 0)
    val = [dict() for _ in groups]

    def conv_stage(g):
        rows = groups[g]
        u = ua_ref[rows, :].astype(F32)
        if g == 0:
            prev_row = jnp.where(ti == 0, 0.0, uprev_ref[BF16_ROWS - 1:BF16_ROWS, :].astype(F32))
        else:
            prev_row = ua_ref[rows.start - 1:rows.start, :].astype(F32)
        if g == MERGE_STREAMS - 1:
            next_row = jnp.where(ti == tiles_per_seq - 1, 0.0, unext_ref[0:1, :].astype(F32))
        else:
            next_row = ua_ref[rows.stop:rows.stop + 1, :].astype(F32)
        u_prev = jnp.where(rid == 0, prev_row, pltpu.roll(u, 1, 0))
        u_next = jnp.where(rid == rows_per - 1, next_row, pltpu.roll(u, rows_per - 1, 0))
        cw = convw_ref[...]
        conv = cw[0:1, :] * u_prev + cw[1:2, :] * u + cw[2:3, :] * u_next
        val[g]["a_in"] = (wa_ref[rows, :].astype(F32) * conv).astype(BF16)

    def branch_ab_stage(g):
        rows = groups[g]
        val[g]["br_a"] = _dot(val[g]["a_in"], wconv_ref[...])
        val[g]["br_b"] = _dot(att_ref[rows, :] * sza_ref[rows, :], watt_ref[...])

    def gla_norm_stage(g):
        rows = groups[g]
        parts = []
        for hh in range(GLA_HEADS):
            cols = slice(hh * GLA_DVH, (hh + 1) * GLA_DVH)
            oh = of_ref[rows, cols].astype(F32) + ob_ref[rows, cols].astype(F32)
            parts.append((_head_norm(oh, glag_ref[...]) * szg_ref[rows, cols].astype(F32)).astype(BF16))
        val[g]["c_in"] = jnp.concatenate(parts, axis=1)

    def branch_c_gate_stage(g):
        rows = groups[g]
        val[g]["br_c"] = _dot(val[g]["c_in"], wgla_ref[...])
        val[g]["gates"] = [
            _sigmoid(_dot(h_ref[rows, :], wgate_ref[:, i * D_MODEL:(i + 1) * D_MODEL])
                     + bgate_ref[:, i * D_MODEL:(i + 1) * D_MODEL]) for i in range(N_BRANCH)]

    def out_stage(g):
        v = val[g]
        merged = v["gates"][0] * v["br_a"] + v["gates"][1] * v["br_b"] + v["gates"][2] * v["br_c"]
        v["out"] = _dot(merged.astype(BF16), wout_ref[...])

    def residual_stage(g):
        rows = groups[g]
        o_ref[rows, :] = x_ref[rows, :] + mod_ref[:, 2 * D_MODEL:] * _head_norm(val[g]["out"], gpost_ref[...])

    stages = (conv_stage, branch_ab_stage, gla_norm_stage, branch_c_gate_stage, out_stage, residual_stage)
    for t in range(len(stages) + MERGE_STREAMS - 1):
        for g in range(MERGE_STREAMS):
            if 0 <= t - g < len(stages):
                stages[t - g](g)


def _merge(ua, wa, att, sza, o_f, o_b, szg, h, x2, mod3, mod_row0, seq_len,
           convw, glag, gpost, bgate, wconv, watt, wgla, wout, wgate):
    n = x2.shape[0]
    tm = min(MERGE_TM, seq_len)
    tps = seq_len // tm
    halo = tm // BF16_ROWS
    n_halo = n // BF16_ROWS
    const = lambda i: (0, 0)
    row = lambda i: (i, 0)
    if mod_row0 is None:
        mod_map = lambda i: (i // tps, 0, 0)
    else:
        mod_map = lambda i: (mod_row0, 0, 0)
    tok = lambda w: pl.BlockSpec((tm, w), row)
    wspec = pl.BlockSpec((D_MODEL, D_MODEL), const, pipeline_mode=pl.Buffered(1))
    return pl.pallas_call(
        functools.partial(_merge_kernel, tiles_per_seq=tps),
        out_shape=jax.ShapeDtypeStruct((n, D_MODEL), F32),
        grid=(n // tm,),
        in_specs=[
            tok(CONV_W),
            pl.BlockSpec((BF16_ROWS, CONV_W), lambda i: (jnp.maximum(i * halo - 1, 0), 0)),
            pl.BlockSpec((BF16_ROWS, CONV_W), lambda i: (jnp.minimum((i + 1) * halo, n_halo - 1), 0)),
            tok(CONV_W), tok(Q_W), tok(Q_W), tok(GLA_DV), tok(GLA_DV), tok(GLA_DV),
            tok(D_MODEL), tok(D_MODEL),
            pl.BlockSpec((None, 1, 3 * D_MODEL), mod_map),
            pl.BlockSpec((3, CONV_W), const),
            pl.BlockSpec((1, GLA_DVH), const),
            pl.BlockSpec((1, D_MODEL), const),
            pl.BlockSpec((1, N_BRANCH * D_MODEL), const),
            wspec, wspec, wspec, wspec,
            pl.BlockSpec((D_MODEL, N_BRANCH * D_MODEL), const, pipeline_mode=pl.Buffered(1)),
        ],
        out_specs=tok(D_MODEL),
        compiler_params=_params("parallel"),
        name="merge_out",
    )(ua, ua, ua, wa, att, sza, o_f, o_b, szg, h, x2, mod3, convw, glag, gpost, bgate,
      wconv, watt, wgla, wout, wgate)


def _rope_tables(n_tokens):
    n_rows = n_tokens // GRID_W
    row = np.repeat(np.arange(n_rows, dtype=np.float32), GRID_W)
    col = np.tile(np.arange(GRID_W, dtype=np.float32), n_rows)
    freqs = (np.float32(ROPE_THETA) ** (-np.arange(ROPE_PAIRS, dtype=np.float32) * np.float32(2.0)
                                        / np.float32(ROPE_AXIS_DIM))).astype(np.float32)
    ar, ac = row[:, None] * freqs, col[:, None] * freqs
    cos_t = np.concatenate([np.cos(ar), np.cos(ar), np.cos(ac), np.cos(ac)], axis=1)
    sin_t = np.concatenate([-np.sin(ar), np.sin(ar), -np.sin(ac), np.sin(ac)], axis=1)
    return jnp.asarray(cos_t, F32), jnp.asarray(sin_t, F32)


def _split_w_in(w):
    wb = w.astype(BF16)
    tail = ORIG_R + 2 * GLA_RANK
    low_rank = jnp.pad(wb[:, ORIG_R:tail], ((0, 0), (0, LANES - 2 * GLA_RANK)))
    return wb, wb[:, tail:ORIG_MG], low_rank, wb[:, ORIG_MG:]


def _pack_decay(w_f, b_f, w_b, b_b):
    wd = jnp.zeros((LANES, 2 * GLA_DK), F32)
    wd = wd.at[:GLA_RANK, :GLA_DK].set(w_f).at[GLA_RANK:2 * GLA_RANK, GLA_DK:].set(w_b)
    return wd.astype(BF16), jnp.concatenate([b_f, b_b])[None, :]


def kernel(x, c, ctx, c_ctx, w_ada, b_ada, g_pre, g_post, w_in, conv_w, q_norm_g, k_norm_g,
           w_decay_fwd, b_decay_fwd, w_decay_bwd, b_decay_bwd, gla_norm_g,
           w_br_conv, w_br_attn, w_br_gla, b_gate, w_out):
    batch, seq, _ = x.shape
    ctx_len = ctx.shape[1]
    assert seq % max(PROJ_TM, MERGE_TM, ATT_TQ, ATT_KV, GLA_ROWS) == 0 and seq % GRID_W == 0
    assert ctx_len % GLA_C == 0 and ctx_len % BF16_ROWS == 0
    assert PROJ_TM % ctx_len == 0 and MERGE_TM % ctx_len == 0 and ATT_TQ % ctx_len == 0

    mod_rows = -(-(batch + 1) // SUBLANES) * SUBLANES
    cvec = jnp.zeros((mod_rows, D_MODEL), F32).at[:batch].set(c).at[batch].set(c_ctx)
    mod = _modulation(cvec, w_ada, b_ada)
    cos_t, sin_t = _rope_tables(seq)
    zero_state = jnp.zeros((batch, GLA_HEADS, GLA_DVH, GLA_DKH), F32)

    xl = x.reshape(batch * seq, D_MODEL)
    xc = ctx.reshape(batch * ctx_len, D_MODEL)
    for l in range(DEPTH):
        last = l == DEPTH - 1
        mod3 = mod[l][:, None, :]
        *w_parts, w_gate = _split_w_in(w_in[l])
        wdec, bdec = _pack_decay(w_decay_fwd[l], b_decay_fwd[l], w_decay_bwd[l], b_decay_bwd[l])
        shared_in = (g_pre[l][None], w_parts, wdec, bdec, q_norm_g[l][None], k_norm_g[l][None])
        pl_ = _projection(xl, mod3, None, seq, *shared_in, cos_t, sin_t, use_rope=True)
        (ua_l, wa_l, q_l, k_l, vt_l, sza_l, gq_l, gk_l, gv_l, laf_l, lab_l, szg_l, h_l) = pl_
        if last:
            k_c, vt_c, gk_c, gv_c, laf_c, lab_c = _projection(
                xc, mod3, batch, ctx_len, *shared_in, cos_t, sin_t, use_rope=False, outs=PROJ_STATE_OUTS)
            s_f, s_b = _gla_scan(None, gk_c, gv_c, laf_c, lab_c, zero_state, zero_state, batch, ctx_len)
        else:
            pc = _projection(xc, mod3, batch, ctx_len, *shared_in, cos_t, sin_t, use_rope=False)
            (ua_c, wa_c, q_c, k_c, vt_c, sza_c, gq_c, gk_c, gv_c, laf_c, lab_c, szg_c, h_c) = pc
            of_c, ob_c, s_f, s_b = _gla_scan(gq_c, gk_c, gv_c, laf_c, lab_c, zero_state, zero_state,
                                             batch, ctx_len)
        att_l = _attention(q_l, k_l, vt_l, k_c, vt_c, batch, seq, seq, ctx_len)
        of_l, ob_l, _, _ = _gla_scan(gq_l, gk_l, gv_l, laf_l, lab_l, s_f, s_b, batch, seq)

        shared_out = (conv_w[l], gla_norm_g[l][None], g_post[l][None], b_gate[l][None],
                      w_br_conv[l].astype(BF16), w_br_attn[l].astype(BF16),
                      w_br_gla[l].astype(BF16), w_out[l].astype(BF16), w_gate)
        if not last:
            att_c = _attention(q_c, None, None, k_c, vt_c, batch, ctx_len, 0, ctx_len)
            xc = _merge(ua_c, wa_c, att_c, sza_c, of_c, ob_c, szg_c, h_c, xc, mod3, batch, ctx_len,
                        *shared_out)
        xl = _merge(ua_l, wa_l, att_l, sza_l, of_l, ob_l, szg_l, h_l, xl, mod3, None, seq,
                    *shared_out)
    return xl.reshape(batch, seq, D_MODEL)
```
